```python
import jax, jax.numpy as jnp
from jax import lax
import numpy as np

D_MODEL = 1024
BATCH = 2
SEQ = 8192
DEPTH = 1

D_MIX = D_MODEL
MLA_HEADS = 8
QK_NOPE = 64
QK_ROPE = 32
V_HEAD = 64
Q_LORA = 256
KV_LORA = 128
ROPE_THETA = 10000.0
Q_BLOCK = 128
SG_HEADS = 8
SG_HEAD_DIM = 64
SG_CHUNK = 128
N_EXPERTS = 32
TOP_K = 4
D_EXPERT = D_MODEL
SWIGLU_LIMIT = 7.0
SWIGLU_ALPHA = 1.702
MOE_BLOCK = 128
D_PLE = 256
EPS = 1e-6
IN_COLS = Q_LORA + KV_LORA + QK_ROPE + 2 * SG_HEADS * SG_HEAD_DIM
MLA_WIDTH = MLA_HEADS * V_HEAD
SG_WIDTH = SG_HEADS * SG_HEAD_DIM

kernel_name = "hymba_mla_sgmlp_moe_sandwich_ple"


def rmsnorm(x, g):
    xf = x.astype(jnp.float32)
    y = xf * lax.rsqrt(jnp.mean(xf * xf, axis=-1, keepdims=True) + EPS)
    return (y * g.astype(jnp.float32)).astype(x.dtype)


def rope_tables(positions):
    inv_freq = 1.0 / (ROPE_THETA ** (jnp.arange(0, QK_ROPE, 2, dtype=jnp.float32) / QK_ROPE))
    ang = positions.astype(jnp.float32)[..., None] * inv_freq
    return jnp.cos(ang), jnp.sin(ang)


def apply_rope(x, cos, sin):
    xf = x.astype(jnp.float32)
    half = xf.shape[-1] // 2
    x1, x2 = xf[..., :half], xf[..., half:]
    return jnp.concatenate([x1 * cos - x2 * sin, x2 * cos + x1 * sin], axis=-1).astype(x.dtype)


def mla_attention(c_q, c_kv, k_rope, cos, sin, q_norm_g, w_uq, kv_norm_g, w_ukv):
    B, S, _ = c_q.shape
    q = jnp.einsum('bsr,rhd->bshd', rmsnorm(c_q, q_norm_g), w_uq)
    q = jnp.concatenate([q[..., :QK_NOPE], apply_rope(q[..., QK_NOPE:], cos[:, :, None], sin[:, :, None])], axis=-1)
    kv = jnp.einsum('bsr,rhd->bshd', rmsnorm(c_kv, kv_norm_g), w_ukv)
    k_nope, v = kv[..., :QK_NOPE], kv[..., QK_NOPE:]
    k_r = apply_rope(k_rope, cos, sin)
    k = jnp.concatenate([k_nope, jnp.broadcast_to(k_r[:, :, None], (B, S, MLA_HEADS, QK_ROPE))], axis=-1)
    scale = (QK_NOPE + QK_ROPE) ** -0.5
    nqb = S // Q_BLOCK
    qb = q.reshape(B, nqb, Q_BLOCK, MLA_HEADS, QK_NOPE + QK_ROPE).transpose(1, 0, 2, 3, 4)
    key_pos = jnp.arange(S)

    def query_block(args):
        i, q_blk = args
        s = jnp.einsum('bqhd,bkhd->bhqk', q_blk, k).astype(jnp.float32) * scale
        q_pos = i * Q_BLOCK + jnp.arange(Q_BLOCK)
        s = jnp.where(key_pos[None, :] <= q_pos[:, None], s, -jnp.inf)
        pr = jax.nn.softmax(s, axis=-1).astype(v.dtype)
        return jnp.einsum('bhqk,bkhd->bqhd', pr, v)

    o = lax.map(query_block, (jnp.arange(nqb), qb))
    return o.transpose(1, 0, 2, 3, 4).reshape(B, S, MLA_WIDTH)


def spatial_gating(z, v_norm_g, w_s, b_s):
    B, S, _ = z.shape
    nc = S // SG_CHUNK
    z = jax.nn.gelu(z)
    u, v = jnp.split(z, 2, axis=-1)
    u = u.reshape(B, nc, SG_CHUNK, SG_HEADS, SG_HEAD_DIM)
    v = rmsnorm(v.reshape(B, nc, SG_CHUNK, SG_HEADS, SG_HEAD_DIM), v_norm_g)
    causal = jnp.tril(jnp.ones((SG_CHUNK, SG_CHUNK), dtype=bool))
    w = jnp.where(causal[None], w_s, jnp.zeros_like(w_s))
    vm = jnp.einsum('hts,bnshc->bnthc', w, v) + b_s.T[None, None, :, :, None]
    return (u * vm).reshape(B, S, SG_WIDTH)


def moe(h, w_router, b_router, w_gate_up, b_gate_up, w_down, b_down):
    B, S, D = h.shape
    N = B * S
    xt = h.reshape(N, D)
    logits = (xt @ w_router + b_router).astype(jnp.float32)
    top_val, top_idx = lax.top_k(logits, TOP_K)
    gate = jax.nn.softmax(top_val, axis=-1)
    A = N * TOP_K
    e_flat = top_idx.reshape(A)
    tok_flat = jnp.repeat(jnp.arange(N, dtype=jnp.int32), TOP_K)
    w_flat = gate.reshape(A)
    order = jnp.argsort(e_flat)
    e_sorted, tok_sorted, w_sorted = e_flat[order], tok_flat[order], w_flat[order]
    counts = jnp.zeros((N_EXPERTS,), jnp.int32).at[e_flat].add(1)
    start = jnp.cumsum(counts) - counts
    padded = (counts + MOE_BLOCK - 1) // MOE_BLOCK * MOE_BLOCK
    pad_end = jnp.cumsum(padded)
    pad_start = pad_end - padded
    dest = pad_start[e_sorted] + (jnp.arange(A, dtype=jnp.int32) - start[e_sorted])
    n_blocks = A // MOE_BLOCK + N_EXPERTS
    P = n_blocks * MOE_BLOCK
    row_tok = jnp.zeros((P,), jnp.int32).at[dest].set(tok_sorted)
    row_w = jnp.zeros((P,), jnp.float32).at[dest].set(w_sorted)
    block_start = jnp.arange(n_blocks, dtype=jnp.int32) * MOE_BLOCK
    block_expert = jnp.minimum(jnp.sum(pad_end[None, :] <= block_start[:, None], axis=-1), N_EXPERTS - 1)
    x_rows = xt[row_tok].reshape(n_blocks, MOE_BLOCK, D)

    def expert_block(args):
        e, xb = args
        gu = xb @ w_gate_up[e] + b_gate_up[e]
        g, u = jnp.split(gu, 2, axis=-1)
        g = jnp.minimum(g, SWIGLU_LIMIT)
        u = jnp.clip(u, -SWIGLU_LIMIT, SWIGLU_LIMIT)
        act = (u + 1.0) * (g * jax.nn.sigmoid(SWIGLU_ALPHA * g))
        return act @ w_down[e] + b_down[e]

    y_rows = lax.map(expert_block, (block_expert, x_rows)).reshape(P, D)
    y = jax.ops.segment_sum(y_rows * row_w[:, None].astype(y_rows.dtype), row_tok, num_segments=N)
    return y.reshape(B, S, D)


def setup_inputs(seed: int = 0) -> dict:
    key = jax.random.key(seed)
    ks = jax.random.split(key, 32)
    f32 = jnp.float32

    def nrm(k, shape, scale):
        return jax.random.normal(k, shape, f32) * scale

    def gain(k, shape):
        return 1.0 + 0.05 * jax.random.normal(k, shape, f32)

    L = DEPTH
    return {
        "x": nrm(ks[0], (BATCH, SEQ, D_MODEL), 1.0),
        "p": nrm(ks[1], (DEPTH, BATCH, SEQ, D_PLE), 1.0),
        "positions": jnp.broadcast_to(jnp.arange(SEQ, dtype=jnp.int32), (BATCH, SEQ)),
        "attn_pre_g": gain(ks[2], (L, D_MODEL)),
        "w_in": nrm(ks[3], (L, D_MODEL, IN_COLS), D_MODEL ** -0.5),
        "q_norm_g": gain(ks[4], (L, Q_LORA)),
        "w_uq": nrm(ks[5], (L, Q_LORA, MLA_HEADS, QK_NOPE + QK_ROPE), Q_LORA ** -0.5),
        "kv_norm_g": gain(ks[6], (L, KV_LORA)),
        "w_ukv": nrm(ks[7], (L, KV_LORA, MLA_HEADS, QK_NOPE + V_HEAD), KV_LORA ** -0.5),
        "sg_norm_g": gain(ks[8], (L, SG_HEADS, SG_HEAD_DIM)),
        "w_spatial": nrm(ks[9], (L, SG_HEADS, SG_CHUNK, SG_CHUNK), SG_CHUNK ** -0.5),
        "b_spatial": 1.0 + 0.1 * jax.random.normal(ks[10], (L, SG_HEADS, SG_CHUNK), f32),
        "mla_out_g": gain(ks[11], (L, MLA_WIDTH)),
        "sg_out_g": gain(ks[12], (L, SG_WIDTH)),
        "w_out": nrm(ks[13], (L, D_MIX, D_MODEL), D_MIX ** -0.5),
        "attn_post_g": gain(ks[14], (L, D_MODEL)),
        "ffn_pre_g": gain(ks[15], (L, D_MODEL)),
        "w_router": nrm(ks[16], (L, D_MODEL, N_EXPERTS), D_MODEL ** -0.5),
        "b_router": nrm(ks[17], (L, N_EXPERTS), 0.01),
        "w_gate_up": nrm(ks[18], (L, N_EXPERTS, D_MODEL, 2 * D_EXPERT), D_MODEL ** -0.5),
        "b_gate_up": nrm(ks[19], (L, N_EXPERTS, 2 * D_EXPERT), 0.02),
        "w_down": nrm(ks[20], (L, N_EXPERTS, D_EXPERT, D_MODEL), D_EXPERT ** -0.5),
        "b_down": nrm(ks[21], (L, N_EXPERTS, D_MODEL), 0.02),
        "ffn_post_g": gain(ks[22], (L, D_MODEL)),
        "w_ple_gate": nrm(ks[23], (L, D_MODEL, D_MODEL), D_MODEL ** -0.5),
        "b_ple_gate": nrm(ks[24], (L, D_MODEL), 0.02),
        "w_ple_proj": nrm(ks[25], (L, D_PLE, D_MODEL), D_PLE ** -0.5),
        "ple_norm_g": gain(ks[26], (L, D_MODEL)),
    }


def reference(x, p, positions, attn_pre_g, w_in, q_norm_g, w_uq, kv_norm_g, w_ukv, sg_norm_g, w_spatial,
              b_spatial, mla_out_g, sg_out_g, w_out, attn_post_g, ffn_pre_g, w_router, b_router, w_gate_up,
              b_gate_up, w_down, b_down, ffn_post_g, w_ple_gate, b_ple_gate, w_ple_proj, ple_norm_g):
    cos, sin = rope_tables(positions)
    splits = [Q_LORA, Q_LORA + KV_LORA, Q_LORA + KV_LORA + QK_ROPE]
    for i in range(DEPTH):
        h = rmsnorm(x, attn_pre_g[i])
        z = h @ w_in[i]
        c_q, c_kv, k_rope, z_sg = jnp.split(z, splits, axis=-1)
        o_mla = mla_attention(c_q, c_kv, k_rope, cos, sin, q_norm_g[i], w_uq[i], kv_norm_g[i], w_ukv[i])
        o_sg = spatial_gating(z_sg, sg_norm_g[i], w_spatial[i], b_spatial[i])
        mix = jnp.concatenate([rmsnorm(o_mla, mla_out_g[i]), rmsnorm(o_sg, sg_out_g[i])], axis=-1)
        x = x + rmsnorm(mix @ w_out[i], attn_post_g[i])
        y = moe(rmsnorm(x, ffn_pre_g[i]), w_router[i], b_router[i], w_gate_up[i], b_gate_up[i], w_down[i], b_down[i])
        x = x + rmsnorm(y, ffn_post_g[i])
        g = jax.nn.sigmoid(x @ w_ple_gate[i] + b_ple_gate[i])
        x = x + rmsnorm(g * (p[i] @ w_ple_proj[i]), ple_norm_g[i])
    return x
```

```python
import functools

import jax
import jax.numpy as jnp
from jax import lax
from jax.experimental import pallas as pl
from jax.experimental.pallas import tpu as pltpu

F32 = jnp.float32
BF16 = jnp.bfloat16

D_MODEL = 1024
HEADS = 8
QK_NOPE = 64
QK_ROPE = 32
V_HEAD = 64
Q_LORA = 256
KV_LORA = 128
ROPE_THETA = 10000.0
SG_HEADS = 8
SG_HEAD_DIM = 64
SG_CHUNK = 128
SG_WIDTH = SG_HEADS * SG_HEAD_DIM
MLA_WIDTH = HEADS * V_HEAD
N_EXPERTS = 32
TOP_K = 4
SWIGLU_LIMIT = 7.0
SWIGLU_ALPHA = 1.702
D_PLE = 256
EPS = 1e-6

LANES = 128
HEAD_PAD = LANES
QK_SCALE = (QK_NOPE + QK_ROPE) ** -0.5
LOG2E = 1.4426950408889634

COL_CQ = 0
COL_CKV = COL_CQ + Q_LORA
COL_KR = COL_CKV + KV_LORA
COL_KRS = COL_KR + LANES
COL_SG = COL_KRS + LANES
IN_COLS_AUG = COL_SG + 2 * SG_WIDTH

TM_IN = 512
BQ = 512
MOE_BM = 256
VMEM_LIMIT = 56 * 1024 * 1024


def _rms(x, g):
    return x * lax.rsqrt(jnp.mean(x * x, axis=-1, keepdims=True) + EPS) * g


def _dot(a, b):
    return jnp.dot(a, b, preferred_element_type=F32)


def _inproj_kernel(x_ref, cos_ref, sin_ref, gpre_ref, win_ref, qg_ref, wq_ref, kvg_ref, wkv_ref,
                   sgg_ref, gsum_ref, wcat_ref, bsp_ref, sgo_ref,
                   q_out, k_out, v_out, sg_out):
    x = x_ref[...]
    h = _rms(x, gpre_ref[...])
    z = _dot(h.astype(BF16), win_ref[...])
    cos = cos_ref[...]
    sin = sin_ref[...]

    cqn = _rms(z[:, COL_CQ:COL_CQ + Q_LORA], qg_ref[...])
    qq = _dot(cqn.astype(BF16), wq_ref[...])
    half = HEADS * HEAD_PAD
    for hd in range(HEADS):
        sl = slice(hd * HEAD_PAD, (hd + 1) * HEAD_PAD)
        sl2 = slice(half + hd * HEAD_PAD, half + (hd + 1) * HEAD_PAD)
        q_out[:, sl] = (qq[:, sl] * cos + qq[:, sl2] * sin).astype(BF16)

    ckvn = _rms(z[:, COL_CKV:COL_CKV + KV_LORA], kvg_ref[...])
    kv = _dot(ckvn.astype(BF16), wkv_ref[...])
    kr = z[:, COL_KR:COL_KR + LANES] * cos + z[:, COL_KRS:COL_KRS + LANES] * sin
    for hd in range(HEADS):
        sl = slice(hd * HEAD_PAD, (hd + 1) * HEAD_PAD)
        k_out[:, sl] = (kv[:, sl] + kr).astype(BF16)
    v_out[...] = kv[:, half:].astype(BF16)

    zg = jax.nn.gelu(z[:, COL_SG:COL_SG + 2 * SG_WIDTH])
    u = zg[:, :SG_WIDTH]
    v = zg[:, SG_WIDTH:]
    v2 = v * v
    v2_hi = v2.astype(BF16)
    v2_lo = (v2 - v2_hi.astype(F32)).astype(BF16)
    gsum = gsum_ref[...]
    ms = (_dot(v2_hi, gsum) + _dot(v2_lo, gsum)) * (1.0 / SG_HEAD_DIM)
    vn = v * lax.rsqrt(ms + EPS) * sgg_ref[...]

    row = lax.broadcasted_iota(jnp.int32, (SG_CHUNK, SG_HEADS * SG_CHUNK), 0)
    col = lax.broadcasted_iota(jnp.int32, (SG_CHUNK, SG_HEADS * SG_CHUNK), 1)
    wcat = jnp.where((col % SG_CHUNK) <= row, wcat_ref[...], 0.0).astype(BF16)
    lane_head = lax.broadcasted_iota(jnp.int32, (SG_CHUNK, SG_WIDTH), 1) // SG_HEAD_DIM
    bsp = bsp_ref[...]
    sgo = sgo_ref[...]
    for c in range(x.shape[0] // SG_CHUNK):
        rows = slice(c * SG_CHUNK, (c + 1) * SG_CHUNK)
        vc = vn[rows]
        vbd = jnp.concatenate(
            [jnp.where(lane_head == hd, vc, 0.0).astype(BF16) for hd in range(SG_HEADS)], axis=0)
        vm = _dot(wcat, vbd) + bsp
        sg_out[rows, :] = _rms(u[rows] * vm, sgo).astype(BF16)


def _inproj(x, cos, sin, gpre, win, qg, wq, kvg, wkv, sgg, gsum, wcat, bsp, sgo):
    n = x.shape[0]
    tm = TM_IN
    tok = lambda w: pl.BlockSpec((tm, w), lambda i: (i, 0))
    full = lambda a: pl.BlockSpec(a.shape, lambda i: (0,) * a.ndim)
    consts = (gpre, win, qg, wq, kvg, wkv, sgg, gsum, wcat, bsp, sgo)
    return pl.pallas_call(
        _inproj_kernel,
        grid=(n // tm,),
        in_specs=[tok(D_MODEL), tok(LANES), tok(LANES)] + [full(a) for a in consts],
        out_specs=[tok(HEADS * HEAD_PAD), tok(HEADS * HEAD_PAD), tok(MLA_WIDTH), tok(SG_WIDTH)],
        out_shape=[jax.ShapeDtypeStruct((n, HEADS * HEAD_PAD), BF16),
                   jax.ShapeDtypeStruct((n, HEADS * HEAD_PAD), BF16),
                   jax.ShapeDtypeStruct((n, MLA_WIDTH), BF16),
                   jax.ShapeDtypeStruct((n, SG_WIDTH), BF16)],
        compiler_params=pltpu.CompilerParams(dimension_semantics=("arbitrary",),
                                             vmem_limit_bytes=VMEM_LIMIT),
        name="inproj",
    )(x, cos, sin, *consts)


def _attn_kernel(q_ref, k_ref, v_ref, g_ref, o_ref, m_sc, l_sc, acc_sc):
    i = pl.program_id(1)
    bq = q_ref.shape[0]
    c_exp = QK_SCALE * LOG2E
    row = lax.broadcasted_iota(jnp.int32, (bq, bq), 0)
    col = lax.broadcasted_iota(jnp.int32, (bq, bq), 1)
    causal = col <= row
    outs = []
    for hd in range(HEADS):
        qh = q_ref[:, hd * HEAD_PAD:(hd + 1) * HEAD_PAD]
        ksl = slice(hd * HEAD_PAD, (hd + 1) * HEAD_PAD)
        vsl = slice((hd // 2) * LANES, (hd // 2 + 1) * LANES)
        m_sc[...] = jnp.full(m_sc.shape, -jnp.inf, F32)
        l_sc[...] = jnp.zeros(l_sc.shape, F32)
        acc_sc[...] = jnp.zeros(acc_sc.shape, F32)

        def step(kb, masked):
            start = pl.multiple_of(kb * bq, bq)
            kh = k_ref[pl.ds(start, bq), ksl]
            s = lax.dot_general(qh, kh, (((1,), (1,)), ((), ())), preferred_element_type=F32)
            if masked:
                s = jnp.where(causal, s, -jnp.inf)
            m_prev = m_sc[...]
            m_new = jnp.maximum(m_prev, jnp.max(s, axis=-1, keepdims=True))
            alpha = jnp.exp2((m_prev - m_new) * c_exp)
            p = jnp.exp2((s - m_new) * c_exp)
            l_sc[...] = alpha * l_sc[...] + jnp.sum(p, axis=-1, keepdims=True)
            vh = v_ref[pl.ds(start, bq), vsl]
            acc_sc[...] = alpha * acc_sc[...] + _dot(p.astype(BF16), vh)
            m_sc[...] = m_new

        def body(kb, carry):
            step(kb, False)
            return carry

        lax.fori_loop(0, i, body, 0)
        step(i, True)
        outs.append(acc_sc[...] / l_sc[...])

    lane = lax.broadcasted_iota(jnp.int32, (bq, LANES), 1)
    pairs = [jnp.where(lane < V_HEAD, outs[2 * j], outs[2 * j + 1]) for j in range(HEADS // 2)]
    o = jnp.concatenate(pairs, axis=-1)
    o_ref[...] = _rms(o, g_ref[...]).astype(BF16)


def _attention(q, k, v, g):
    b, s, _ = q.shape
    bq = BQ
    resident = lambda w: pl.BlockSpec((None, s, w), lambda bi, i: (bi, 0, 0),
                                      pipeline_mode=pl.Buffered(1))
    return pl.pallas_call(
        _attn_kernel,
        grid=(b, s // bq),
        in_specs=[pl.BlockSpec((None, bq, HEADS * HEAD_PAD), lambda bi, i: (bi, i, 0)),
                  resident(HEADS * HEAD_PAD), resident(MLA_WIDTH),
                  pl.BlockSpec(g.shape, lambda bi, i: (0, 0))],
        out_specs=pl.BlockSpec((None, bq, MLA_WIDTH), lambda bi, i: (bi, i, 0)),
        out_shape=jax.ShapeDtypeStruct((b, s, MLA_WIDTH), BF16),
        scratch_shapes=[pltpu.VMEM((bq, 1), F32), pltpu.VMEM((bq, 1), F32),
                        pltpu.VMEM((bq, LANES), F32)],
        compiler_params=pltpu.CompilerParams(dimension_semantics=("arbitrary", "arbitrary"),
                                             vmem_limit_bytes=VMEM_LIMIT),
        name="attention",
    )(q, k, v, g)


def _postattn_kernel(x_ref, mla_ref, sg_ref, wo1_ref, wo2_ref, gpost_ref, gffn_ref, wr_ref, br_ref,
                     x1_out, h2_out, logit_out):
    a = _dot(mla_ref[...], wo1_ref[...]) + _dot(sg_ref[...], wo2_ref[...])
    x1 = x_ref[...] + _rms(a, gpost_ref[...])
    x1_out[...] = x1
    h2 = _rms(x1, gffn_ref[...]).astype(BF16)
    h2_out[...] = h2
    logit_out[...] = _dot(h2, wr_ref[...]) + br_ref[...]


def _postattn(x, mla, sg, wo1, wo2, gpost, gffn, wr, br):
    n = x.shape[0]
    tm = TM_IN
    tok = lambda w: pl.BlockSpec((tm, w), lambda i: (i, 0))
    full = lambda a: pl.BlockSpec(a.shape, lambda i: (0,) * a.ndim)
    consts = (wo1, wo2, gpost, gffn, wr, br)
    return pl.pallas_call(
        _postattn_kernel,
        grid=(n // tm,),
        in_specs=[tok(D_MODEL), tok(MLA_WIDTH), tok(SG_WIDTH)] + [full(a) for a in consts],
        out_specs=[tok(D_MODEL), tok(D_MODEL), tok(N_EXPERTS)],
        out_shape=[jax.ShapeDtypeStruct((n, D_MODEL), F32),
                   jax.ShapeDtypeStruct((n, D_MODEL), BF16),
                   jax.ShapeDtypeStruct((n, N_EXPERTS), F32)],
        compiler_params=pltpu.CompilerParams(dimension_semantics=("arbitrary",),
                                             vmem_limit_bytes=VMEM_LIMIT),
        name="postattn",
    )(x, mla, sg, *consts)


def _moe_kernel(be_ref, nb_ref, x_ref, wgu_ref, bgu_ref, wd_ref, bd_ref, y_ref):
    i = pl.program_id(0)

    @pl.when(i < nb_ref[0])
    def _():
        gu = _dot(x_ref[...], wgu_ref[0]) + bgu_ref[0]
        g = jnp.minimum(gu[:, :D_MODEL], SWIGLU_LIMIT)
        u = jnp.clip(gu[:, D_MODEL:], -SWIGLU_LIMIT, SWIGLU_LIMIT)
        act = (u + 1.0) * (g * jax.nn.sigmoid(SWIGLU_ALPHA * g))
        y_ref[...] = _dot(act.astype(BF16), wd_ref[0]) + bd_ref[0]

    @pl.when(i >= nb_ref[0])
    def _():
        y_ref[...] = jnp.zeros(y_ref.shape, y_ref.dtype)


def _moe(block_expert, n_used, x_rows, wgu, bgu, wd, bd):
    p = x_rows.shape[0]
    bm = MOE_BM
    grid_spec = pltpu.PrefetchScalarGridSpec(
        num_scalar_prefetch=2,
        grid=(p // bm,),
        in_specs=[pl.BlockSpec((bm, D_MODEL), lambda i, be, nb: (i, 0)),
                  pl.BlockSpec((1, D_MODEL, 2 * D_MODEL), lambda i, be, nb: (be[i], 0, 0)),
                  pl.BlockSpec((1, 1, 2 * D_MODEL), lambda i, be, nb: (be[i], 0, 0)),
                  pl.BlockSpec((1, D_MODEL, D_MODEL), lambda i, be, nb: (be[i], 0, 0)),
                  pl.BlockSpec((1, 1, D_MODEL), lambda i, be, nb: (be[i], 0, 0))],
        out_specs=pl.BlockSpec((bm, D_MODEL), lambda i, be, nb: (i, 0)),
    )
    return pl.pallas_call(
        _moe_kernel,
        grid_spec=grid_spec,
        out_shape=jax.ShapeDtypeStruct((p, D_MODEL), F32),
        compiler_params=pltpu.CompilerParams(dimension_semantics=("arbitrary",),
                                             vmem_limit_bytes=VMEM_LIMIT),
        name="moe",
    )(block_expert, n_used, x_rows, wgu, bgu, wd, bd)


def _final_kernel(x1_ref, y_ref, p_ref, gpost_ref, wg_ref, bg_ref, wp_ref, gple_ref, o_ref):
    x2 = x1_ref[...] + _rms(y_ref[...], gpost_ref[...])
    gate = jax.nn.sigmoid(_dot(x2.astype(BF16), wg_ref[...]) + bg_ref[...])
    pp = _dot(p_ref[...].astype(BF16), wp_ref[...])
    o_ref[...] = x2 + _rms(gate * pp, gple_ref[...])


def _final(x1, y, p, gpost, wg, bg, wp, gple):
    n = x1.shape[0]
    tm = TM_IN
    tok = lambda w: pl.BlockSpec((tm, w), lambda i: (i, 0))
    full = lambda a: pl.BlockSpec(a.shape, lambda i: (0,) * a.ndim)
    consts = (gpost, wg, bg, wp, gple)
    return pl.pallas_call(
        _final_kernel,
        grid=(n // tm,),
        in_specs=[tok(D_MODEL), tok(D_MODEL), tok(D_PLE)] + [full(a) for a in consts],
        out_specs=tok(D_MODEL),
        out_shape=jax.ShapeDtypeStruct((n, D_MODEL), F32),
        compiler_params=pltpu.CompilerParams(dimension_semantics=("arbitrary",),
                                             vmem_limit_bytes=VMEM_LIMIT),
        name="final",
    )(x1, y, p, *consts)


def _rope_tables(positions):
    inv_freq = 1.0 / (ROPE_THETA ** (jnp.arange(0, QK_ROPE, 2, dtype=F32) / QK_ROPE))
    ang = positions.astype(F32).reshape(-1)[:, None] * inv_freq
    cos, sin = jnp.cos(ang), jnp.sin(ang)
    n = ang.shape[0]
    ones = jnp.ones((n, QK_NOPE), F32)
    zeros = jnp.zeros((n, QK_NOPE), F32)
    tail1 = jnp.ones((n, HEAD_PAD - QK_NOPE - QK_ROPE), F32)
    tail0 = jnp.zeros((n, HEAD_PAD - QK_NOPE - QK_ROPE), F32)
    return (jnp.concatenate([ones, cos, cos, tail1], axis=-1),
            jnp.concatenate([zeros, sin, sin, tail0], axis=-1))


def _rot_half_cols(w):
    half = w.shape[-1] // 2
    return jnp.concatenate([-w[..., half:], w[..., :half]], axis=-1)


def _pad_head(w):
    return jnp.pad(w, [(0, 0)] * (w.ndim - 1) + [(0, HEAD_PAD - w.shape[-1])])


def _layer(x, p_l, cos, sin, prm):
    (attn_pre_g, w_in, q_norm_g, w_uq, kv_norm_g, w_ukv, sg_norm_g, w_spatial, b_spatial, mla_out_g,
     sg_out_g, w_out, attn_post_g, ffn_pre_g, w_router, b_router, w_gate_up, b_gate_up, w_down, b_down,
     ffn_post_g, w_ple_gate, b_ple_gate, w_ple_proj, ple_norm_g) = prm
    b, s, _ = x.shape
    n = b * s
    xt = x.reshape(n, D_MODEL)
    row2 = lambda a: a.reshape(1, -1)

    w_kr = w_in[:, Q_LORA + KV_LORA:Q_LORA + KV_LORA + QK_ROPE]
    place = lambda w: jnp.pad(w, ((0, 0), (QK_NOPE, HEAD_PAD - QK_NOPE - QK_ROPE)))
    win_aug = jnp.concatenate(
        [w_in[:, :Q_LORA + KV_LORA], place(w_kr), place(_rot_half_cols(w_kr)),
         w_in[:, Q_LORA + KV_LORA + QK_ROPE:]], axis=-1).astype(BF16)
    w_q_rot = jnp.concatenate([jnp.zeros_like(w_uq[..., :QK_NOPE]), _rot_half_cols(w_uq[..., QK_NOPE:])],
                              axis=-1)
    wq_aug = jnp.concatenate([_pad_head(w_uq).reshape(Q_LORA, -1), _pad_head(w_q_rot).reshape(Q_LORA, -1)],
                             axis=-1).astype(BF16)
    wkv_aug = jnp.concatenate([_pad_head(w_ukv[..., :QK_NOPE]).reshape(KV_LORA, -1),
                               w_ukv[..., QK_NOPE:].reshape(KV_LORA, -1)], axis=-1).astype(BF16)
    head_of = jnp.arange(SG_WIDTH) // SG_HEAD_DIM
    gsum = (head_of[:, None] == head_of[None, :]).astype(BF16)
    wcat = w_spatial.transpose(1, 0, 2).reshape(SG_CHUNK, SG_HEADS * SG_CHUNK)
    bsp = jnp.repeat(b_spatial.T, SG_HEAD_DIM, axis=1)

    q, k, v, mix_sg = _inproj(xt, cos, sin, row2(attn_pre_g), win_aug, row2(q_norm_g), wq_aug,
                              row2(kv_norm_g), wkv_aug, row2(sg_norm_g), gsum, wcat, bsp, row2(sg_out_g))

    mix_mla = _attention(q.reshape(b, s, -1), k.reshape(b, s, -1), v.reshape(b, s, -1), row2(mla_out_g))
    mix_mla = mix_mla.reshape(n, MLA_WIDTH)

    w_out_b = w_out.astype(BF16)
    x1, h2, logits = _postattn(xt, mix_mla, mix_sg, w_out_b[:MLA_WIDTH], w_out_b[MLA_WIDTH:],
                               row2(attn_post_g), row2(ffn_pre_g), w_router.astype(BF16), row2(b_router))

    top_val, top_idx = lax.top_k(logits, TOP_K)
    gate = jax.nn.softmax(top_val, axis=-1)
    a = n * TOP_K
    e_flat = top_idx.reshape(a)
    tok_flat = jnp.repeat(jnp.arange(n, dtype=jnp.int32), TOP_K)
    order = jnp.argsort(e_flat)
    e_sorted, tok_sorted = e_flat[order], tok_flat[order]
    counts = jnp.zeros((N_EXPERTS,), jnp.int32).at[e_flat].add(1)
    start = jnp.cumsum(counts) - counts
    padded = (counts + MOE_BM - 1) // MOE_BM * MOE_BM
    pad_end = jnp.cumsum(padded)
    pad_start = pad_end - padded
    dest_sorted = pad_start[e_sorted] + (jnp.arange(a, dtype=jnp.int32) - start[e_sorted])
    n_blocks = a // MOE_BM + N_EXPERTS
    p_rows = n_blocks * MOE_BM
    row_tok = jnp.zeros((p_rows,), jnp.int32).at[dest_sorted].set(tok_sorted)
    dest = jnp.zeros((a,), jnp.int32).at[order].set(dest_sorted)
    block_start = jnp.arange(n_blocks, dtype=jnp.int32) * MOE_BM
    block_expert = jnp.minimum(jnp.sum(pad_end[None, :] <= block_start[:, None], axis=-1),
                               N_EXPERTS - 1).astype(jnp.int32)
    n_used = (pad_end[-1] // MOE_BM).astype(jnp.int32).reshape(1)
    block_expert = jnp.where(block_start < pad_end[-1], block_expert, block_expert[jnp.maximum(n_used[0] - 1, 0)])

    x_rows = jnp.take(h2, row_tok, axis=0)
    y_rows = _moe(block_expert, n_used, x_rows, w_gate_up.astype(BF16), b_gate_up[:, None, :],
                  w_down.astype(BF16), b_down[:, None, :])
    y = jnp.sum(jnp.take(y_rows, dest, axis=0).reshape(n, TOP_K, D_MODEL) * gate[..., None], axis=1)

    out = _final(x1, y, p_l.reshape(n, D_PLE), row2(ffn_post_g), w_ple_gate.astype(BF16),
                 row2(b_ple_gate), w_ple_proj.astype(BF16), row2(ple_norm_g))
    return out.reshape(b, s, D_MODEL)


def kernel(x, p, positions, attn_pre_g, w_in, q_norm_g, w_uq, kv_norm_g, w_ukv, sg_norm_g, w_spatial, b_spatial, mla_out_g, sg_out_g, w_out, attn_post_g, ffn_pre_g, w_router, b_router, w_gate_up, b_gate_up, w_down, b_down, ffn_post_g, w_ple_gate, b_ple_gate, w_ple_proj, ple_norm_g):
    cos, sin = _rope_tables(positions)
    params = (attn_pre_g, w_in, q_norm_g, w_uq, kv_norm_g, w_ukv, sg_norm_g, w_spatial, b_spatial, mla_out_g,
              sg_out_g, w_out, attn_post_g, ffn_pre_g, w_router, b_router, w_gate_up, b_gate_up, w_down, b_down,
              ffn_post_g, w_ple_gate, b_ple_gate, w_ple_proj, ple_norm_g)
    for i in range(x_shape_depth(p)):
        x = _layer(x, p[i], cos, sin, tuple(a[i] for a in params))
    return x


def x_shape_depth(p):
    return p.shape[0]
```

```python
import jax
import jax.numpy as jnp
from jax import lax
from jax.experimental import pallas as pl
from jax.experimental.pallas import tpu as pltpu

F32 = jnp.float32
BF16 = jnp.bfloat16
I32 = jnp.int32

D_MODEL = 1024
HEADS = 8
QK_NOPE = 64
QK_ROPE = 32
V_HEAD = 64
Q_LORA = 256
KV_LORA = 128
ROPE_THETA = 10000.0
SG_HEADS = 8
SG_HEAD_DIM = 64
SG_CHUNK = 128
SG_WIDTH = SG_HEADS * SG_HEAD_DIM
MLA_WIDTH = HEADS * V_HEAD
N_EXPERTS = 32
TOP_K = 4
SWIGLU_LIMIT = 7.0
SWIGLU_ALPHA = 1.702
D_PLE = 256
EPS = 1e-6

LANES = 128
HEAD_PAD = LANES
QK_SCALE = (QK_NOPE + QK_ROPE) ** -0.5
LOG2E = 1.4426950408889634

COL_CQ = 0
COL_CKV = COL_CQ + Q_LORA
COL_KR = COL_CKV + KV_LORA
COL_KRS = COL_KR + LANES
COL_SG = COL_KRS + LANES
IN_COLS_AUG = COL_SG + 2 * SG_WIDTH

TM = 512
BQ = 512
MOE_BM = 256
VMEM_LIMIT = 56 * 1024 * 1024

NT_DIMS = (((1,), (1,)), ((), ()))


def _rms(x, g):
    return x * lax.rsqrt(jnp.mean(x * x, axis=-1, keepdims=True) + EPS) * g


def _dot(a, b):
    return jnp.dot(a, b, preferred_element_type=F32)


def _dot_nt(a, b):
    return lax.dot_general(a, b, NT_DIMS, preferred_element_type=F32)


def _inproj_kernel(x_ref, cos_ref, sin_ref, gpre_ref, win_ref, qg_ref, wq_ref, kvg_ref, wk_ref, wvt_ref,
                   sgg_ref, gsum_ref, wcat_ref, bsp_ref, sgo_ref,
                   q_out, k_out, vt_out, sg_out):
    x = x_ref[...]
    h = _rms(x, gpre_ref[...])
    z = _dot(h.astype(BF16), win_ref[...])
    cos = cos_ref[...]
    sin = sin_ref[...]

    cqn = _rms(z[:, COL_CQ:COL_CQ + Q_LORA], qg_ref[...])
    qq = _dot(cqn.astype(BF16), wq_ref[...])
    half = HEADS * HEAD_PAD
    for hd in range(HEADS):
        sl = slice(hd * HEAD_PAD, (hd + 1) * HEAD_PAD)
        sl2 = slice(half + hd * HEAD_PAD, half + (hd + 1) * HEAD_PAD)
        q_out[:, sl] = (qq[:, sl] * cos + qq[:, sl2] * sin).astype(BF16)

    ckvn = _rms(z[:, COL_CKV:COL_CKV + KV_LORA], kvg_ref[...]).astype(BF16)
    kk = _dot(ckvn, wk_ref[...])
    kr = z[:, COL_KR:COL_KR + LANES] * cos + z[:, COL_KRS:COL_KRS + LANES] * sin
    for hd in range(HEADS):
        sl = slice(hd * HEAD_PAD, (hd + 1) * HEAD_PAD)
        k_out[:, sl] = (kk[:, sl] + kr).astype(BF16)
    vt_out[...] = _dot_nt(wvt_ref[...], ckvn).astype(BF16)

    zg = jax.nn.gelu(z[:, COL_SG:COL_SG + 2 * SG_WIDTH])
    u = zg[:, :SG_WIDTH]
    v = zg[:, SG_WIDTH:]
    v2 = v * v
    v2_hi = v2.astype(BF16)
    v2_lo = (v2 - v2_hi.astype(F32)).astype(BF16)
    gsum = gsum_ref[...]
    ms = (_dot(v2_hi, gsum) + _dot(v2_lo, gsum)) * (1.0 / SG_HEAD_DIM)
    vn = v * lax.rsqrt(ms + EPS) * sgg_ref[...]

    row = lax.broadcasted_iota(I32, (SG_CHUNK, SG_HEADS * SG_CHUNK), 0)
    col = lax.broadcasted_iota(I32, (SG_CHUNK, SG_HEADS * SG_CHUNK), 1)
    wcat = jnp.where((col % SG_CHUNK) <= row, wcat_ref[...], 0.0).astype(BF16)
    lane_head = lax.broadcasted_iota(I32, (SG_CHUNK, SG_WIDTH), 1) // SG_HEAD_DIM
    bsp = bsp_ref[...]
    sgo = sgo_ref[...]
    for c in range(x.shape[0] // SG_CHUNK):
        rows = slice(c * SG_CHUNK, (c + 1) * SG_CHUNK)
        vc = vn[rows]
        vbd = jnp.concatenate(
            [jnp.where(lane_head == hd, vc, 0.0).astype(BF16) for hd in range(SG_HEADS)], axis=0)
        vm = _dot(wcat, vbd) + bsp
        sg_out[rows, :] = _rms(u[rows] * vm, sgo).astype(BF16)


def _inproj(x, cos, sin, gpre, win, qg, wq, kvg, wk, wvt, sgg, gsum, wcat, bsp, sgo):
    n = x.shape[0]
    tok = lambda w: pl.BlockSpec((TM, w), lambda i: (i, 0))
    full = lambda a: pl.BlockSpec(a.shape, lambda i: (0,) * a.ndim)
    consts = (gpre, win, qg, wq, kvg, wk, wvt, sgg, gsum, wcat, bsp, sgo)
    return pl.pallas_call(
        _inproj_kernel,
        grid=(n // TM,),
        in_specs=[tok(D_MODEL), tok(LANES), tok(LANES)] + [full(a) for a in consts],
        out_specs=[tok(HEADS * HEAD_PAD), tok(HEADS * HEAD_PAD),
                   pl.BlockSpec((MLA_WIDTH, TM), lambda i: (0, i)), tok(SG_WIDTH)],
        out_shape=[jax.ShapeDtypeStruct((n, HEADS * HEAD_PAD), BF16),
                   jax.ShapeDtypeStruct((n, HEADS * HEAD_PAD), BF16),
                   jax.ShapeDtypeStruct((MLA_WIDTH, n), BF16),
                   jax.ShapeDtypeStruct((n, SG_WIDTH), BF16)],
        compiler_params=pltpu.CompilerParams(dimension_semantics=("arbitrary",),
                                             vmem_limit_bytes=VMEM_LIMIT),
        name="inproj",
    )(x, cos, sin, *consts)


def _attn_kernel(q_ref, k_ref, vt_ref, g_ref, o_ref, m_sc, l_sc, acc_sc):
    i = pl.program_id(1)
    bq = q_ref.shape[0]
    bk = bq
    c_exp = QK_SCALE * LOG2E
    m_sc[...] = jnp.full(m_sc.shape, -jnp.inf, F32)
    l_sc[...] = jnp.zeros(l_sc.shape, F32)
    acc_sc[...] = jnp.zeros(acc_sc.shape, F32)

    def block(kb, masked):
        start = pl.multiple_of(kb * bk, bk)

        def qk(hd):
            hs = slice(hd * HEAD_PAD, (hd + 1) * HEAD_PAD)
            return _dot_nt(k_ref[pl.ds(start, bk), hs], q_ref[:, hs])

        st_next = qk(0)
        for hd in range(HEADS):
            vs = slice(hd * V_HEAD, (hd + 1) * V_HEAD)
            st = st_next
            if hd + 1 < HEADS:
                st_next = qk(hd + 1)
            if masked:
                causal = (lax.broadcasted_iota(I32, (bk, bq), 0) <= lax.broadcasted_iota(I32, (bk, bq), 1))
                st = jnp.where(causal, st, -jnp.inf)
            m_prev = m_sc[hd:hd + 1, :]
            m_new = jnp.maximum(m_prev, jnp.max(st, axis=0, keepdims=True))
            alpha = jnp.exp2((m_prev - m_new) * c_exp)
            p = jnp.exp2((st - m_new) * c_exp)
            l_sc[hd:hd + 1, :] = alpha * l_sc[hd:hd + 1, :] + jnp.sum(p, axis=0, keepdims=True)
            pv = _dot(vt_ref[vs, pl.ds(start, bk)], p.astype(BF16))
            acc_sc[vs, :] = alpha * acc_sc[vs, :] + pv
            m_sc[hd:hd + 1, :] = m_new

    def body(kb, carry):
        block(kb, False)
        return carry

    lax.fori_loop(0, i, body, 0)
    block(i, True)
    ot = jnp.concatenate(
        [acc_sc[hd * V_HEAD:(hd + 1) * V_HEAD, :] * (1.0 / l_sc[hd:hd + 1, :]) for hd in range(HEADS)], axis=0)
    o_ref[...] = _rms(ot.T, g_ref[...]).astype(BF16)


def _attention(q, k, vt, g, b, s):
    nq = s // BQ
    return pl.pallas_call(
        _attn_kernel,
        grid=(b, nq),
        in_specs=[pl.BlockSpec((BQ, HEADS * HEAD_PAD), lambda bi, i: (bi * nq + i, 0)),
                  pl.BlockSpec((s, HEADS * HEAD_PAD), lambda bi, i: (bi, 0), pipeline_mode=pl.Buffered(1)),
                  pl.BlockSpec((MLA_WIDTH, s), lambda bi, i: (0, bi), pipeline_mode=pl.Buffered(1)),
                  pl.BlockSpec(g.shape, lambda bi, i: (0, 0))],
        out_specs=pl.BlockSpec((BQ, MLA_WIDTH), lambda bi, i: (bi * nq + i, 0)),
        out_shape=jax.ShapeDtypeStruct((b * s, MLA_WIDTH), BF16),
        scratch_shapes=[pltpu.VMEM((HEADS, BQ), F32), pltpu.VMEM((HEADS, BQ), F32),
                        pltpu.VMEM((MLA_WIDTH, BQ), F32)],
        compiler_params=pltpu.CompilerParams(dimension_semantics=("arbitrary", "arbitrary"),
                                             vmem_limit_bytes=VMEM_LIMIT),
        name="attention",
    )(q, k, vt, g)


def _postattn_kernel(x_ref, mla_ref, sg_ref, wo1_ref, wo2_ref, gpost_ref, gffn_ref, wrt_ref, brt_ref,
                     x1_out, h2_out, idx_out, gate_out, rank_out, cnt_out, cnt_sc):
    i = pl.program_id(0)

    @pl.when(i == 0)
    def _():
        cnt_sc[...] = jnp.zeros(cnt_sc.shape, F32)

    a = _dot(mla_ref[...], wo1_ref[...]) + _dot(sg_ref[...], wo2_ref[...])
    x1 = x_ref[...] + _rms(a, gpost_ref[...])
    x1_out[...] = x1
    h2 = _rms(x1, gffn_ref[...]).astype(BF16)
    h2_out[...] = h2

    tm = h2.shape[0]
    logits = _dot_nt(wrt_ref[...], h2) + brt_ref[...]
    eidx = lax.broadcasted_iota(I32, (N_EXPERTS, tm), 0)
    vals, idxs, sels = [], [], []
    for _ in range(TOP_K):
        mx = jnp.max(logits, axis=0, keepdims=True)
        ik = jnp.min(jnp.where(logits == mx, eidx, N_EXPERTS), axis=0, keepdims=True)
        sel = eidx == ik
        vals.append(mx)
        idxs.append(ik)
        sels.append(sel)
        logits = jnp.where(sel, -jnp.inf, logits)
    ex = [jnp.exp(v - vals[0]) for v in vals]
    den = ex[0] + ex[1] + ex[2] + ex[3]
    gate_out[...] = jnp.concatenate([e / den for e in ex], axis=0)
    idx_out[...] = jnp.concatenate(idxs, axis=0)

    maskf = sum(jnp.where(s, 1.0, 0.0) for s in sels)
    before = (lax.broadcasted_iota(I32, (tm, tm), 0) < lax.broadcasted_iota(I32, (tm, tm), 1))
    prefix = _dot(maskf.astype(BF16), jnp.where(before, 1.0, 0.0).astype(BF16))
    base = cnt_sc[...]
    tot = base + prefix
    ranks = [jnp.sum(jnp.where(s, tot, 0.0), axis=0, keepdims=True) for s in sels]
    rank_out[...] = jnp.concatenate(ranks, axis=0).astype(I32)
    cnt = base + jnp.sum(maskf, axis=1, keepdims=True)
    cnt_sc[...] = cnt
    cnt_out[...] = jnp.broadcast_to(cnt, cnt_out.shape)


def _postattn(x, mla, sg, wo1, wo2, gpost, gffn, wrt, brt):
    n = x.shape[0]
    tok = lambda w: pl.BlockSpec((TM, w), lambda i: (i, 0))
    tokt = pl.BlockSpec((TOP_K, TM), lambda i: (0, i))
    full = lambda a: pl.BlockSpec(a.shape, lambda i: (0,) * a.ndim)
    consts = (wo1, wo2, gpost, gffn, wrt, brt)
    return pl.pallas_call(
        _postattn_kernel,
        grid=(n // TM,),
        in_specs=[tok(D_MODEL), tok(MLA_WIDTH), tok(SG_WIDTH)] + [full(a) for a in consts],
        out_specs=[tok(D_MODEL), tok(D_MODEL), tokt, tokt, tokt,
                   pl.BlockSpec((N_EXPERTS, LANES), lambda i: (0, 0))],
        out_shape=[jax.ShapeDtypeStruct((n, D_MODEL), F32),
                   jax.ShapeDtypeStruct((n, D_MODEL), BF16),
                   jax.ShapeDtypeStruct((TOP_K, n), I32),
                   jax.ShapeDtypeStruct((TOP_K, n), F32),
                   jax.ShapeDtypeStruct((TOP_K, n), I32),
                   jax.ShapeDtypeStruct((N_EXPERTS, LANES), F32)],
        scratch_shapes=[pltpu.VMEM((N_EXPERTS, 1), F32)],
        compiler_params=pltpu.CompilerParams(dimension_semantics=("arbitrary",),
                                             vmem_limit_bytes=VMEM_LIMIT),
        name="postattn",
    )(x, mla, sg, *consts)


def _moe_kernel(be_ref, nb_ref, x_ref, wgu_ref, bgu_ref, wd_ref, bd_ref, y_ref):
    i = pl.program_id(0)

    @pl.when(i < nb_ref[0])
    def _():
        gu = _dot(x_ref[...], wgu_ref[0]) + bgu_ref[0]
        g = jnp.minimum(gu[:, :D_MODEL], SWIGLU_LIMIT)
        u = jnp.clip(gu[:, D_MODEL:], -SWIGLU_LIMIT, SWIGLU_LIMIT)
        act = (u + 1.0) * (g * jax.nn.sigmoid(SWIGLU_ALPHA * g))
        y_ref[...] = (_dot(act.astype(BF16), wd_ref[0]) + bd_ref[0]).astype(y_ref.dtype)

    @pl.when(i >= nb_ref[0])
    def _():
        y_ref[...] = jnp.zeros(y_ref.shape, y_ref.dtype)


def _moe(block_expert, n_used, x_rows, wgu, bgu, wd, bd):
    p = x_rows.shape[0]
    bm = MOE_BM
    grid_spec = pltpu.PrefetchScalarGridSpec(
        num_scalar_prefetch=2,
        grid=(p // bm,),
        in_specs=[pl.BlockSpec((bm, D_MODEL), lambda i, be, nb: (i, 0)),
                  pl.BlockSpec((1, D_MODEL, 2 * D_MODEL), lambda i, be, nb: (be[i], 0, 0)),
                  pl.BlockSpec((1, 1, 2 * D_MODEL), lambda i, be, nb: (be[i], 0, 0)),
                  pl.BlockSpec((1, D_MODEL, D_MODEL), lambda i, be, nb: (be[i], 0, 0)),
                  pl.BlockSpec((1, 1, D_MODEL), lambda i, be, nb: (be[i], 0, 0))],
        out_specs=pl.BlockSpec((bm, D_MODEL), lambda i, be, nb: (i, 0)),
    )
    return pl.pallas_call(
        _moe_kernel,
        grid_spec=grid_spec,
        out_shape=jax.ShapeDtypeStruct((p, D_MODEL), BF16),
        compiler_params=pltpu.CompilerParams(dimension_semantics=("arbitrary",),
                                             vmem_limit_bytes=VMEM_LIMIT),
        name="moe",
    )(block_expert, n_used, x_rows, wgu, bgu, wd, bd)


def _final_kernel(x1_ref, yg_ref, gt_ref, p_ref, gpost_ref, wg_ref, bg_ref, wp_ref, gple_ref, o_ref):
    gt = gt_ref[...]
    y = gt[:, 0:1] * yg_ref[0].astype(F32)
    for k in range(1, TOP_K):
        y = y + gt[:, k:k + 1] * yg_ref[k].astype(F32)
    x2 = x1_ref[...] + _rms(y, gpost_ref[...])
    gate = jax.nn.sigmoid(_dot(x2.astype(BF16), wg_ref[...]) + bg_ref[...])
    pp = _dot(p_ref[...].astype(BF16), wp_ref[...])
    o_ref[...] = x2 + _rms(gate * pp, gple_ref[...])


def _final(x1, yg, gate_t, p, gpost, wg, bg, wp, gple):
    n = x1.shape[0]
    tok = lambda w: pl.BlockSpec((TM, w), lambda i: (i, 0))
    full = lambda a: pl.BlockSpec(a.shape, lambda i: (0,) * a.ndim)
    consts = (gpost, wg, bg, wp, gple)
    return pl.pallas_call(
        _final_kernel,
        grid=(n // TM,),
        in_specs=[tok(D_MODEL), pl.BlockSpec((TOP_K, TM, D_MODEL), lambda i: (0, i, 0)), tok(TOP_K),
                  tok(D_PLE)] + [full(a) for a in consts],
        out_specs=tok(D_MODEL),
        out_shape=jax.ShapeDtypeStruct((n, D_MODEL), F32),
        compiler_params=pltpu.CompilerParams(dimension_semantics=("arbitrary",),
                                             vmem_limit_bytes=VMEM_LIMIT),
        name="final",
    )(x1, yg, gate_t, p, *consts)


def _rope_tables(positions):
    inv_freq = 1.0 / (ROPE_THETA ** (jnp.arange(0, QK_ROPE, 2, dtype=F32) / QK_ROPE))
    ang = positions.astype(F32).reshape(-1)[:, None] * inv_freq
    cos, sin = jnp.cos(ang), jnp.sin(ang)
    n = ang.shape[0]
    ones = jnp.ones((n, QK_NOPE), F32)
    zeros = jnp.zeros((n, QK_NOPE), F32)
    tail1 = jnp.ones((n, HEAD_PAD - QK_NOPE - QK_ROPE), F32)
    tail0 = jnp.zeros((n, HEAD_PAD - QK_NOPE - QK_ROPE), F32)
    return (jnp.concatenate([ones, cos, cos, tail1], axis=-1),
            jnp.concatenate([zeros, sin, sin, tail0], axis=-1))


def _rot_half_cols(w):
    half = w.shape[-1] // 2
    return jnp.concatenate([-w[..., half:], w[..., :half]], axis=-1)


def _pad_head(w):
    return jnp.pad(w, [(0, 0)] * (w.ndim - 1) + [(0, HEAD_PAD - w.shape[-1])])


def _layer(x, p_l, cos, sin, prm):
    (attn_pre_g, w_in, q_norm_g, w_uq, kv_norm_g, w_ukv, sg_norm_g, w_spatial, b_spatial, mla_out_g,
     sg_out_g, w_out, attn_post_g, ffn_pre_g, w_router, b_router, w_gate_up, b_gate_up, w_down, b_down,
     ffn_post_g, w_ple_gate, b_ple_gate, w_ple_proj, ple_norm_g) = prm
    b, s, _ = x.shape
    n = b * s
    xt = x.reshape(n, D_MODEL)
    row2 = lambda a: a.reshape(1, -1)

    w_kr = w_in[:, Q_LORA + KV_LORA:Q_LORA + KV_LORA + QK_ROPE]
    place = lambda w: jnp.pad(w, ((0, 0), (QK_NOPE, HEAD_PAD - QK_NOPE - QK_ROPE)))
    win_aug = jnp.concatenate(
        [w_in[:, :Q_LORA + KV_LORA], place(w_kr), place(_rot_half_cols(w_kr)),
         w_in[:, Q_LORA + KV_LORA + QK_ROPE:]], axis=-1).astype(BF16)
    w_q_rot = jnp.concatenate([jnp.zeros_like(w_uq[..., :QK_NOPE]), _rot_half_cols(w_uq[..., QK_NOPE:])],
                              axis=-1)
    wq_aug = jnp.concatenate([_pad_head(w_uq).reshape(Q_LORA, -1), _pad_head(w_q_rot).reshape(Q_LORA, -1)],
                             axis=-1).astype(BF16)
    wk_pad = _pad_head(w_ukv[..., :QK_NOPE]).reshape(KV_LORA, -1).astype(BF16)
    wv_t = w_ukv[..., QK_NOPE:].reshape(KV_LORA, -1).T.astype(BF16)
    head_of = jnp.arange(SG_WIDTH) // SG_HEAD_DIM
    gsum = (head_of[:, None] == head_of[None, :]).astype(BF16)
    wcat = w_spatial.transpose(1, 0, 2).reshape(SG_CHUNK, SG_HEADS * SG_CHUNK)
    bsp = jnp.repeat(b_spatial.T, SG_HEAD_DIM, axis=1)

    q, k, vt, mix_sg = _inproj(xt, cos, sin, row2(attn_pre_g), win_aug, row2(q_norm_g), wq_aug,
                               row2(kv_norm_g), wk_pad, wv_t, row2(sg_norm_g), gsum, wcat, bsp,
                               row2(sg_out_g))
    mix_mla = _attention(q, k, vt, row2(mla_out_g), b, s)

    w_out_b = w_out.astype(BF16)
    x1, h2, idx, gate, rank, cnt = _postattn(
        xt, mix_mla, mix_sg, w_out_b[:MLA_WIDTH], w_out_b[MLA_WIDTH:], row2(attn_post_g), row2(ffn_pre_g),
        w_router.T.astype(BF16), b_router.reshape(N_EXPERTS, 1))

    a = n * TOP_K
    counts = cnt[:, 0].astype(I32)
    padded = (counts + MOE_BM - 1) // MOE_BM * MOE_BM
    pad_end = jnp.cumsum(padded)
    pad_start = pad_end - padded
    dest = (pad_start[idx] + rank).reshape(a)
    n_blocks = a // MOE_BM + N_EXPERTS
    row_tok = jnp.zeros((n_blocks * MOE_BM,), I32).at[dest].set(jnp.tile(jnp.arange(n, dtype=I32), TOP_K))
    block_start = jnp.arange(n_blocks, dtype=I32) * MOE_BM
    n_used = (pad_end[-1] // MOE_BM).astype(I32)
    block_expert = jnp.sum(pad_end[None, :] <= jnp.minimum(block_start, pad_end[-1] - MOE_BM)[:, None],
                           axis=-1).astype(I32)

    x_rows = jnp.take(h2, row_tok, axis=0)
    y_rows = _moe(block_expert, n_used.reshape(1), x_rows, w_gate_up.astype(BF16), b_gate_up[:, None, :],
                  w_down.astype(BF16), b_down[:, None, :])
    yg = jnp.take(y_rows, dest, axis=0).reshape(TOP_K, n, D_MODEL)

    out = _final(x1, yg, gate.T, p_l.reshape(n, D_PLE), row2(ffn_post_g), w_ple_gate.astype(BF16),
                 row2(b_ple_gate), w_ple_proj.astype(BF16), row2(ple_norm_g))
    return out.reshape(b, s, D_MODEL)


def kernel(x, p, positions, attn_pre_g, w_in, q_norm_g, w_uq, kv_norm_g, w_ukv, sg_norm_g, w_spatial, b_spatial, mla_out_g, sg_out_g, w_out, attn_post_g, ffn_pre_g, w_router, b_router, w_gate_up, b_gate_up, w_down, b_down, ffn_post_g, w_ple_gate, b_ple_gate, w_ple_proj, ple_norm_g):
    cos, sin = _rope_tables(positions)
    params = (attn_pre_g, w_in, q_norm_g, w_uq, kv_norm_g, w_ukv, sg_norm_g, w_spatial, b_spatial, mla_out_g,
              sg_out_g, w_out, attn_post_g, ffn_pre_g, w_router, b_router, w_gate_up, b_gate_up, w_down, b_down,
              ffn_post_g, w_ple_gate, b_ple_gate, w_ple_proj, ple_norm_g)
    for layer in range(p.shape[0]):
        x = _layer(x, p[layer], cos, sin, tuple(a[layer] for a in params))
    return x
```

```python
import jax
import jax.numpy as jnp
from jax import lax
from jax.experimental import pallas as pl
from jax.experimental.pallas import tpu as pltpu

F32 = jnp.float32
BF16 = jnp.bfloat16
I32 = jnp.int32

D_MODEL = 1024
HEADS = 8
QK_NOPE = 64
QK_ROPE = 32
V_HEAD = 64
Q_LORA = 256
KV_LORA = 128
ROPE_THETA = 10000.0
SG_HEADS = 8
SG_HEAD_DIM = 64
SG_CHUNK = 128
SG_WIDTH = SG_HEADS * SG_HEAD_DIM
MLA_WIDTH = HEADS * V_HEAD
N_EXPERTS = 32
TOP_K = 4
SWIGLU_LIMIT = 7.0
SWIGLU_ALPHA = 1.702
D_PLE = 256
EPS = 1e-6

LANES = 128
HEAD_PAD = LANES
QK_SCALE = (QK_NOPE + QK_ROPE) ** -0.5
LOG2E = 1.4426950408889634

COL_CQ = 0
COL_CKV = COL_CQ + Q_LORA
COL_KR = COL_CKV + KV_LORA
COL_KRS = COL_KR + LANES
COL_SG = COL_KRS + LANES
IN_COLS_AUG = COL_SG + 2 * SG_WIDTH

TM = 512
BQ = 512
MOE_BM = 256
VMEM_LIMIT = 56 * 1024 * 1024

NT_DIMS = (((1,), (1,)), ((), ()))


def _rms(x, g):
    return x * lax.rsqrt(jnp.mean(x * x, axis=-1, keepdims=True) + EPS) * g


def _dot(a, b):
    return jnp.dot(a, b, preferred_element_type=F32)


def _dot_nt(a, b):
    return lax.dot_general(a, b, NT_DIMS, preferred_element_type=F32)


def _inproj_kernel(x_ref, cos_ref, sin_ref, gpre_ref, win_ref, qg_ref, wq_ref, kvg_ref, wk_ref, wvt_ref,
                   sgg_ref, gsum_ref, wcat_ref, bsp_ref, sgo_ref,
                   q_out, k_out, vt_out, sg_out):
    x = x_ref[...]
    h = _rms(x, gpre_ref[...])
    z = _dot(h.astype(BF16), win_ref[...])
    cos = cos_ref[...]
    sin = sin_ref[...]

    cqn = _rms(z[:, COL_CQ:COL_CQ + Q_LORA], qg_ref[...])
    qq = _dot(cqn.astype(BF16), wq_ref[...])
    half = HEADS * HEAD_PAD
    for hd in range(HEADS):
        sl = slice(hd * HEAD_PAD, (hd + 1) * HEAD_PAD)
        sl2 = slice(half + hd * HEAD_PAD, half + (hd + 1) * HEAD_PAD)
        q_out[:, sl] = (qq[:, sl] * cos + qq[:, sl2] * sin).astype(BF16)

    ckvn = _rms(z[:, COL_CKV:COL_CKV + KV_LORA], kvg_ref[...]).astype(BF16)
    kk = _dot(ckvn, wk_ref[...])
    kr = z[:, COL_KR:COL_KR + LANES] * cos + z[:, COL_KRS:COL_KRS + LANES] * sin
    for hd in range(HEADS):
        sl = slice(hd * HEAD_PAD, (hd + 1) * HEAD_PAD)
        k_out[:, sl] = (kk[:, sl] + kr).astype(BF16)
    vt_out[...] = _dot_nt(wvt_ref[...], ckvn).astype(BF16)

    zg = jax.nn.gelu(z[:, COL_SG:COL_SG + 2 * SG_WIDTH])
    u = zg[:, :SG_WIDTH]
    v = zg[:, SG_WIDTH:]
    v2 = v * v
    v2_hi = v2.astype(BF16)
    v2_lo = (v2 - v2_hi.astype(F32)).astype(BF16)
    gsum = gsum_ref[...]
    ms = (_dot(v2_hi, gsum) + _dot(v2_lo, gsum)) * (1.0 / SG_HEAD_DIM)
    vn = v * lax.rsqrt(ms + EPS) * sgg_ref[...]

    row = lax.broadcasted_iota(I32, (SG_CHUNK, SG_HEADS * SG_CHUNK), 0)
    col = lax.broadcasted_iota(I32, (SG_CHUNK, SG_HEADS * SG_CHUNK), 1)
    wcat = jnp.where((col % SG_CHUNK) <= row, wcat_ref[...], 0.0).astype(BF16)
    lane_head = lax.broadcasted_iota(I32, (SG_CHUNK, SG_WIDTH), 1) // SG_HEAD_DIM
    bsp = bsp_ref[...]
    sgo = sgo_ref[...]
    for c in range(x.shape[0] // SG_CHUNK):
        rows = slice(c * SG_CHUNK, (c + 1) * SG_CHUNK)
        vc = vn[rows]
        vbd = jnp.concatenate(
            [jnp.where(lane_head == hd, vc, 0.0).astype(BF16) for hd in range(SG_HEADS)], axis=0)
        vm = _dot(wcat, vbd) + bsp
        sg_out[rows, :] = _rms(u[rows] * vm, sgo).astype(BF16)


def _inproj(x, cos, sin, gpre, win, qg, wq, kvg, wk, wvt, sgg, gsum, wcat, bsp, sgo):
    n = x.shape[0]
    tok = lambda w: pl.BlockSpec((TM, w), lambda i: (i, 0))
    full = lambda a: pl.BlockSpec(a.shape, lambda i: (0,) * a.ndim)
    consts = (gpre, win, qg, wq, kvg, wk, wvt, sgg, gsum, wcat, bsp, sgo)
    return pl.pallas_call(
        _inproj_kernel,
        grid=(n // TM,),
        in_specs=[tok(D_MODEL), tok(LANES), tok(LANES)] + [full(a) for a in consts],
        out_specs=[tok(HEADS * HEAD_PAD), tok(HEADS * HEAD_PAD),
                   pl.BlockSpec((MLA_WIDTH, TM), lambda i: (0, i)), tok(SG_WIDTH)],
        out_shape=[jax.ShapeDtypeStruct((n, HEADS * HEAD_PAD), BF16),
                   jax.ShapeDtypeStruct((n, HEADS * HEAD_PAD), BF16),
                   jax.ShapeDtypeStruct((MLA_WIDTH, n), BF16),
                   jax.ShapeDtypeStruct((n, SG_WIDTH), BF16)],
        compiler_params=pltpu.CompilerParams(dimension_semantics=("arbitrary",),
                                             vmem_limit_bytes=VMEM_LIMIT),
        name="inproj",
    )(x, cos, sin, *consts)


def _attn_kernel(q_ref, k_ref, vt_ref, g_ref, o_ref, m_sc, l_sc, acc_sc):
    i = pl.program_id(1)
    bq = q_ref.shape[0]
    bk = bq
    c_exp = QK_SCALE * LOG2E
    m_sc[...] = jnp.full(m_sc.shape, -jnp.inf, F32)
    l_sc[...] = jnp.zeros(l_sc.shape, F32)
    acc_sc[...] = jnp.zeros(acc_sc.shape, F32)

    def block(kb, masked):
        start = pl.multiple_of(kb * bk, bk)

        def qk(hd):
            hs = slice(hd * HEAD_PAD, (hd + 1) * HEAD_PAD)
            return _dot_nt(k_ref[pl.ds(start, bk), hs], q_ref[:, hs])

        st_next = qk(0)
        for hd in range(HEADS):
            vs = slice(hd * V_HEAD, (hd + 1) * V_HEAD)
            st = st_next
            if hd + 1 < HEADS:
                st_next = qk(hd + 1)
            if masked:
                causal = (lax.broadcasted_iota(I32, (bk, bq), 0) <= lax.broadcasted_iota(I32, (bk, bq), 1))
                st = jnp.where(causal, st, -jnp.inf)
            m_prev = m_sc[hd:hd + 1, :]
            m_new = jnp.maximum(m_prev, jnp.max(st, axis=0, keepdims=True))
            alpha = jnp.exp2((m_prev - m_new) * c_exp)
            p = jnp.exp2((st - m_new) * c_exp)
            l_sc[hd:hd + 1, :] = alpha * l_sc[hd:hd + 1, :] + jnp.sum(p, axis=0, keepdims=True)
            pv = _dot(vt_ref[vs, pl.ds(start, bk)], p.astype(BF16))
            acc_sc[vs, :] = alpha * acc_sc[vs, :] + pv
            m_sc[hd:hd + 1, :] = m_new

    def body(kb, carry):
        block(kb, False)
        return carry

    lax.fori_loop(0, i, body, 0)
    block(i, True)
    ot = jnp.concatenate(
        [acc_sc[hd * V_HEAD:(hd + 1) * V_HEAD, :] * (1.0 / l_sc[hd:hd + 1, :]) for hd in range(HEADS)], axis=0)
    o_ref[...] = _rms(ot.T, g_ref[...]).astype(BF16)


def _attention(q, k, vt, g, b, s):
    nq = s // BQ
    return pl.pallas_call(
        _attn_kernel,
        grid=(b, nq),
        in_specs=[pl.BlockSpec((BQ, HEADS * HEAD_PAD), lambda bi, i: (bi * nq + i, 0)),
                  pl.BlockSpec((s, HEADS * HEAD_PAD), lambda bi, i: (bi, 0), pipeline_mode=pl.Buffered(1)),
                  pl.BlockSpec((MLA_WIDTH, s), lambda bi, i: (0, bi), pipeline_mode=pl.Buffered(1)),
                  pl.BlockSpec(g.shape, lambda bi, i: (0, 0))],
        out_specs=pl.BlockSpec((BQ, MLA_WIDTH), lambda bi, i: (bi * nq + i, 0)),
        out_shape=jax.ShapeDtypeStruct((b * s, MLA_WIDTH), BF16),
        scratch_shapes=[pltpu.VMEM((HEADS, BQ), F32), pltpu.VMEM((HEADS, BQ), F32),
                        pltpu.VMEM((MLA_WIDTH, BQ), F32)],
        compiler_params=pltpu.CompilerParams(dimension_semantics=("arbitrary", "arbitrary"),
                                             vmem_limit_bytes=VMEM_LIMIT),
        name="attention",
    )(q, k, vt, g)


def _postattn_kernel(x_ref, mla_ref, sg_ref, wo1_ref, wo2_ref, gpost_ref, gffn_ref, wrt_ref, brt_ref,
                     x1_out, h2_out, idx_out, gate_out, rank_out, cnt_out, cnt_sc):
    i = pl.program_id(0)

    @pl.when(i == 0)
    def _():
        cnt_sc[...] = jnp.zeros(cnt_sc.shape, F32)

    a = _dot(mla_ref[...], wo1_ref[...]) + _dot(sg_ref[...], wo2_ref[...])
    x1 = x_ref[...] + _rms(a, gpost_ref[...])
    x1_out[...] = x1
    h2 = _rms(x1, gffn_ref[...]).astype(BF16)
    h2_out[...] = h2

    tm = h2.shape[0]
    logits = _dot_nt(wrt_ref[...], h2) + brt_ref[...]
    eidx = lax.broadcasted_iota(I32, (N_EXPERTS, tm), 0)
    vals, idxs, sels = [], [], []
    for _ in range(TOP_K):
        mx = jnp.max(logits, axis=0, keepdims=True)
        ik = jnp.min(jnp.where(logits == mx, eidx, N_EXPERTS), axis=0, keepdims=True)
        sel = eidx == ik
        vals.append(mx)
        idxs.append(ik)
        sels.append(sel)
        logits = jnp.where(sel, -jnp.inf, logits)
    ex = [jnp.exp(v - vals[0]) for v in vals]
    den = ex[0] + ex[1] + ex[2] + ex[3]
    gate_out[...] = jnp.concatenate([e / den for e in ex], axis=0)
    idx_out[...] = jnp.concatenate(idxs, axis=0)

    maskf = sum(jnp.where(s, 1.0, 0.0) for s in sels)
    before = (lax.broadcasted_iota(I32, (tm, tm), 0) < lax.broadcasted_iota(I32, (tm, tm), 1))
    prefix = _dot(maskf.astype(BF16), jnp.where(before, 1.0, 0.0).astype(BF16))
    base = cnt_sc[...]
    tot = base + prefix
    ranks = [jnp.sum(jnp.where(s, tot, 0.0), axis=0, keepdims=True) for s in sels]
    rank_out[...] = jnp.concatenate(ranks, axis=0).astype(I32)
    cnt = base + jnp.sum(maskf, axis=1, keepdims=True)
    cnt_sc[...] = cnt
    cnt_out[...] = jnp.broadcast_to(cnt, cnt_out.shape)


def _postattn(x, mla, sg, wo1, wo2, gpost, gffn, wrt, brt):
    n = x.shape[0]
    tok = lambda w: pl.BlockSpec((TM, w), lambda i: (i, 0))
    tokt = pl.BlockSpec((TOP_K, TM), lambda i: (0, i))
    full = lambda a: pl.BlockSpec(a.shape, lambda i: (0,) * a.ndim)
    consts = (wo1, wo2, gpost, gffn, wrt, brt)
    return pl.pallas_call(
        _postattn_kernel,
        grid=(n // TM,),
        in_specs=[tok(D_MODEL), tok(MLA_WIDTH), tok(SG_WIDTH)] + [full(a) for a in consts],
        out_specs=[tok(D_MODEL), tok(D_MODEL), tokt, tokt, tokt,
                   pl.BlockSpec((N_EXPERTS, LANES), lambda i: (0, 0))],
        out_shape=[jax.ShapeDtypeStruct((n, D_MODEL), F32),
                   jax.ShapeDtypeStruct((2 * n, D_MODEL), BF16),
                   jax.ShapeDtypeStruct((TOP_K, n), I32),
                   jax.ShapeDtypeStruct((TOP_K, n), F32),
                   jax.ShapeDtypeStruct((TOP_K, n), I32),
                   jax.ShapeDtypeStruct((N_EXPERTS, LANES), F32)],
        scratch_shapes=[pltpu.VMEM((N_EXPERTS, 1), F32)],
        compiler_params=pltpu.CompilerParams(dimension_semantics=("arbitrary",),
                                             vmem_limit_bytes=VMEM_LIMIT),
        name="postattn",
    )(x, mla, sg, *consts)


def _moe_kernel(be_ref, nb_ref, x_ref, wgu_ref, bgu_ref, wd_ref, bd_ref, y_ref, wgu_sc, wd_sc):
    i = pl.program_id(0)

    @pl.when((i == 0) | (be_ref[i] != be_ref[jnp.maximum(i - 1, 0)]))
    def _():
        wgu_sc[...] = wgu_ref[0].astype(BF16)
        wd_sc[...] = wd_ref[0].astype(BF16)

    @pl.when(i < nb_ref[0])
    def _():
        gu = _dot(x_ref[...], wgu_sc[...]) + bgu_ref[0]
        g = jnp.minimum(gu[:, :D_MODEL], SWIGLU_LIMIT)
        u = jnp.clip(gu[:, D_MODEL:], -SWIGLU_LIMIT, SWIGLU_LIMIT)
        act = (u + 1.0) * (g * jax.nn.sigmoid(SWIGLU_ALPHA * g))
        y_ref[...] = (_dot(act.astype(BF16), wd_sc[...]) + bd_ref[0]).astype(y_ref.dtype)

    @pl.when(i >= nb_ref[0])
    def _():
        y_ref[...] = jnp.zeros(y_ref.shape, y_ref.dtype)


def _moe(block_expert, n_used, x_rows, wgu, bgu, wd, bd):
    p = x_rows.shape[0]
    bm = MOE_BM
    grid_spec = pltpu.PrefetchScalarGridSpec(
        num_scalar_prefetch=2,
        grid=(p // bm,),
        in_specs=[pl.BlockSpec((bm, D_MODEL), lambda i, be, nb: (i, 0)),
                  pl.BlockSpec((1, D_MODEL, 2 * D_MODEL), lambda i, be, nb: (be[i], 0, 0)),
                  pl.BlockSpec((1, 1, 2 * D_MODEL), lambda i, be, nb: (be[i], 0, 0)),
                  pl.BlockSpec((1, D_MODEL, D_MODEL), lambda i, be, nb: (be[i], 0, 0)),
                  pl.BlockSpec((1, 1, D_MODEL), lambda i, be, nb: (be[i], 0, 0))],
        out_specs=pl.BlockSpec((bm, D_MODEL), lambda i, be, nb: (i, 0)),
        scratch_shapes=[pltpu.VMEM((D_MODEL, 2 * D_MODEL), BF16), pltpu.VMEM((D_MODEL, D_MODEL), BF16)],
    )
    return pl.pallas_call(
        _moe_kernel,
        grid_spec=grid_spec,
        out_shape=jax.ShapeDtypeStruct((p, D_MODEL), BF16),
        compiler_params=pltpu.CompilerParams(dimension_semantics=("arbitrary",),
                                             vmem_limit_bytes=VMEM_LIMIT),
        name="moe",
    )(block_expert, n_used, x_rows, wgu, bgu, wd, bd)


def _final_kernel(x1_ref, yg_ref, gt_ref, p_ref, gpost_ref, wg_ref, bg_ref, wp_ref, gple_ref, o_ref):
    gt = gt_ref[...]
    y = gt[:, 0:1] * yg_ref[0].astype(F32)
    for k in range(1, TOP_K):
        y = y + gt[:, k:k + 1] * yg_ref[k].astype(F32)
    x2 = x1_ref[...] + _rms(y, gpost_ref[...])
    gate = jax.nn.sigmoid(_dot(x2.astype(BF16), wg_ref[...]) + bg_ref[...])
    pp = _dot(p_ref[...].astype(BF16), wp_ref[...])
    o_ref[...] = x2 + _rms(gate * pp, gple_ref[...])


def _final(x1, yg, gate_t, p, gpost, wg, bg, wp, gple):
    n = x1.shape[0]
    tok = lambda w: pl.BlockSpec((TM, w), lambda i: (i, 0))
    full = lambda a: pl.BlockSpec(a.shape, lambda i: (0,) * a.ndim)
    consts = (gpost, wg, bg, wp, gple)
    return pl.pallas_call(
        _final_kernel,
        grid=(n // TM,),
        in_specs=[tok(D_MODEL), pl.BlockSpec((TOP_K, TM, D_MODEL), lambda i: (0, i, 0)), tok(TOP_K),
                  tok(D_PLE)] + [full(a) for a in consts],
        out_specs=tok(D_MODEL),
        out_shape=jax.ShapeDtypeStruct((n, D_MODEL), F32),
        compiler_params=pltpu.CompilerParams(dimension_semantics=("arbitrary",),
                                             vmem_limit_bytes=VMEM_LIMIT),
        name="final",
    )(x1, yg, gate_t, p, *consts)


def _rope_tables(positions):
    inv_freq = 1.0 / (ROPE_THETA ** (jnp.arange(0, QK_ROPE, 2, dtype=F32) / QK_ROPE))
    freq = jnp.concatenate([jnp.zeros((QK_NOPE,), F32), inv_freq, inv_freq,
                            jnp.zeros((HEAD_PAD - QK_NOPE - QK_ROPE,), F32)])
    ang = positions.astype(F32).reshape(-1)[:, None] * freq
    return jnp.cos(ang), jnp.sin(ang)


def _rot_half_cols(w):
    half = w.shape[-1] // 2
    return jnp.concatenate([-w[..., half:], w[..., :half]], axis=-1)


def _pad_head(w):
    return jnp.pad(w, [(0, 0)] * (w.ndim - 1) + [(0, HEAD_PAD - w.shape[-1])])


def _layer(x, p_l, cos, sin, prm):
    (attn_pre_g, w_in, q_norm_g, w_uq, kv_norm_g, w_ukv, sg_norm_g, w_spatial, b_spatial, mla_out_g,
     sg_out_g, w_out, attn_post_g, ffn_pre_g, w_router, b_router, w_gate_up, b_gate_up, w_down, b_down,
     ffn_post_g, w_ple_gate, b_ple_gate, w_ple_proj, ple_norm_g) = prm
    b, s, _ = x.shape
    n = b * s
    xt = x.reshape(n, D_MODEL)
    row2 = lambda a: a.reshape(1, -1)

    w_kr = w_in[:, Q_LORA + KV_LORA:Q_LORA + KV_LORA + QK_ROPE]
    place = lambda w: jnp.pad(w, ((0, 0), (QK_NOPE, HEAD_PAD - QK_NOPE - QK_ROPE)))
    win_aug = jnp.concatenate(
        [w_in[:, :Q_LORA + KV_LORA], place(w_kr), place(_rot_half_cols(w_kr)),
         w_in[:, Q_LORA + KV_LORA + QK_ROPE:]], axis=-1).astype(BF16)
    w_q_rot = jnp.concatenate([jnp.zeros_like(w_uq[..., :QK_NOPE]), _rot_half_cols(w_uq[..., QK_NOPE:])],
                              axis=-1)
    wq_aug = jnp.concatenate([_pad_head(w_uq).reshape(Q_LORA, -1), _pad_head(w_q_rot).reshape(Q_LORA, -1)],
                             axis=-1).astype(BF16)
    wk_pad = _pad_head(w_ukv[..., :QK_NOPE]).reshape(KV_LORA, -1).astype(BF16)
    wv_t = w_ukv[..., QK_NOPE:].reshape(KV_LORA, -1).T.astype(BF16)
    head_of = jnp.arange(SG_WIDTH) // SG_HEAD_DIM
    gsum = (head_of[:, None] == head_of[None, :]).astype(BF16)
    wcat = w_spatial.transpose(1, 0, 2).reshape(SG_CHUNK, SG_HEADS * SG_CHUNK)
    bsp = jnp.repeat(b_spatial.T, SG_HEAD_DIM, axis=1)

    q, k, vt, mix_sg = _inproj(xt, cos, sin, row2(attn_pre_g), win_aug, row2(q_norm_g), wq_aug,
                               row2(kv_norm_g), wk_pad, wv_t, row2(sg_norm_g), gsum, wcat, bsp,
                               row2(sg_out_g))
    mix_mla = _attention(q, k, vt, row2(mla_out_g), b, s)

    w_out_b = w_out.astype(BF16)
    x1, h2, idx, gate, rank, cnt = _postattn(
        xt, mix_mla, mix_sg, w_out_b[:MLA_WIDTH], w_out_b[MLA_WIDTH:], row2(attn_post_g), row2(ffn_pre_g),
        w_router.T.astype(BF16), b_router.reshape(N_EXPERTS, 1))

    a = n * TOP_K
    counts = cnt[:, 0].astype(I32)
    padded = (counts + MOE_BM - 1) // MOE_BM * MOE_BM
    pad_end = jnp.cumsum(padded)
    pad_start = pad_end - padded
    start_of = jnp.sum(jnp.where(idx[..., None] == jnp.arange(N_EXPERTS, dtype=I32), pad_start, 0), axis=-1)
    dest = (start_of + rank).reshape(a)
    n_blocks = a // MOE_BM + N_EXPERTS
    row_tok = jnp.zeros((n_blocks * MOE_BM,), I32).at[dest].set(jnp.tile(jnp.arange(n, dtype=I32), TOP_K))
    block_start = jnp.arange(n_blocks, dtype=I32) * MOE_BM
    n_used = (pad_end[-1] // MOE_BM).astype(I32)
    block_expert = jnp.sum(pad_end[None, :] <= jnp.minimum(block_start, pad_end[-1] - MOE_BM)[:, None],
                           axis=-1).astype(I32)

    x_rows = h2.at[row_tok].get(mode="promise_in_bounds")
    y_rows = _moe(block_expert, n_used.reshape(1), x_rows, w_gate_up, b_gate_up[:, None, :],
                  w_down, b_down[:, None, :])
    yg = y_rows.at[dest].get(mode="promise_in_bounds").reshape(TOP_K, n, D_MODEL)

    out = _final(x1, yg, gate.T, p_l.reshape(n, D_PLE), row2(ffn_post_g), w_ple_gate.astype(BF16),
                 row2(b_ple_gate), w_ple_proj.astype(BF16), row2(ple_norm_g))
    return out.reshape(b, s, D_MODEL)


def kernel(x, p, positions, attn_pre_g, w_in, q_norm_g, w_uq, kv_norm_g, w_ukv, sg_norm_g, w_spatial, b_spatial, mla_out_g, sg_out_g, w_out, attn_post_g, ffn_pre_g, w_router, b_router, w_gate_up, b_gate_up, w_down, b_down, ffn_post_g, w_ple_gate, b_ple_gate, w_ple_proj, ple_norm_g):
    cos, sin = _rope_tables(positions)
    params = (attn_pre_g, w_in, q_norm_g, w_uq, kv_norm_g, w_ukv, sg_norm_g, w_spatial, b_spatial, mla_out_g,
              sg_out_g, w_out, attn_post_g, ffn_pre_g, w_router, b_router, w_gate_up, b_gate_up, w_down, b_down,
              ffn_post_g, w_ple_gate, b_ple_gate, w_ple_proj, ple_norm_g)
    for layer in range(p.shape[0]):
        x = _layer(x, p[layer], cos, sin, tuple(a[layer] for a in params))
    return x
```

```python
import functools

import jax
import jax.numpy as jnp
from jax import lax
from jax.experimental import pallas as pl
from jax.experimental.pallas import tpu as pltpu
from jax.experimental.pallas import tpu_sc as plsc

F32 = jnp.float32
BF16 = jnp.bfloat16
I32 = jnp.int32

D_MODEL = 1024
HEADS = 8
QK_NOPE = 64
QK_ROPE = 32
V_HEAD = 64
Q_LORA = 256
KV_LORA = 128
ROPE_THETA = 10000.0
SG_HEADS = 8
SG_HEAD_DIM = 64
SG_CHUNK = 128
SG_WIDTH = SG_HEADS * SG_HEAD_DIM
MLA_WIDTH = HEADS * V_HEAD
N_EXPERTS = 32
TOP_K = 4
SWIGLU_LIMIT = 7.0
SWIGLU_ALPHA = 1.702
D_PLE = 256
EPS = 1e-6

LANES = 128
HEAD_PAD = LANES
QK_SCALE = (QK_NOPE + QK_ROPE) ** -0.5
LOG2E = 1.4426950408889634

COL_CQ = 0
COL_CKV = COL_CQ + Q_LORA
COL_KR = COL_CKV + KV_LORA
COL_KRS = COL_KR + LANES
COL_SG = COL_KRS + LANES
IN_COLS_AUG = COL_SG + 2 * SG_WIDTH

TM = 512
BQ = 512
MOE_BM = 256
VMEM_LIMIT = 56 * 1024 * 1024

NT_DIMS = (((1,), (1,)), ((), ()))
ROW_WORDS = D_MODEL // 2
SC_CHUNK = 64
SC_DEST_CHUNK = 8192


def _rms(x, g):
    return x * lax.rsqrt(jnp.mean(x * x, axis=-1, keepdims=True) + EPS) * g


def _dot(a, b):
    return jnp.dot(a, b, preferred_element_type=F32)


def _dot_nt(a, b):
    return lax.dot_general(a, b, NT_DIMS, preferred_element_type=F32)


def _pack_halves(x):
    half = x.shape[1] // 2
    return pltpu.pack_elementwise([x[:, :half], x[:, half:]], packed_dtype=BF16)


def _unpack_halves(w):
    return (pltpu.unpack_elementwise(w, index=0, packed_dtype=BF16, unpacked_dtype=F32),
            pltpu.unpack_elementwise(w, index=1, packed_dtype=BF16, unpacked_dtype=F32))


def _inproj_kernel(x_ref, cos_ref, sin_ref, gpre_ref, win_ref, qg_ref, wq_ref, kvg_ref, wk_ref, wvt_ref,
                   sgg_ref, gsum_ref, wcat_ref, bsp_ref, sgo_ref,
                   q_out, k_out, vt_out, sg_out):
    x = x_ref[...]
    h = _rms(x, gpre_ref[...])
    z = _dot(h.astype(BF16), win_ref[...])
    cos = cos_ref[...]
    sin = sin_ref[...]

    cqn = _rms(z[:, COL_CQ:COL_CQ + Q_LORA], qg_ref[...])
    qq = _dot(cqn.astype(BF16), wq_ref[...])
    half = HEADS * HEAD_PAD
    for hd in range(HEADS):
        sl = slice(hd * HEAD_PAD, (hd + 1) * HEAD_PAD)
        sl2 = slice(half + hd * HEAD_PAD, half + (hd + 1) * HEAD_PAD)
        q_out[:, sl] = (qq[:, sl] * cos + qq[:, sl2] * sin).astype(BF16)

    ckvn = _rms(z[:, COL_CKV:COL_CKV + KV_LORA], kvg_ref[...]).astype(BF16)
    kk = _dot(ckvn, wk_ref[...])
    kr = z[:, COL_KR:COL_KR + LANES] * cos + z[:, COL_KRS:COL_KRS + LANES] * sin
    for hd in range(HEADS):
        sl = slice(hd * HEAD_PAD, (hd + 1) * HEAD_PAD)
        k_out[:, sl] = (kk[:, sl] + kr).astype(BF16)
    vt_out[...] = _dot_nt(wvt_ref[...], ckvn).astype(BF16)

    zg = jax.nn.gelu(z[:, COL_SG:COL_SG + 2 * SG_WIDTH])
    u = zg[:, :SG_WIDTH]
    v = zg[:, SG_WIDTH:]
    v2 = v * v
    v2_hi = v2.astype(BF16)
    v2_lo = (v2 - v2_hi.astype(F32)).astype(BF16)
    gsum = gsum_ref[...]
    ms = (_dot(v2_hi, gsum) + _dot(v2_lo, gsum)) * (1.0 / SG_HEAD_DIM)
    vn = v * lax.rsqrt(ms + EPS) * sgg_ref[...]

    row = lax.broadcasted_iota(I32, (SG_CHUNK, SG_HEADS * SG_CHUNK), 0)
    col = lax.broadcasted_iota(I32, (SG_CHUNK, SG_HEADS * SG_CHUNK), 1)
    wcat = jnp.where((col % SG_CHUNK) <= row, wcat_ref[...], 0.0).astype(BF16)
    lane_head = lax.broadcasted_iota(I32, (SG_CHUNK, SG_WIDTH), 1) // SG_HEAD_DIM
    bsp = bsp_ref[...]
    sgo = sgo_ref[...]
    for c in range(x.shape[0] // SG_CHUNK):
        rows = slice(c * SG_CHUNK, (c + 1) * SG_CHUNK)
        vc = vn[rows]
        vbd = jnp.concatenate(
            [jnp.where(lane_head == hd, vc, 0.0).astype(BF16) for hd in range(SG_HEADS)], axis=0)
        vm = _dot(wcat, vbd) + bsp
        sg_out[rows, :] = _rms(u[rows] * vm, sgo).astype(BF16)


def _inproj(x, cos, sin, gpre, win, qg, wq, kvg, wk, wvt, sgg, gsum, wcat, bsp, sgo):
    n = x.shape[0]
    tok = lambda w: pl.BlockSpec((TM, w), lambda i: (i, 0))
    full = lambda a: pl.BlockSpec(a.shape, lambda i: (0,) * a.ndim)
    consts = (gpre, win, qg, wq, kvg, wk, wvt, sgg, gsum, wcat, bsp, sgo)
    return pl.pallas_call(
        _inproj_kernel,
        grid=(n // TM,),
        in_specs=[tok(D_MODEL), tok(LANES), tok(LANES)] + [full(a) for a in consts],
        out_specs=[tok(HEADS * HEAD_PAD), tok(HEADS * HEAD_PAD),
                   pl.BlockSpec((MLA_WIDTH, TM), lambda i: (0, i)), tok(SG_WIDTH)],
        out_shape=[jax.ShapeDtypeStruct((n, HEADS * HEAD_PAD), BF16),
                   jax.ShapeDtypeStruct((n, HEADS * HEAD_PAD), BF16),
                   jax.ShapeDtypeStruct((MLA_WIDTH, n), BF16),
                   jax.ShapeDtypeStruct((n, SG_WIDTH), BF16)],
        compiler_params=pltpu.CompilerParams(dimension_semantics=("arbitrary",),
                                             vmem_limit_bytes=VMEM_LIMIT),
        name="inproj",
    )(x, cos, sin, *consts)


def _attn_kernel(q_ref, k_ref, vt_ref, g_ref, o_ref, m_sc, l_sc, acc_sc):
    i = pl.program_id(1)
    bq = q_ref.shape[0]
    bk = bq
    c_exp = QK_SCALE * LOG2E
    m_sc[...] = jnp.full(m_sc.shape, -jnp.inf, F32)
    l_sc[...] = jnp.zeros(l_sc.shape, F32)
    acc_sc[...] = jnp.zeros(acc_sc.shape, F32)

    def block(kb, masked):
        start = pl.multiple_of(kb * bk, bk)

        def qk(hd):
            hs = slice(hd * HEAD_PAD, (hd + 1) * HEAD_PAD)
            return _dot_nt(k_ref[pl.ds(start, bk), hs], q_ref[:, hs])

        st_next = qk(0)
        for hd in range(HEADS):
            vs = slice(hd * V_HEAD, (hd + 1) * V_HEAD)
            st = st_next
            if hd + 1 < HEADS:
                st_next = qk(hd + 1)
            if masked:
                causal = (lax.broadcasted_iota(I32, (bk, bq), 0) <= lax.broadcasted_iota(I32, (bk, bq), 1))
                st = jnp.where(causal, st, -jnp.inf)
            m_prev = m_sc[hd:hd + 1, :]
            m_new = jnp.maximum(m_prev, jnp.max(st, axis=0, keepdims=True))
            alpha = jnp.exp2((m_prev - m_new) * c_exp)
            p = jnp.exp2((st - m_new) * c_exp)
            l_sc[hd:hd + 1, :] = alpha * l_sc[hd:hd + 1, :] + jnp.sum(p, axis=0, keepdims=True)
            pv = _dot(vt_ref[vs, pl.ds(start, bk)], p.astype(BF16))
            acc_sc[vs, :] = alpha * acc_sc[vs, :] + pv
            m_sc[hd:hd + 1, :] = m_new

    def body(kb, carry):
        block(kb, False)
        return carry

    lax.fori_loop(0, i, body, 0)
    block(i, True)
    ot = jnp.concatenate(
        [acc_sc[hd * V_HEAD:(hd + 1) * V_HEAD, :] * (1.0 / l_sc[hd:hd + 1, :]) for hd in range(HEADS)], axis=0)
    o_ref[...] = _rms(ot.T, g_ref[...]).astype(BF16)


def _attention(q, k, vt, g, b, s):
    nq = s // BQ
    return pl.pallas_call(
        _attn_kernel,
        grid=(b, nq),
        in_specs=[pl.BlockSpec((BQ, HEADS * HEAD_PAD), lambda bi, i: (bi * nq + i, 0)),
                  pl.BlockSpec((s, HEADS * HEAD_PAD), lambda bi, i: (bi, 0), pipeline_mode=pl.Buffered(1)),
                  pl.BlockSpec((MLA_WIDTH, s), lambda bi, i: (0, bi), pipeline_mode=pl.Buffered(1)),
                  pl.BlockSpec(g.shape, lambda bi, i: (0, 0))],
        out_specs=pl.BlockSpec((BQ, MLA_WIDTH), lambda bi, i: (bi * nq + i, 0)),
        out_shape=jax.ShapeDtypeStruct((b * s, MLA_WIDTH), BF16),
        scratch_shapes=[pltpu.VMEM((HEADS, BQ), F32), pltpu.VMEM((HEADS, BQ), F32),
                        pltpu.VMEM((MLA_WIDTH, BQ), F32)],
        compiler_params=pltpu.CompilerParams(dimension_semantics=("arbitrary", "arbitrary"),
                                             vmem_limit_bytes=VMEM_LIMIT),
        name="attention",
    )(q, k, vt, g)


def _postattn_kernel(x_ref, mla_ref, sg_ref, wo1_ref, wo2_ref, gpost_ref, gffn_ref, wrt_ref, brt_ref,
                     x1_out, h2_out, idx_out, gate_out, rank_out, cnt_out, cnt_sc):
    i = pl.program_id(0)

    @pl.when(i == 0)
    def _():
        cnt_sc[...] = jnp.zeros(cnt_sc.shape, F32)

    a = _dot(mla_ref[...], wo1_ref[...]) + _dot(sg_ref[...], wo2_ref[...])
    x1 = x_ref[...] + _rms(a, gpost_ref[...])
    x1_out[...] = x1
    h2f = _rms(x1, gffn_ref[...])
    h2_out[...] = _pack_halves(h2f)
    h2 = h2f.astype(BF16)

    tm = h2.shape[0]
    logits = _dot_nt(wrt_ref[...], h2) + brt_ref[...]
    eidx = lax.broadcasted_iota(I32, (N_EXPERTS, tm), 0)
    vals, idxs, sels = [], [], []
    for _ in range(TOP_K):
        mx = jnp.max(logits, axis=0, keepdims=True)
        ik = jnp.min(jnp.where(logits == mx, eidx, N_EXPERTS), axis=0, keepdims=True)
        sel = eidx == ik
        vals.append(mx)
        idxs.append(ik)
        sels.append(sel)
        logits = jnp.where(sel, -jnp.inf, logits)
    ex = [jnp.exp(v - vals[0]) for v in vals]
    den = ex[0] + ex[1] + ex[2] + ex[3]
    gate_out[...] = jnp.concatenate([e / den for e in ex], axis=0)
    idx_out[...] = jnp.concatenate(idxs, axis=0)

    maskf = sum(jnp.where(s, 1.0, 0.0) for s in sels)
    before = (lax.broadcasted_iota(I32, (tm, tm), 0) < lax.broadcasted_iota(I32, (tm, tm), 1))
    prefix = _dot(maskf.astype(BF16), jnp.where(before, 1.0, 0.0).astype(BF16))
    base = cnt_sc[...]
    tot = base + prefix
    ranks = [jnp.sum(jnp.where(s, tot, 0.0), axis=0, keepdims=True) for s in sels]
    rank_out[...] = jnp.concatenate(ranks, axis=0).astype(I32)
    cnt = base + jnp.sum(maskf, axis=1, keepdims=True)
    cnt_sc[...] = cnt
    cnt_out[...] = jnp.broadcast_to(cnt, cnt_out.shape)


def _postattn(x, mla, sg, wo1, wo2, gpost, gffn, wrt, brt):
    n = x.shape[0]
    tok = lambda w: pl.BlockSpec((TM, w), lambda i: (i, 0))
    tokt = pl.BlockSpec((TOP_K, TM), lambda i: (0, i))
    full = lambda a: pl.BlockSpec(a.shape, lambda i: (0,) * a.ndim)
    consts = (wo1, wo2, gpost, gffn, wrt, brt)
    return pl.pallas_call(
        _postattn_kernel,
        grid=(n // TM,),
        in_specs=[tok(D_MODEL), tok(MLA_WIDTH), tok(SG_WIDTH)] + [full(a) for a in consts],
        out_specs=[tok(D_MODEL), tok(ROW_WORDS), tokt, tokt, tokt,
                   pl.BlockSpec((N_EXPERTS, LANES), lambda i: (0, 0))],
        out_shape=[jax.ShapeDtypeStruct((n, D_MODEL), F32),
                   jax.ShapeDtypeStruct((n, ROW_WORDS), I32),
                   jax.ShapeDtypeStruct((TOP_K, n), I32),
                   jax.ShapeDtypeStruct((TOP_K, n), F32),
                   jax.ShapeDtypeStruct((TOP_K, n), I32),
                   jax.ShapeDtypeStruct((N_EXPERTS, LANES), F32)],
        scratch_shapes=[pltpu.VMEM((N_EXPERTS, 1), F32)],
        compiler_params=pltpu.CompilerParams(dimension_semantics=("arbitrary",),
                                             vmem_limit_bytes=VMEM_LIMIT),
        name="postattn",
    )(x, mla, sg, *consts)


def _sc_workers():
    info = plsc.get_sparse_core_info()
    return info.num_cores, info.num_cores * info.num_subcores, info.num_lanes


def _sc_stream_rows(table_hbm, idx_v, out_hbm, base, n_chunks, buf, sem_g, sem_w):
    def gather(j, b):
        rows = idx_v.at[pl.ds(pl.multiple_of(j * SC_CHUNK, SC_CHUNK), SC_CHUNK)]
        return pltpu.make_async_copy(table_hbm.at[rows], buf.at[b], sem_g.at[b])

    def write(j, b):
        rows = pl.ds(pl.multiple_of(base + j * SC_CHUNK, SC_CHUNK), SC_CHUNK)
        return pltpu.make_async_copy(buf.at[b], out_hbm.at[rows], sem_w.at[b])

    gather(0, 0).start()

    @pl.loop(0, n_chunks, step=2)
    def _(j0):
        for b in range(2):
            j = j0 + b
            gather(j, b).wait()

            @pl.when(j >= 1)
            def _():
                write(j - 1, 1 - b).wait()

            @pl.when(j + 1 < n_chunks)
            def _():
                gather(j + 1, 1 - b).start()

            write(j, b).start()

    write(n_chunks - 1, 1).wait()


def _sc_gather(table, idx):
    n_out = idx.shape[0]
    width = table.shape[1]
    num_cores, workers, _ = _sc_workers()
    per_w = n_out // workers
    n_chunks = per_w // SC_CHUNK
    assert per_w * workers == n_out and n_chunks * SC_CHUNK == per_w and n_chunks % 2 == 0

    @functools.partial(
        pl.kernel, mesh=plsc.VectorSubcoreMesh(core_axis_name="c", subcore_axis_name="s"),
        out_type=jax.ShapeDtypeStruct((n_out, width), table.dtype),
        scratch_types=[pltpu.VMEM((per_w,), I32), pltpu.VMEM((2, SC_CHUNK, width), table.dtype),
                       pltpu.SemaphoreType.DMA((2,)), pltpu.SemaphoreType.DMA((2,))])
    def gather_kernel(table_hbm, idx_hbm, out_hbm, idx_v, buf, sem_g, sem_w):
        wid = lax.axis_index("s") * num_cores + lax.axis_index("c")
        base = pl.multiple_of(wid * per_w, SC_CHUNK)
        pltpu.sync_copy(idx_hbm.at[pl.ds(base, per_w)], idx_v)
        _sc_stream_rows(table_hbm, idx_v, out_hbm, base, n_chunks, buf, sem_g, sem_w)

    return gather_kernel(table, idx)


def _sc_dispatch(table, dest, n_out):
    n_assign = dest.shape[0]
    n, width = table.shape
    num_cores, workers, lanes = _sc_workers()
    per_w = n_out // workers
    n_chunks = per_w // SC_CHUNK
    assert per_w * workers == n_out and n_chunks * SC_CHUNK == per_w and n_chunks % 2 == 0
    assert n_assign % SC_DEST_CHUNK == 0 and SC_DEST_CHUNK % lanes == 0 and per_w % lanes == 0

    @functools.partial(
        pl.kernel, mesh=plsc.VectorSubcoreMesh(core_axis_name="c", subcore_axis_name="s"),
        out_type=jax.ShapeDtypeStruct((n_out, width), table.dtype),
        scratch_types=[pltpu.VMEM((per_w,), I32), pltpu.VMEM((SC_DEST_CHUNK,), I32),
                       pltpu.VMEM((2, SC_CHUNK, width), table.dtype),
                       pltpu.SemaphoreType.DMA((2,)), pltpu.SemaphoreType.DMA((2,))],
        compiler_params=pltpu.CompilerParams(needs_layout_passes=False))
    def dispatch_kernel(table_hbm, dest_hbm, out_hbm, tok_v, dest_v, buf, sem_g, sem_w):
        wid = lax.axis_index("s") * num_cores + lax.axis_index("c")
        base = pl.multiple_of(wid * per_w, SC_CHUNK)
        lane = lax.iota(I32, lanes)

        @pl.loop(0, per_w, step=lanes)
        def _(r):
            tok_v[pl.ds(r, lanes)] = lax.rem(base + r + lane, n)

        @pl.loop(0, n_assign // SC_DEST_CHUNK)
        def _(c):
            first = pl.multiple_of(c * SC_DEST_CHUNK, SC_DEST_CHUNK)
            pltpu.sync_copy(dest_hbm.at[pl.ds(first, SC_DEST_CHUNK)], dest_v)

            @pl.loop(0, SC_DEST_CHUNK, step=lanes)
            def _(i):
                local = dest_v[pl.ds(i, lanes)] - base
                mine = (local >= 0) & (local < per_w)
                tok = lax.rem(first + i + lane, n)
                plsc.store_scatter(tok_v, [jnp.where(mine, local, 0)], tok, mask=mine)

        _sc_stream_rows(table_hbm, tok_v, out_hbm, base, n_chunks, buf, sem_g, sem_w)

    return dispatch_kernel(table, dest)


def _moe_kernel(be_ref, nb_ref, x_ref, wgu_ref, bgu_ref, wd_ref, bd_ref, y_ref, wgu_sc, wd_sc):
    i = pl.program_id(0)

    @pl.when((i == 0) | (be_ref[i] != be_ref[jnp.maximum(i - 1, 0)]))
    def _():
        wgu_sc[...] = wgu_ref[0].astype(BF16)
        wd_sc[...] = wd_ref[0].astype(BF16)

    @pl.when(i < nb_ref[0])
    def _():
        x = jnp.concatenate(_unpack_halves(x_ref[...]), axis=1).astype(BF16)
        gu = _dot(x, wgu_sc[...]) + bgu_ref[0]
        g = jnp.minimum(gu[:, :D_MODEL], SWIGLU_LIMIT)
        u = jnp.clip(gu[:, D_MODEL:], -SWIGLU_LIMIT, SWIGLU_LIMIT)
        act = (u + 1.0) * (g * jax.nn.sigmoid(SWIGLU_ALPHA * g))
        y_ref[...] = _pack_halves(_dot(act.astype(BF16), wd_sc[...]) + bd_ref[0])

    @pl.when(i >= nb_ref[0])
    def _():
        y_ref[...] = jnp.zeros(y_ref.shape, y_ref.dtype)


def _moe(block_expert, n_used, x_rows, wgu, bgu, wd, bd):
    p = x_rows.shape[0]
    bm = MOE_BM
    grid_spec = pltpu.PrefetchScalarGridSpec(
        num_scalar_prefetch=2,
        grid=(p // bm,),
        in_specs=[pl.BlockSpec((bm, ROW_WORDS), lambda i, be, nb: (i, 0)),
                  pl.BlockSpec((1, D_MODEL, 2 * D_MODEL), lambda i, be, nb: (be[i], 0, 0)),
                  pl.BlockSpec((1, 1, 2 * D_MODEL), lambda i, be, nb: (be[i], 0, 0)),
                  pl.BlockSpec((1, D_MODEL, D_MODEL), lambda i, be, nb: (be[i], 0, 0)),
                  pl.BlockSpec((1, 1, D_MODEL), lambda i, be, nb: (be[i], 0, 0))],
        out_specs=pl.BlockSpec((bm, ROW_WORDS), lambda i, be, nb: (i, 0)),
        scratch_shapes=[pltpu.VMEM((D_MODEL, 2 * D_MODEL), BF16), pltpu.VMEM((D_MODEL, D_MODEL), BF16)],
    )
    return pl.pallas_call(
        _moe_kernel,
        grid_spec=grid_spec,
        out_shape=jax.ShapeDtypeStruct((p, ROW_WORDS), I32),
        compiler_params=pltpu.CompilerParams(dimension_semantics=("arbitrary",),
                                             vmem_limit_bytes=VMEM_LIMIT),
        name="moe",
    )(block_expert, n_used, x_rows, wgu, bgu, wd, bd)


def _final_kernel(x1_ref, yg_ref, gt_ref, p_ref, gpost_ref, wg_ref, bg_ref, wp_ref, gple_ref, o_ref):
    gt = gt_ref[...]
    lo, hi = _unpack_halves(yg_ref[0])
    y = gt[:, 0:1] * jnp.concatenate([lo, hi], axis=1)
    for k in range(1, TOP_K):
        lo, hi = _unpack_halves(yg_ref[k])
        y = y + gt[:, k:k + 1] * jnp.concatenate([lo, hi], axis=1)
    x2 = x1_ref[...] + _rms(y, gpost_ref[...])
    gate = jax.nn.sigmoid(_dot(x2.astype(BF16), wg_ref[...]) + bg_ref[...])
    pp = _dot(p_ref[...].astype(BF16), wp_ref[...])
    o_ref[...] = x2 + _rms(gate * pp, gple_ref[...])


def _final(x1, yg, gate_t, p, gpost, wg, bg, wp, gple):
    n = x1.shape[0]
    tok = lambda w: pl.BlockSpec((TM, w), lambda i: (i, 0))
    full = lambda a: pl.BlockSpec(a.shape, lambda i: (0,) * a.ndim)
    consts = (gpost, wg, bg, wp, gple)
    return pl.pallas_call(
        _final_kernel,
        grid=(n // TM,),
        in_specs=[tok(D_MODEL), pl.BlockSpec((TOP_K, TM, ROW_WORDS), lambda i: (0, i, 0)), tok(TOP_K),
                  tok(D_PLE)] + [full(a) for a in consts],
        out_specs=tok(D_MODEL),
        out_shape=jax.ShapeDtypeStruct((n, D_MODEL), F32),
        compiler_params=pltpu.CompilerParams(dimension_semantics=("arbitrary",),
                                             vmem_limit_bytes=VMEM_LIMIT),
        name="final",
    )(x1, yg, gate_t, p, *consts)


def _rope_tables(positions):
    inv_freq = 1.0 / (ROPE_THETA ** (jnp.arange(0, QK_ROPE, 2, dtype=F32) / QK_ROPE))
    freq = jnp.concatenate([jnp.zeros((QK_NOPE,), F32), inv_freq, inv_freq,
                            jnp.zeros((HEAD_PAD - QK_NOPE - QK_ROPE,), F32)])
    ang = positions.astype(F32).reshape(-1)[:, None] * freq
    return jnp.cos(ang), jnp.sin(ang)


def _rot_half_cols(w):
    half = w.shape[-1] // 2
    return jnp.concatenate([-w[..., half:], w[..., :half]], axis=-1)


def _pad_head(w):
    return jnp.pad(w, [(0, 0)] * (w.ndim - 1) + [(0, HEAD_PAD - w.shape[-1])])


def _layer(x, p_l, cos, sin, prm):
    (attn_pre_g, w_in, q_norm_g, w_uq, kv_norm_g, w_ukv, sg_norm_g, w_spatial, b_spatial, mla_out_g,
     sg_out_g, w_out, attn_post_g, ffn_pre_g, w_router, b_router, w_gate_up, b_gate_up, w_down, b_down,
     ffn_post_g, w_ple_gate, b_ple_gate, w_ple_proj, ple_norm_g) = prm
    b, s, _ = x.shape
    n = b * s
    xt = x.reshape(n, D_MODEL)
    row2 = lambda a: a.reshape(1, -1)

    w_kr = w_in[:, Q_LORA + KV_LORA:Q_LORA + KV_LORA + QK_ROPE]
    place = lambda w: jnp.pad(w, ((0, 0), (QK_NOPE, HEAD_PAD - QK_NOPE - QK_ROPE)))
    win_aug = jnp.concatenate(
        [w_in[:, :Q_LORA + KV_LORA], place(w_kr), place(_rot_half_cols(w_kr)),
         w_in[:, Q_LORA + KV_LORA + QK_ROPE:]], axis=-1).astype(BF16)
    w_q_rot = jnp.concatenate([jnp.zeros_like(w_uq[..., :QK_NOPE]), _rot_half_cols(w_uq[..., QK_NOPE:])],
                              axis=-1)
    wq_aug = jnp.concatenate([_pad_head(w_uq).reshape(Q_LORA, -1), _pad_head(w_q_rot).reshape(Q_LORA, -1)],
                             axis=-1).astype(BF16)
    wk_pad = _pad_head(w_ukv[..., :QK_NOPE]).reshape(KV_LORA, -1).astype(BF16)
    wv_t = w_ukv[..., QK_NOPE:].reshape(KV_LORA, -1).T.astype(BF16)
    head_of = jnp.arange(SG_WIDTH) // SG_HEAD_DIM
    gsum = (head_of[:, None] == head_of[None, :]).astype(BF16)
    wcat = w_spatial.transpose(1, 0, 2).reshape(SG_CHUNK, SG_HEADS * SG_CHUNK)
    bsp = jnp.repeat(b_spatial.T, SG_HEAD_DIM, axis=1)

    q, k, vt, mix_sg = _inproj(xt, cos, sin, row2(attn_pre_g), win_aug, row2(q_norm_g), wq_aug,
                               row2(kv_norm_g), wk_pad, wv_t, row2(sg_norm_g), gsum, wcat, bsp,
                               row2(sg_out_g))
    mix_mla = _attention(q, k, vt, row2(mla_out_g), b, s)

    w_out_b = w_out.astype(BF16)
    x1, h2, idx, gate, rank, cnt = _postattn(
        xt, mix_mla, mix_sg, w_out_b[:MLA_WIDTH], w_out_b[MLA_WIDTH:], row2(attn_post_g), row2(ffn_pre_g),
        w_router.T.astype(BF16), b_router.reshape(N_EXPERTS, 1))

    a = n * TOP_K
    counts = cnt[:, 0].astype(I32)
    padded = (counts + MOE_BM - 1) // MOE_BM * MOE_BM
    pad_end = jnp.cumsum(padded)
    pad_start = pad_end - padded
    start_of = jnp.sum(jnp.where(idx[..., None] == jnp.arange(N_EXPERTS, dtype=I32), pad_start, 0), axis=-1)
    dest = (start_of + rank).reshape(a)
    n_blocks = a // MOE_BM + N_EXPERTS
    block_start = jnp.arange(n_blocks, dtype=I32) * MOE_BM
    n_used = (pad_end[-1] // MOE_BM).astype(I32)
    block_expert = jnp.sum(pad_end[None, :] <= jnp.minimum(block_start, pad_end[-1] - MOE_BM)[:, None],
                           axis=-1).astype(I32)

    x_rows = _sc_dispatch(h2, dest, n_blocks * MOE_BM)
    y_rows = _moe(block_expert, n_used.reshape(1), x_rows, w_gate_up, b_gate_up[:, None, :],
                  w_down, b_down[:, None, :])
    yg = _sc_gather(y_rows, dest).reshape(TOP_K, n, ROW_WORDS)

    out = _final(x1, yg, gate.T, p_l.reshape(n, D_PLE), row2(ffn_post_g), w_ple_gate.astype(BF16),
                 row2(b_ple_gate), w_ple_proj.astype(BF16), row2(ple_norm_g))
    return out.reshape(b, s, D_MODEL)


def kernel(x, p, positions, attn_pre_g, w_in, q_norm_g, w_uq, kv_norm_g, w_ukv, sg_norm_g, w_spatial, b_spatial, mla_out_g, sg_out_g, w_out, attn_post_g, ffn_pre_g, w_router, b_router, w_gate_up, b_gate_up, w_down, b_down, ffn_post_g, w_ple_gate, b_ple_gate, w_ple_proj, ple_norm_g):
    cos, sin = _rope_tables(positions)
    params = (attn_pre_g, w_in, q_norm_g, w_uq, kv_norm_g, w_ukv, sg_norm_g, w_spatial, b_spatial, mla_out_g,
              sg_out_g, w_out, attn_post_g, ffn_pre_g, w_router, b_router, w_gate_up, b_gate_up, w_down, b_down,
              ffn_post_g, w_ple_gate, b_ple_gate, w_ple_proj, ple_norm_g)
    for layer in range(p.shape[0]):
        x = _layer(x, p[layer], cos, sin, tuple(a[layer] for a in params))
    return x
```

```python
import functools

import jax
import jax.numpy as jnp
from jax import lax
from jax.experimental import pallas as pl
from jax.experimental.pallas import tpu as pltpu
from jax.experimental.pallas import tpu_sc as plsc

F32 = jnp.float32
BF16 = jnp.bfloat16
I32 = jnp.int32

D_MODEL = 1024
HEADS = 8
QK_NOPE = 64
QK_ROPE = 32
V_HEAD = 64
Q_LORA = 256
KV_LORA = 128
ROPE_THETA = 10000.0
SG_HEADS = 8
SG_HEAD_DIM = 64
SG_CHUNK = 128
SG_WIDTH = SG_HEADS * SG_HEAD_DIM
MLA_WIDTH = HEADS * V_HEAD
N_EXPERTS = 32
TOP_K = 4
SWIGLU_LIMIT = 7.0
SWIGLU_ALPHA = 1.702
D_PLE = 256
EPS = 1e-6

LANES = 128
HEAD_PAD = LANES
QK_SCALE = (QK_NOPE + QK_ROPE) ** -0.5
LOG2E = 1.4426950408889634

COL_CQ = 0
COL_CKV = COL_CQ + Q_LORA
COL_KR = COL_CKV + KV_LORA
COL_KRS = COL_KR + LANES
COL_SG = COL_KRS + LANES
IN_COLS_AUG = COL_SG + 2 * SG_WIDTH

TM = 512
BQ = 512
ATTN_ONES_ROWS = 16
MOE_BM = 256
VMEM_LIMIT = 56 * 1024 * 1024

NT_DIMS = (((1,), (1,)), ((), ()))
ROW_WORDS = D_MODEL // 2
SC_CHUNK = 64
SC_DEST_CHUNK = 8192


def _rms(x, g):
    return x * lax.rsqrt(jnp.mean(x * x, axis=-1, keepdims=True) + EPS) * g


def _dot(a, b):
    return jnp.dot(a, b, preferred_element_type=F32)


def _dot_nt(a, b):
    return lax.dot_general(a, b, NT_DIMS, preferred_element_type=F32)


def _pack_halves(x):
    half = x.shape[1] // 2
    return pltpu.pack_elementwise([x[:, :half], x[:, half:]], packed_dtype=BF16)


def _unpack_halves(w):
    return (pltpu.unpack_elementwise(w, index=0, packed_dtype=BF16, unpacked_dtype=F32),
            pltpu.unpack_elementwise(w, index=1, packed_dtype=BF16, unpacked_dtype=F32))


def _inproj_kernel(x_ref, cos_ref, sin_ref, gpre_ref, win_ref, qg_ref, wq_ref, kvg_ref, wk_ref, wvt_ref,
                   sgg_ref, gsum_ref, wcat_ref, bsp_ref, sgo_ref,
                   q_out, k_out, vt_out, sg_out):
    x = x_ref[...]
    h = _rms(x, gpre_ref[...])
    z = _dot(h.astype(BF16), win_ref[...])
    cos = cos_ref[...]
    sin = sin_ref[...]

    cqn = _rms(z[:, COL_CQ:COL_CQ + Q_LORA], qg_ref[...])
    qq = _dot(cqn.astype(BF16), wq_ref[...])
    half = HEADS * HEAD_PAD
    for hd in range(HEADS):
        sl = slice(hd * HEAD_PAD, (hd + 1) * HEAD_PAD)
        sl2 = slice(half + hd * HEAD_PAD, half + (hd + 1) * HEAD_PAD)
        q_out[:, sl] = ((qq[:, sl] * cos + qq[:, sl2] * sin) * (QK_SCALE * LOG2E)).astype(BF16)

    ckvn = _rms(z[:, COL_CKV:COL_CKV + KV_LORA], kvg_ref[...]).astype(BF16)
    kk = _dot(ckvn, wk_ref[...])
    kr = z[:, COL_KR:COL_KR + LANES] * cos + z[:, COL_KRS:COL_KRS + LANES] * sin
    for hd in range(HEADS):
        sl = slice(hd * HEAD_PAD, (hd + 1) * HEAD_PAD)
        k_out[:, sl] = (kk[:, sl] + kr).astype(BF16)
    vt_out[...] = _dot_nt(wvt_ref[...], ckvn).astype(BF16)

    zg = jax.nn.gelu(z[:, COL_SG:COL_SG + 2 * SG_WIDTH])
    u = zg[:, :SG_WIDTH]
    v = zg[:, SG_WIDTH:]
    v2 = v * v
    v2_hi = v2.astype(BF16)
    v2_lo = (v2 - v2_hi.astype(F32)).astype(BF16)
    gsum = gsum_ref[...]
    ms = (_dot(v2_hi, gsum) + _dot(v2_lo, gsum)) * (1.0 / SG_HEAD_DIM)
    vn = v * lax.rsqrt(ms + EPS) * sgg_ref[...]

    row = lax.broadcasted_iota(I32, (SG_CHUNK, SG_HEADS * SG_CHUNK), 0)
    col = lax.broadcasted_iota(I32, (SG_CHUNK, SG_HEADS * SG_CHUNK), 1)
    wcat = jnp.where((col % SG_CHUNK) <= row, wcat_ref[...], 0.0).astype(BF16)
    lane_head = lax.broadcasted_iota(I32, (SG_CHUNK, SG_WIDTH), 1) // SG_HEAD_DIM
    bsp = bsp_ref[...]
    sgo = sgo_ref[...]
    for c in range(x.shape[0] // SG_CHUNK):
        rows = slice(c * SG_CHUNK, (c + 1) * SG_CHUNK)
        vc = vn[rows]
        vbd = jnp.concatenate(
            [jnp.where(lane_head == hd, vc, 0.0).astype(BF16) for hd in range(SG_HEADS)], axis=0)
        vm = _dot(wcat, vbd) + bsp
        sg_out[rows, :] = _rms(u[rows] * vm, sgo).astype(BF16)


def _inproj(x, cos, sin, gpre, win, qg, wq, kvg, wk, wvt, sgg, gsum, wcat, bsp, sgo):
    n = x.shape[0]
    tok = lambda w: pl.BlockSpec((TM, w), lambda i: (i, 0))
    full = lambda a: pl.BlockSpec(a.shape, lambda i: (0,) * a.ndim)
    consts = (gpre, win, qg, wq, kvg, wk, wvt, sgg, gsum, wcat, bsp, sgo)
    return pl.pallas_call(
        _inproj_kernel,
        grid=(n // TM,),
        in_specs=[tok(D_MODEL), tok(LANES), tok(LANES)] + [full(a) for a in consts],
        out_specs=[tok(HEADS * HEAD_PAD), tok(HEADS * HEAD_PAD),
                   pl.BlockSpec((MLA_WIDTH, TM), lambda i: (0, i)), tok(SG_WIDTH)],
        out_shape=[jax.ShapeDtypeStruct((n, HEADS * HEAD_PAD), BF16),
                   jax.ShapeDtypeStruct((n, HEADS * HEAD_PAD), BF16),
                   jax.ShapeDtypeStruct((MLA_WIDTH, n), BF16),
                   jax.ShapeDtypeStruct((n, SG_WIDTH), BF16)],
        compiler_params=pltpu.CompilerParams(dimension_semantics=("arbitrary",),
                                             vmem_limit_bytes=VMEM_LIMIT),
        name="inproj",
    )(x, cos, sin, *consts)


def _attn_kernel(q_ref, k_ref, vt_ref, g_ref, o_ref, m_sc, l_sc, acc_sc, st0_sc):
    i = pl.program_id(1)
    bq = q_ref.shape[0]
    bk = bq
    m_sc[...] = jnp.full(m_sc.shape, -jnp.inf, F32)
    l_sc[...] = jnp.zeros(l_sc.shape, F32)
    acc_sc[...] = jnp.zeros(acc_sc.shape, F32)
    ones = jnp.ones((ATTN_ONES_ROWS, bk), BF16)

    def qk(kb, hd):
        start = pl.multiple_of(kb * bk, bk)
        hs = slice(hd * HEAD_PAD, (hd + 1) * HEAD_PAD)
        return _dot_nt(k_ref[pl.ds(start, bk), hs], q_ref[:, hs])

    def block(kb, masked):
        start = pl.multiple_of(kb * bk, bk)
        queue = [st0_sc[...], qk(kb, 1)]
        for hd in range(HEADS):
            vs = slice(hd * V_HEAD, (hd + 1) * V_HEAD)
            st = queue.pop(0)
            if hd + 2 < HEADS:
                queue.append(qk(kb, hd + 2))
            elif hd + 2 == HEADS and not masked:
                st0_sc[...] = qk(kb + 1, 0)
            if masked:
                causal = (lax.broadcasted_iota(I32, (bk, bq), 0) <= lax.broadcasted_iota(I32, (bk, bq), 1))
                st = jnp.where(causal, st, -jnp.inf)
            m_prev = m_sc[hd:hd + 1, :]
            m_new = jnp.maximum(m_prev, jnp.max(st, axis=0, keepdims=True))
            alpha = jnp.exp2(m_prev - m_new)
            p = jnp.exp2(st - m_new).astype(BF16)
            vta = jnp.concatenate([vt_ref[vs, pl.ds(start, bk)], ones], axis=0)
            pv = _dot(vta, p)
            l_sc[hd:hd + 1, :] = alpha * l_sc[hd:hd + 1, :] + pv[V_HEAD:V_HEAD + 1, :]
            acc_sc[vs, :] = alpha * acc_sc[vs, :] + pv[:V_HEAD, :]
            m_sc[hd:hd + 1, :] = m_new

    def body(kb, carry):
        block(kb, False)
        return carry

    st0_sc[...] = qk(0, 0)
    lax.fori_loop(0, i, body, 0)
    block(i, True)
    ot = jnp.concatenate(
        [acc_sc[hd * V_HEAD:(hd + 1) * V_HEAD, :] * (1.0 / l_sc[hd:hd + 1, :]) for hd in range(HEADS)], axis=0)
    o_ref[...] = _rms(ot.T, g_ref[...]).astype(BF16)


def _attention(q, k, vt, g, b, s):
    nq = s // BQ
    return pl.pallas_call(
        _attn_kernel,
        grid=(b, nq),
        in_specs=[pl.BlockSpec((BQ, HEADS * HEAD_PAD), lambda bi, i: (bi * nq + i, 0)),
                  pl.BlockSpec((s, HEADS * HEAD_PAD), lambda bi, i: (bi, 0), pipeline_mode=pl.Buffered(1)),
                  pl.BlockSpec((MLA_WIDTH, s), lambda bi, i: (0, bi), pipeline_mode=pl.Buffered(1)),
                  pl.BlockSpec(g.shape, lambda bi, i: (0, 0))],
        out_specs=pl.BlockSpec((BQ, MLA_WIDTH), lambda bi, i: (bi * nq + i, 0)),
        out_shape=jax.ShapeDtypeStruct((b * s, MLA_WIDTH), BF16),
        scratch_shapes=[pltpu.VMEM((HEADS, BQ), F32), pltpu.VMEM((HEADS, BQ), F32),
                        pltpu.VMEM((MLA_WIDTH, BQ), F32), pltpu.VMEM((BQ, BQ), F32)],
        compiler_params=pltpu.CompilerParams(dimension_semantics=("arbitrary", "arbitrary"),
                                             vmem_limit_bytes=VMEM_LIMIT),
        name="attention",
    )(q, k, vt, g)


def _postattn_kernel(x_ref, mla_ref, sg_ref, wo1_ref, wo2_ref, gpost_ref, gffn_ref, wrt_ref, brt_ref,
                     x1_out, h2_out, idx_out, gate_out, rank_out, cnt_out, cnt_sc):
    i = pl.program_id(0)

    @pl.when(i == 0)
    def _():
        cnt_sc[...] = jnp.zeros(cnt_sc.shape, F32)

    a = _dot(mla_ref[...], wo1_ref[...]) + _dot(sg_ref[...], wo2_ref[...])
    x1 = x_ref[...] + _rms(a, gpost_ref[...])
    x1_out[...] = x1
    h2f = _rms(x1, gffn_ref[...])
    h2_out[...] = _pack_halves(h2f)
    h2 = h2f.astype(BF16)

    tm = h2.shape[0]
    logits = _dot_nt(wrt_ref[...], h2) + brt_ref[...]
    eidx = lax.broadcasted_iota(I32, (N_EXPERTS, tm), 0)
    vals, idxs, sels = [], [], []
    for _ in range(TOP_K):
        mx = jnp.max(logits, axis=0, keepdims=True)
        ik = jnp.min(jnp.where(logits == mx, eidx, N_EXPERTS), axis=0, keepdims=True)
        sel = eidx == ik
        vals.append(mx)
        idxs.append(ik)
        sels.append(sel)
        logits = jnp.where(sel, -jnp.inf, logits)
    ex = [jnp.exp(v - vals[0]) for v in vals]
    den = ex[0] + ex[1] + ex[2] + ex[3]
    gate_out[...] = jnp.concatenate([e / den for e in ex], axis=0)
    idx_out[...] = jnp.concatenate(idxs, axis=0)

    maskf = sum(jnp.where(s, 1.0, 0.0) for s in sels)
    before = (lax.broadcasted_iota(I32, (tm, tm), 0) < lax.broadcasted_iota(I32, (tm, tm), 1))
    prefix = _dot(maskf.astype(BF16), jnp.where(before, 1.0, 0.0).astype(BF16))
    base = cnt_sc[...]
    tot = base + prefix
    ranks = [jnp.sum(jnp.where(s, tot, 0.0), axis=0, keepdims=True) for s in sels]
    rank_out[...] = jnp.concatenate(ranks, axis=0).astype(I32)
    cnt = base + jnp.sum(maskf, axis=1, keepdims=True)
    cnt_sc[...] = cnt
    cnt_out[...] = jnp.broadcast_to(cnt, cnt_out.shape)


def _postattn(x, mla, sg, wo1, wo2, gpost, gffn, wrt, brt):
    n = x.shape[0]
    tok = lambda w: pl.BlockSpec((TM, w), lambda i: (i, 0))
    tokt = pl.BlockSpec((TOP_K, TM), lambda i: (0, i))
    full = lambda a: pl.BlockSpec(a.shape, lambda i: (0,) * a.ndim)
    consts = (wo1, wo2, gpost, gffn, wrt, brt)
    return pl.pallas_call(
        _postattn_kernel,
        grid=(n // TM,),
        in_specs=[tok(D_MODEL), tok(MLA_WIDTH), tok(SG_WIDTH)] + [full(a) for a in consts],
        out_specs=[tok(D_MODEL), tok(ROW_WORDS), tokt, tokt, tokt,
                   pl.BlockSpec((N_EXPERTS, LANES), lambda i: (0, 0))],
        out_shape=[jax.ShapeDtypeStruct((n, D_MODEL), F32),
                   jax.ShapeDtypeStruct((n, ROW_WORDS), I32),
                   jax.ShapeDtypeStruct((TOP_K, n), I32),
                   jax.ShapeDtypeStruct((TOP_K, n), F32),
                   jax.ShapeDtypeStruct((TOP_K, n), I32),
                   jax.ShapeDtypeStruct((N_EXPERTS, LANES), F32)],
        scratch_shapes=[pltpu.VMEM((N_EXPERTS, 1), F32)],
        compiler_params=pltpu.CompilerParams(dimension_semantics=("arbitrary",),
                                             vmem_limit_bytes=VMEM_LIMIT),
        name="postattn",
    )(x, mla, sg, *consts)


def _sc_workers():
    info = plsc.get_sparse_core_info()
    return info.num_cores, info.num_cores * info.num_subcores, info.num_lanes


def _sc_stream_rows(table_hbm, idx_v, out_hbm, base, n_chunks, buf, sem_g, sem_w):
    def gather(j, b):
        rows = idx_v.at[pl.ds(pl.multiple_of(j * SC_CHUNK, SC_CHUNK), SC_CHUNK)]
        return pltpu.make_async_copy(table_hbm.at[rows], buf.at[b], sem_g.at[b])

    def write(j, b):
        rows = pl.ds(pl.multiple_of(base + j * SC_CHUNK, SC_CHUNK), SC_CHUNK)
        return pltpu.make_async_copy(buf.at[b], out_hbm.at[rows], sem_w.at[b])

    gather(0, 0).start()

    @pl.loop(0, n_chunks, step=2)
    def _(j0):
        for b in range(2):
            j = j0 + b
            gather(j, b).wait()

            @pl.when(j >= 1)
            def _():
                write(j - 1, 1 - b).wait()

            @pl.when(j + 1 < n_chunks)
            def _():
                gather(j + 1, 1 - b).start()

            write(j, b).start()

    write(n_chunks - 1, 1).wait()


def _sc_gather(table, idx):
    n_out = idx.shape[0]
    width = table.shape[1]
    num_cores, workers, _ = _sc_workers()
    per_w = n_out // workers
    n_chunks = per_w // SC_CHUNK
    assert per_w * workers == n_out and n_chunks * SC_CHUNK == per_w and n_chunks % 2 == 0

    @functools.partial(
        pl.kernel, mesh=plsc.VectorSubcoreMesh(core_axis_name="c", subcore_axis_name="s"),
        out_type=jax.ShapeDtypeStruct((n_out, width), table.dtype),
        scratch_types=[pltpu.VMEM((per_w,), I32), pltpu.VMEM((2, SC_CHUNK, width), table.dtype),
                       pltpu.SemaphoreType.DMA((2,)), pltpu.SemaphoreType.DMA((2,))])
    def gather_kernel(table_hbm, idx_hbm, out_hbm, idx_v, buf, sem_g, sem_w):
        wid = lax.axis_index("s") * num_cores + lax.axis_index("c")
        base = pl.multiple_of(wid * per_w, SC_CHUNK)
        pltpu.sync_copy(idx_hbm.at[pl.ds(base, per_w)], idx_v)
        _sc_stream_rows(table_hbm, idx_v, out_hbm, base, n_chunks, buf, sem_g, sem_w)

    return gather_kernel(table, idx)


def _sc_dispatch(table, dest, n_out):
    n_assign = dest.shape[0]
    n, width = table.shape
    num_cores, workers, lanes = _sc_workers()
    per_w = n_out // workers
    n_chunks = per_w // SC_CHUNK
    assert per_w * workers == n_out and n_chunks * SC_CHUNK == per_w and n_chunks % 2 == 0
    assert n_assign % SC_DEST_CHUNK == 0 and SC_DEST_CHUNK % lanes == 0 and per_w % lanes == 0

    @functools.partial(
        pl.kernel, mesh=plsc.VectorSubcoreMesh(core_axis_name="c", subcore_axis_name="s"),
        out_type=jax.ShapeDtypeStruct((n_out, width), table.dtype),
        scratch_types=[pltpu.VMEM((per_w,), I32), pltpu.VMEM((SC_DEST_CHUNK,), I32),
                       pltpu.VMEM((2, SC_CHUNK, width), table.dtype),
                       pltpu.SemaphoreType.DMA((2,)), pltpu.SemaphoreType.DMA((2,))],
        compiler_params=pltpu.CompilerParams(needs_layout_passes=False))
    def dispatch_kernel(table_hbm, dest_hbm, out_hbm, tok_v, dest_v, buf, sem_g, sem_w):
        wid = lax.axis_index("s") * num_cores + lax.axis_index("c")
        base = pl.multiple_of(wid * per_w, SC_CHUNK)
        lane = lax.iota(I32, lanes)

        @pl.loop(0, per_w, step=lanes)
        def _(r):
            tok_v[pl.ds(r, lanes)] = lax.rem(base + r + lane, n)

        @pl.loop(0, n_assign // SC_DEST_CHUNK)
        def _(c):
            first = pl.multiple_of(c * SC_DEST_CHUNK, SC_DEST_CHUNK)
            pltpu.sync_copy(dest_hbm.at[pl.ds(first, SC_DEST_CHUNK)], dest_v)

            @pl.loop(0, SC_DEST_CHUNK, step=lanes)
            def _(i):
                local = dest_v[pl.ds(i, lanes)] - base
                mine = (local >= 0) & (local < per_w)
                tok = lax.rem(first + i + lane, n)
                plsc.store_scatter(tok_v, [jnp.where(mine, local, 0)], tok, mask=mine)

        _sc_stream_rows(table_hbm, tok_v, out_hbm, base, n_chunks, buf, sem_g, sem_w)

    return dispatch_kernel(table, dest)


def _moe_kernel(be_ref, nb_ref, x_ref, wgu_ref, bgu_ref, wd_ref, bd_ref, y_ref, wgu_sc, wd_sc):
    i = pl.program_id(0)

    @pl.when((i == 0) | (be_ref[i] != be_ref[jnp.maximum(i - 1, 0)]))
    def _():
        wgu_sc[...] = wgu_ref[0].astype(BF16)
        wd_sc[...] = wd_ref[0].astype(BF16)

    @pl.when(i < nb_ref[0])
    def _():
        x = jnp.concatenate(_unpack_halves(x_ref[...]), axis=1).astype(BF16)
        gu = _dot(x, wgu_sc[...]) + bgu_ref[0]
        g = jnp.minimum(gu[:, :D_MODEL], SWIGLU_LIMIT)
        u = jnp.clip(gu[:, D_MODEL:], -SWIGLU_LIMIT, SWIGLU_LIMIT)
        act = (u + 1.0) * (g * jax.nn.sigmoid(SWIGLU_ALPHA * g))
        y_ref[...] = _pack_halves(_dot(act.astype(BF16), wd_sc[...]) + bd_ref[0])

    @pl.when(i >= nb_ref[0])
    def _():
        y_ref[...] = jnp.zeros(y_ref.shape, y_ref.dtype)


def _moe(block_expert, n_used, x_rows, wgu, bgu, wd, bd):
    p = x_rows.shape[0]
    bm = MOE_BM
    grid_spec = pltpu.PrefetchScalarGridSpec(
        num_scalar_prefetch=2,
        grid=(p // bm,),
        in_specs=[pl.BlockSpec((bm, ROW_WORDS), lambda i, be, nb: (i, 0)),
                  pl.BlockSpec((1, D_MODEL, 2 * D_MODEL), lambda i, be, nb: (be[i], 0, 0)),
                  pl.BlockSpec((1, 1, 2 * D_MODEL), lambda i, be, nb: (be[i], 0, 0)),
                  pl.BlockSpec((1, D_MODEL, D_MODEL), lambda i, be, nb: (be[i], 0, 0)),
                  pl.BlockSpec((1, 1, D_MODEL), lambda i, be, nb: (be[i], 0, 0))],
        out_specs=pl.BlockSpec((bm, ROW_WORDS), lambda i, be, nb: (i, 0)),
        scratch_shapes=[pltpu.VMEM((D_MODEL, 2 * D_MODEL), BF16), pltpu.VMEM((D_MODEL, D_MODEL), BF16)],
    )
    return pl.pallas_call(
        _moe_kernel,
        grid_spec=grid_spec,
        out_shape=jax.ShapeDtypeStruct((p, ROW_WORDS), I32),
        compiler_params=pltpu.CompilerParams(dimension_semantics=("arbitrary",),
                                             vmem_limit_bytes=VMEM_LIMIT),
        name="moe",
    )(block_expert, n_used, x_rows, wgu, bgu, wd, bd)


def _final_kernel(x1_ref, yg_ref, gt_ref, p_ref, gpost_ref, wg_ref, bg_ref, wp_ref, gple_ref, o_ref):
    gt = gt_ref[...]
    lo, hi = _unpack_halves(yg_ref[0])
    y = gt[:, 0:1] * jnp.concatenate([lo, hi], axis=1)
    for k in range(1, TOP_K):
        lo, hi = _unpack_halves(yg_ref[k])
        y = y + gt[:, k:k + 1] * jnp.concatenate([lo, hi], axis=1)
    x2 = x1_ref[...] + _rms(y, gpost_ref[...])
    gate = jax.nn.sigmoid(_dot(x2.astype(BF16), wg_ref[...]) + bg_ref[...])
    pp = _dot(p_ref[...].astype(BF16), wp_ref[...])
    o_ref[...] = x2 + _rms(gate * pp, gple_ref[...])


def _final(x1, yg, gate_t, p, gpost, wg, bg, wp, gple):
    n = x1.shape[0]
    tok = lambda w: pl.BlockSpec((TM, w), lambda i: (i, 0))
    full = lambda a: pl.BlockSpec(a.shape, lambda i: (0,) * a.ndim)
    consts = (gpost, wg, bg, wp, gple)
    return pl.pallas_call(
        _final_kernel,
        grid=(n // TM,),
        in_specs=[tok(D_MODEL), pl.BlockSpec((TOP_K, TM, ROW_WORDS), lambda i: (0, i, 0)), tok(TOP_K),
                  tok(D_PLE)] + [full(a) for a in consts],
        out_specs=tok(D_MODEL),
        out_shape=jax.ShapeDtypeStruct((n, D_MODEL), F32),
        compiler_params=pltpu.CompilerParams(dimension_semantics=("arbitrary",),
                                             vmem_limit_bytes=VMEM_LIMIT),
        name="final",
    )(x1, yg, gate_t, p, *consts)


def _rope_tables(positions):
    inv_freq = 1.0 / (ROPE_THETA ** (jnp.arange(0, QK_ROPE, 2, dtype=F32) / QK_ROPE))
    freq = jnp.concatenate([jnp.zeros((QK_NOPE,), F32), inv_freq, inv_freq,
                            jnp.zeros((HEAD_PAD - QK_NOPE - QK_ROPE,), F32)])
    ang = positions.astype(F32).reshape(-1)[:, None] * freq
    return jnp.cos(ang), jnp.sin(ang)


def _rot_half_cols(w):
    half = w.shape[-1] // 2
    return jnp.concatenate([-w[..., half:], w[..., :half]], axis=-1)


def _pad_head(w):
    return jnp.pad(w, [(0, 0)] * (w.ndim - 1) + [(0, HEAD_PAD - w.shape[-1])])


def _layer(x, p_l, cos, sin, prm):
    (attn_pre_g, w_in, q_norm_g, w_uq, kv_norm_g, w_ukv, sg_norm_g, w_spatial, b_spatial, mla_out_g,
     sg_out_g, w_out, attn_post_g, ffn_pre_g, w_router, b_router, w_gate_up, b_gate_up, w_down, b_down,
     ffn_post_g, w_ple_gate, b_ple_gate, w_ple_proj, ple_norm_g) = prm
    b, s, _ = x.shape
    n = b * s
    xt = x.reshape(n, D_MODEL)
    row2 = lambda a: a.reshape(1, -1)

    w_kr = w_in[:, Q_LORA + KV_LORA:Q_LORA + KV_LORA + QK_ROPE]
    place = lambda w: jnp.pad(w, ((0, 0), (QK_NOPE, HEAD_PAD - QK_NOPE - QK_ROPE)))
    win_aug = jnp.concatenate(
        [w_in[:, :Q_LORA + KV_LORA], place(w_kr), place(_rot_half_cols(w_kr)),
         w_in[:, Q_LORA + KV_LORA + QK_ROPE:]], axis=-1).astype(BF16)
    w_q_rot = jnp.concatenate([jnp.zeros_like(w_uq[..., :QK_NOPE]), _rot_half_cols(w_uq[..., QK_NOPE:])],
                              axis=-1)
    wq_aug = jnp.concatenate([_pad_head(w_uq).reshape(Q_LORA, -1), _pad_head(w_q_rot).reshape(Q_LORA, -1)],
                             axis=-1).astype(BF16)
    wk_pad = _pad_head(w_ukv[..., :QK_NOPE]).reshape(KV_LORA, -1).astype(BF16)
    wv_t = w_ukv[..., QK_NOPE:].reshape(KV_LORA, -1).T.astype(BF16)
    head_of = jnp.arange(SG_WIDTH) // SG_HEAD_DIM
    gsum = (head_of[:, None] == head_of[None, :]).astype(BF16)
    wcat = w_spatial.transpose(1, 0, 2).reshape(SG_CHUNK, SG_HEADS * SG_CHUNK)
    bsp = jnp.repeat(b_spatial.T, SG_HEAD_DIM, axis=1)

    q, k, vt, mix_sg = _inproj(xt, cos, sin, row2(attn_pre_g), win_aug, row2(q_norm_g), wq_aug,
                               row2(kv_norm_g), wk_pad, wv_t, row2(sg_norm_g), gsum, wcat, bsp,
                               row2(sg_out_g))
    mix_mla = _attention(q, k, vt, row2(mla_out_g), b, s)

    w_out_b = w_out.astype(BF16)
    x1, h2, idx, gate, rank, cnt = _postattn(
        xt, mix_mla, mix_sg, w_out_b[:MLA_WIDTH], w_out_b[MLA_WIDTH:], row2(attn_post_g), row2(ffn_pre_g),
        w_router.T.astype(BF16), b_router.reshape(N_EXPERTS, 1))

    a = n * TOP_K
    counts = cnt[:, 0].astype(I32)
    padded = (counts + MOE_BM - 1) // MOE_BM * MOE_BM
    pad_end = jnp.cumsum(padded)
    pad_start = pad_end - padded
    start_of = jnp.sum(jnp.where(idx[..., None] == jnp.arange(N_EXPERTS, dtype=I32), pad_start, 0), axis=-1)
    dest = (start_of + rank).reshape(a)
    n_blocks = a // MOE_BM + N_EXPERTS
    block_start = jnp.arange(n_blocks, dtype=I32) * MOE_BM
    n_used = (pad_end[-1] // MOE_BM).astype(I32)
    block_expert = jnp.sum(pad_end[None, :] <= jnp.minimum(block_start, pad_end[-1] - MOE_BM)[:, None],
                           axis=-1).astype(I32)

    x_rows = _sc_dispatch(h2, dest, n_blocks * MOE_BM)
    y_rows = _moe(block_expert, n_used.reshape(1), x_rows, w_gate_up, b_gate_up[:, None, :],
                  w_down, b_down[:, None, :])
    yg = _sc_gather(y_rows, dest).reshape(TOP_K, n, ROW_WORDS)

    out = _final(x1, yg, gate.T, p_l.reshape(n, D_PLE), row2(ffn_post_g), w_ple_gate.astype(BF16),
                 row2(b_ple_gate), w_ple_proj.astype(BF16), row2(ple_norm_g))
    return out.reshape(b, s, D_MODEL)


def kernel(x, p, positions, attn_pre_g, w_in, q_norm_g, w_uq, kv_norm_g, w_ukv, sg_norm_g, w_spatial, b_spatial, mla_out_g, sg_out_g, w_out, attn_post_g, ffn_pre_g, w_router, b_router, w_gate_up, b_gate_up, w_down, b_down, ffn_post_g, w_ple_gate, b_ple_gate, w_ple_proj, ple_norm_g):
    cos, sin = _rope_tables(positions)
    params = (attn_pre_g, w_in, q_norm_g, w_uq, kv_norm_g, w_ukv, sg_norm_g, w_spatial, b_spatial, mla_out_g,
              sg_out_g, w_out, attn_post_g, ffn_pre_g, w_router, b_router, w_gate_up, b_gate_up, w_down, b_down,
              ffn_post_g, w_ple_gate, b_ple_gate, w_ple_proj, ple_norm_g)
    for layer in range(p.shape[0]):
        x = _layer(x, p[layer], cos, sin, tuple(a[layer] for a in params))
    return x
```

```python
import functools

import jax
import jax.numpy as jnp
from jax import lax
from jax.experimental import pallas as pl
from jax.experimental.pallas import tpu as pltpu
from jax.experimental.pallas import tpu_sc as plsc

F32 = jnp.float32
BF16 = jnp.bfloat16
I32 = jnp.int32

D_MODEL = 1024
HEADS = 8
QK_NOPE = 64
QK_ROPE = 32
V_HEAD = 64
Q_LORA = 256
KV_LORA = 128
ROPE_THETA = 10000.0
SG_HEADS = 8
SG_HEAD_DIM = 64
SG_CHUNK = 128
SG_WIDTH = SG_HEADS * SG_HEAD_DIM
MLA_WIDTH = HEADS * V_HEAD
N_EXPERTS = 32
TOP_K = 4
SWIGLU_LIMIT = 7.0
SWIGLU_ALPHA = 1.702
D_PLE = 256
EPS = 1e-6

LANES = 128
HEAD_PAD = LANES
QK_SCALE = (QK_NOPE + QK_ROPE) ** -0.5
LOG2E = 1.4426950408889634

COL_CQ = 0
COL_CKV = COL_CQ + Q_LORA
COL_KR = COL_CKV + KV_LORA
COL_KRS = COL_KR + LANES
COL_SG = COL_KRS + LANES
IN_COLS_AUG = COL_SG + 2 * SG_WIDTH

TM = 512
BQ = 512
ATTN_BK = 1024
ATTN_ONES_ROWS = 16
MOE_BM = 256
VMEM_LIMIT = 56 * 1024 * 1024

NT_DIMS = (((1,), (1,)), ((), ()))
ROW_WORDS = D_MODEL // 2
SC_CHUNK = 64
SC_DEST_CHUNK = 8192


def _rms(x, g):
    return x * lax.rsqrt(jnp.mean(x * x, axis=-1, keepdims=True) + EPS) * g


def _dot(a, b):
    return jnp.dot(a, b, preferred_element_type=F32)


def _dot_nt(a, b):
    return lax.dot_general(a, b, NT_DIMS, preferred_element_type=F32)


def _pack_halves(x):
    half = x.shape[1] // 2
    return pltpu.pack_elementwise([x[:, :half], x[:, half:]], packed_dtype=BF16)


def _unpack_halves(w):
    return (pltpu.unpack_elementwise(w, index=0, packed_dtype=BF16, unpacked_dtype=F32),
            pltpu.unpack_elementwise(w, index=1, packed_dtype=BF16, unpacked_dtype=F32))


def _inproj_kernel(x_ref, cos_ref, sin_ref, gpre_ref, win_ref, qg_ref, wq_ref, kvg_ref, wk_ref, wvt_ref,
                   sgg_ref, gsum_ref, wcat_ref, bsp_ref, sgo_ref,
                   q_out, k_out, vt_out, sg_out):
    x = x_ref[...]
    h = _rms(x, gpre_ref[...])
    z = _dot(h.astype(BF16), win_ref[...])
    cos = cos_ref[...]
    sin = sin_ref[...]

    cqn = _rms(z[:, COL_CQ:COL_CQ + Q_LORA], qg_ref[...])
    qq = _dot(cqn.astype(BF16), wq_ref[...])
    half = HEADS * HEAD_PAD
    for hd in range(HEADS):
        sl = slice(hd * HEAD_PAD, (hd + 1) * HEAD_PAD)
        sl2 = slice(half + hd * HEAD_PAD, half + (hd + 1) * HEAD_PAD)
        q_out[:, sl] = ((qq[:, sl] * cos + qq[:, sl2] * sin) * (QK_SCALE * LOG2E)).astype(BF16)

    ckvn = _rms(z[:, COL_CKV:COL_CKV + KV_LORA], kvg_ref[...]).astype(BF16)
    kk = _dot(ckvn, wk_ref[...])
    kr = z[:, COL_KR:COL_KR + LANES] * cos + z[:, COL_KRS:COL_KRS + LANES] * sin
    for hd in range(HEADS):
        sl = slice(hd * HEAD_PAD, (hd + 1) * HEAD_PAD)
        k_out[:, sl] = (kk[:, sl] + kr).astype(BF16)
    vt_out[...] = _dot_nt(wvt_ref[...], ckvn).astype(BF16)

    zg = jax.nn.gelu(z[:, COL_SG:COL_SG + 2 * SG_WIDTH])
    u = zg[:, :SG_WIDTH]
    v = zg[:, SG_WIDTH:]
    v2 = v * v
    v2_hi = v2.astype(BF16)
    v2_lo = (v2 - v2_hi.astype(F32)).astype(BF16)
    gsum = gsum_ref[...]
    ms = (_dot(v2_hi, gsum) + _dot(v2_lo, gsum)) * (1.0 / SG_HEAD_DIM)
    vn = v * lax.rsqrt(ms + EPS) * sgg_ref[...]

    row = lax.broadcasted_iota(I32, (SG_CHUNK, SG_HEADS * SG_CHUNK), 0)
    col = lax.broadcasted_iota(I32, (SG_CHUNK, SG_HEADS * SG_CHUNK), 1)
    wcat = jnp.where((col % SG_CHUNK) <= row, wcat_ref[...], 0.0).astype(BF16)
    lane_head = lax.broadcasted_iota(I32, (SG_CHUNK, SG_WIDTH), 1) // SG_HEAD_DIM
    bsp = bsp_ref[...]
    sgo = sgo_ref[...]
    for c in range(x.shape[0] // SG_CHUNK):
        rows = slice(c * SG_CHUNK, (c + 1) * SG_CHUNK)
        vc = vn[rows]
        vbd = jnp.concatenate(
            [jnp.where(lane_head == hd, vc, 0.0).astype(BF16) for hd in range(SG_HEADS)], axis=0)
        vm = _dot(wcat, vbd) + bsp
        sg_out[rows, :] = _rms(u[rows] * vm, sgo).astype(BF16)


def _inproj(x, cos, sin, gpre, win, qg, wq, kvg, wk, wvt, sgg, gsum, wcat, bsp, sgo):
    n = x.shape[0]
    tok = lambda w: pl.BlockSpec((TM, w), lambda i: (i, 0))
    full = lambda a: pl.BlockSpec(a.shape, lambda i: (0,) * a.ndim)
    consts = (gpre, win, qg, wq, kvg, wk, wvt, sgg, gsum, wcat, bsp, sgo)
    return pl.pallas_call(
        _inproj_kernel,
        grid=(n // TM,),
        in_specs=[tok(D_MODEL), tok(LANES), tok(LANES)] + [full(a) for a in consts],
        out_specs=[tok(HEADS * HEAD_PAD), tok(HEADS * HEAD_PAD),
                   pl.BlockSpec((MLA_WIDTH, TM), lambda i: (0, i)), tok(SG_WIDTH)],
        out_shape=[jax.ShapeDtypeStruct((n, HEADS * HEAD_PAD), BF16),
                   jax.ShapeDtypeStruct((n, HEADS * HEAD_PAD), BF16),
                   jax.ShapeDtypeStruct((MLA_WIDTH, n), BF16),
                   jax.ShapeDtypeStruct((n, SG_WIDTH), BF16)],
        compiler_params=pltpu.CompilerParams(dimension_semantics=("arbitrary",),
                                             vmem_limit_bytes=VMEM_LIMIT),
        name="inproj",
    )(x, cos, sin, *consts)


def _attn_kernel(q_ref, k_ref, vt_ref, g_ref, o_ref, m_sc, l_sc, acc_sc):
    i = pl.program_id(1)
    bq = q_ref.shape[0]
    m_sc[...] = jnp.full(m_sc.shape, -jnp.inf, F32)
    l_sc[...] = jnp.zeros(l_sc.shape, F32)
    acc_sc[...] = jnp.zeros(acc_sc.shape, F32)

    def qk(start, nk, hd):
        hs = slice(hd * HEAD_PAD, (hd + 1) * HEAD_PAD)
        return _dot_nt(k_ref[pl.ds(start, nk), hs], q_ref[:, hs])

    def block(start, nk, masked):
        ones = jnp.ones((ATTN_ONES_ROWS, nk), BF16)
        queue = [qk(start, nk, 0), qk(start, nk, 1)]
        for hd in range(HEADS):
            vs = slice(hd * V_HEAD, (hd + 1) * V_HEAD)
            st = queue.pop(0)
            if hd + 2 < HEADS:
                queue.append(qk(start, nk, hd + 2))
            if masked:
                causal = (lax.broadcasted_iota(I32, (nk, bq), 0) <= lax.broadcasted_iota(I32, (nk, bq), 1))
                st = jnp.where(causal, st, -jnp.inf)
            m_prev = m_sc[hd:hd + 1, :]
            m_new = jnp.maximum(m_prev, jnp.max(st, axis=0, keepdims=True))
            alpha = jnp.exp2(m_prev - m_new)
            p = jnp.exp2(st - m_new).astype(BF16)
            vta = jnp.concatenate([vt_ref[vs, pl.ds(start, nk)], ones], axis=0)
            pv = _dot(vta, p)
            l_sc[hd:hd + 1, :] = alpha * l_sc[hd:hd + 1, :] + pv[V_HEAD:V_HEAD + 1, :]
            acc_sc[vs, :] = alpha * acc_sc[vs, :] + pv[:V_HEAD, :]
            m_sc[hd:hd + 1, :] = m_new

    def body(kb, carry):
        block(pl.multiple_of(kb * ATTN_BK, ATTN_BK), ATTN_BK, False)
        return carry

    visible = i * bq
    lax.fori_loop(0, visible // ATTN_BK, body, 0)
    for r in range(ATTN_BK // bq - 1, 0, -1):

        @pl.when(visible % ATTN_BK >= r * bq)
        def _():
            block(pl.multiple_of((i - r) * bq, bq), bq, False)

    block(pl.multiple_of(i * bq, bq), bq, True)
    ot = jnp.concatenate(
        [acc_sc[hd * V_HEAD:(hd + 1) * V_HEAD, :] * (1.0 / l_sc[hd:hd + 1, :]) for hd in range(HEADS)], axis=0)
    o_ref[...] = _rms(ot.T, g_ref[...]).astype(BF16)


def _attention(q, k, vt, g, b, s):
    nq = s // BQ
    return pl.pallas_call(
        _attn_kernel,
        grid=(b, nq),
        in_specs=[pl.BlockSpec((BQ, HEADS * HEAD_PAD), lambda bi, i: (bi * nq + i, 0)),
                  pl.BlockSpec((s, HEADS * HEAD_PAD), lambda bi, i: (bi, 0), pipeline_mode=pl.Buffered(1)),
                  pl.BlockSpec((MLA_WIDTH, s), lambda bi, i: (0, bi), pipeline_mode=pl.Buffered(1)),
                  pl.BlockSpec(g.shape, lambda bi, i: (0, 0))],
        out_specs=pl.BlockSpec((BQ, MLA_WIDTH), lambda bi, i: (bi * nq + i, 0)),
        out_shape=jax.ShapeDtypeStruct((b * s, MLA_WIDTH), BF16),
        scratch_shapes=[pltpu.VMEM((HEADS, BQ), F32), pltpu.VMEM((HEADS, BQ), F32),
                        pltpu.VMEM((MLA_WIDTH, BQ), F32)],
        compiler_params=pltpu.CompilerParams(dimension_semantics=("arbitrary", "arbitrary"),
                                             vmem_limit_bytes=VMEM_LIMIT),
        name="attention",
    )(q, k, vt, g)


def _postattn_kernel(x_ref, mla_ref, sg_ref, wo1_ref, wo2_ref, gpost_ref, gffn_ref, wrt_ref, brt_ref,
                     x1_out, h2_out, idx_out, gate_out, rank_out, cnt_out, cnt_sc):
    i = pl.program_id(0)

    @pl.when(i == 0)
    def _():
        cnt_sc[...] = jnp.zeros(cnt_sc.shape, F32)

    a = _dot(mla_ref[...], wo1_ref[...]) + _dot(sg_ref[...], wo2_ref[...])
    x1 = x_ref[...] + _rms(a, gpost_ref[...])
    x1_out[...] = x1
    h2f = _rms(x1, gffn_ref[...])
    h2_out[...] = _pack_halves(h2f)
    h2 = h2f.astype(BF16)

    tm = h2.shape[0]
    logits = _dot_nt(wrt_ref[...], h2) + brt_ref[...]
    eidx = lax.broadcasted_iota(I32, (N_EXPERTS, tm), 0)
    vals, idxs, sels = [], [], []
    for _ in range(TOP_K):
        mx = jnp.max(logits, axis=0, keepdims=True)
        ik = jnp.min(jnp.where(logits == mx, eidx, N_EXPERTS), axis=0, keepdims=True)
        sel = eidx == ik
        vals.append(mx)
        idxs.append(ik)
        sels.append(sel)
        logits = jnp.where(sel, -jnp.inf, logits)
    ex = [jnp.exp(v - vals[0]) for v in vals]
    den = ex[0] + ex[1] + ex[2] + ex[3]
    gate_out[...] = jnp.concatenate([e / den for e in ex], axis=0)
    idx_out[...] = jnp.concatenate(idxs, axis=0)

    maskf = sum(jnp.where(s, 1.0, 0.0) for s in sels)
    before = (lax.broadcasted_iota(I32, (tm, tm), 0) < lax.broadcasted_iota(I32, (tm, tm), 1))
    prefix = _dot(maskf.astype(BF16), jnp.where(before, 1.0, 0.0).astype(BF16))
    base = cnt_sc[...]
    tot = base + prefix
    ranks = [jnp.sum(jnp.where(s, tot, 0.0), axis=0, keepdims=True) for s in sels]
    rank_out[...] = jnp.concatenate(ranks, axis=0).astype(I32)
    cnt = base + jnp.sum(maskf, axis=1, keepdims=True)
    cnt_sc[...] = cnt
    cnt_out[...] = jnp.broadcast_to(cnt, cnt_out.shape)


def _postattn(x, mla, sg, wo1, wo2, gpost, gffn, wrt, brt):
    n = x.shape[0]
    tok = lambda w: pl.BlockSpec((TM, w), lambda i: (i, 0))
    tokt = pl.BlockSpec((TOP_K, TM), lambda i: (0, i))
    full = lambda a: pl.BlockSpec(a.shape, lambda i: (0,) * a.ndim)
    consts = (wo1, wo2, gpost, gffn, wrt, brt)
    return pl.pallas_call(
        _postattn_kernel,
        grid=(n // TM,),
        in_specs=[tok(D_MODEL), tok(MLA_WIDTH), tok(SG_WIDTH)] + [full(a) for a in consts],
        out_specs=[tok(D_MODEL), tok(ROW_WORDS), tokt, tokt, tokt,
                   pl.BlockSpec((N_EXPERTS, LANES), lambda i: (0, 0))],
        out_shape=[jax.ShapeDtypeStruct((n, D_MODEL), F32),
                   jax.ShapeDtypeStruct((n, ROW_WORDS), I32),
                   jax.ShapeDtypeStruct((TOP_K, n), I32),
                   jax.ShapeDtypeStruct((TOP_K, n), F32),
                   jax.ShapeDtypeStruct((TOP_K, n), I32),
                   jax.ShapeDtypeStruct((N_EXPERTS, LANES), F32)],
        scratch_shapes=[pltpu.VMEM((N_EXPERTS, 1), F32)],
        compiler_params=pltpu.CompilerParams(dimension_semantics=("arbitrary",),
                                             vmem_limit_bytes=VMEM_LIMIT),
        name="postattn",
    )(x, mla, sg, *consts)


def _sc_workers():
    info = plsc.get_sparse_core_info()
    return info.num_cores, info.num_cores * info.num_subcores, info.num_lanes


def _sc_stream_rows(table_hbm, idx_v, out_hbm, base, n_chunks, buf, sem_g, sem_w):
    def gather(j, b):
        rows = idx_v.at[pl.ds(pl.multiple_of(j * SC_CHUNK, SC_CHUNK), SC_CHUNK)]
        return pltpu.make_async_copy(table_hbm.at[rows], buf.at[b], sem_g.at[b])

    def write(j, b):
        rows = pl.ds(pl.multiple_of(base + j * SC_CHUNK, SC_CHUNK), SC_CHUNK)
        return pltpu.make_async_copy(buf.at[b], out_hbm.at[rows], sem_w.at[b])

    gather(0, 0).start()

    @pl.loop(0, n_chunks, step=2)
    def _(j0):
        for b in range(2):
            j = j0 + b
            gather(j, b).wait()

            @pl.when(j >= 1)
            def _():
                write(j - 1, 1 - b).wait()

            @pl.when(j + 1 < n_chunks)
            def _():
                gather(j + 1, 1 - b).start()

            write(j, b).start()

    write(n_chunks - 1, 1).wait()


def _sc_gather(table, idx):
    n_out = idx.shape[0]
    width = table.shape[1]
    num_cores, workers, _ = _sc_workers()
    per_w = n_out // workers
    n_chunks = per_w // SC_CHUNK
    assert per_w * workers == n_out and n_chunks * SC_CHUNK == per_w and n_chunks % 2 == 0

    @functools.partial(
        pl.kernel, mesh=plsc.VectorSubcoreMesh(core_axis_name="c", subcore_axis_name="s"),
        out_type=jax.ShapeDtypeStruct((n_out, width), table.dtype),
        scratch_types=[pltpu.VMEM((per_w,), I32), pltpu.VMEM((2, SC_CHUNK, width), table.dtype),
                       pltpu.SemaphoreType.DMA((2,)), pltpu.SemaphoreType.DMA((2,))])
    def gather_kernel(table_hbm, idx_hbm, out_hbm, idx_v, buf, sem_g, sem_w):
        wid = lax.axis_index("s") * num_cores + lax.axis_index("c")
        base = pl.multiple_of(wid * per_w, SC_CHUNK)
        pltpu.sync_copy(idx_hbm.at[pl.ds(base, per_w)], idx_v)
        _sc_stream_rows(table_hbm, idx_v, out_hbm, base, n_chunks, buf, sem_g, sem_w)

    return gather_kernel(table, idx)


def _sc_dispatch(table, dest, n_out):
    n_assign = dest.shape[0]
    n, width = table.shape
    num_cores, workers, lanes = _sc_workers()
    per_w = n_out // workers
    n_chunks = per_w // SC_CHUNK
    assert per_w * workers == n_out and n_chunks * SC_CHUNK == per_w and n_chunks % 2 == 0
    assert n_assign % SC_DEST_CHUNK == 0 and SC_DEST_CHUNK % lanes == 0 and per_w % lanes == 0

    @functools.partial(
        pl.kernel, mesh=plsc.VectorSubcoreMesh(core_axis_name="c", subcore_axis_name="s"),
        out_type=jax.ShapeDtypeStruct((n_out, width), table.dtype),
        scratch_types=[pltpu.VMEM((per_w,), I32), pltpu.VMEM((SC_DEST_CHUNK,), I32),
                       pltpu.VMEM((2, SC_CHUNK, width), table.dtype),
                       pltpu.SemaphoreType.DMA((2,)), pltpu.SemaphoreType.DMA((2,))],
        compiler_params=pltpu.CompilerParams(needs_layout_passes=False))
    def dispatch_kernel(table_hbm, dest_hbm, out_hbm, tok_v, dest_v, buf, sem_g, sem_w):
        wid = lax.axis_index("s") * num_cores + lax.axis_index("c")
        base = pl.multiple_of(wid * per_w, SC_CHUNK)
        lane = lax.iota(I32, lanes)

        @pl.loop(0, per_w, step=lanes)
        def _(r):
            tok_v[pl.ds(r, lanes)] = lax.rem(base + r + lane, n)

        @pl.loop(0, n_assign // SC_DEST_CHUNK)
        def _(c):
            first = pl.multiple_of(c * SC_DEST_CHUNK, SC_DEST_CHUNK)
            pltpu.sync_copy(dest_hbm.at[pl.ds(first, SC_DEST_CHUNK)], dest_v)

            @pl.loop(0, SC_DEST_CHUNK, step=lanes)
            def _(i):
                local = dest_v[pl.ds(i, lanes)] - base
                mine = (local >= 0) & (local < per_w)
                tok = lax.rem(first + i + lane, n)
                plsc.store_scatter(tok_v, [jnp.where(mine, local, 0)], tok, mask=mine)

        _sc_stream_rows(table_hbm, tok_v, out_hbm, base, n_chunks, buf, sem_g, sem_w)

    return dispatch_kernel(table, dest)


def _moe_kernel(be_ref, nb_ref, x_ref, wgu_ref, bgu_ref, wd_ref, bd_ref, y_ref, wgu_sc, wd_sc):
    i = pl.program_id(0)

    @pl.when((i == 0) | (be_ref[i] != be_ref[jnp.maximum(i - 1, 0)]))
    def _():
        wgu_sc[...] = wgu_ref[0].astype(BF16)
        wd_sc[...] = wd_ref[0].astype(BF16)

    @pl.when(i < nb_ref[0])
    def _():
        x = jnp.concatenate(_unpack_halves(x_ref[...]), axis=1).astype(BF16)
        gu = _dot(x, wgu_sc[...]) + bgu_ref[0]
        g = jnp.minimum(gu[:, :D_MODEL], SWIGLU_LIMIT)
        u = jnp.clip(gu[:, D_MODEL:], -SWIGLU_LIMIT, SWIGLU_LIMIT)
        act = (u + 1.0) * (g * jax.nn.sigmoid(SWIGLU_ALPHA * g))
        y_ref[...] = _pack_halves(_dot(act.astype(BF16), wd_sc[...]) + bd_ref[0])

    @pl.when(i >= nb_ref[0])
    def _():
        y_ref[...] = jnp.zeros(y_ref.shape, y_ref.dtype)


def _moe(block_expert, n_used, x_rows, wgu, bgu, wd, bd):
    p = x_rows.shape[0]
    bm = MOE_BM
    grid_spec = pltpu.PrefetchScalarGridSpec(
        num_scalar_prefetch=2,
        grid=(p // bm,),
        in_specs=[pl.BlockSpec((bm, ROW_WORDS), lambda i, be, nb: (i, 0)),
                  pl.BlockSpec((1, D_MODEL, 2 * D_MODEL), lambda i, be, nb: (be[i], 0, 0)),
                  pl.BlockSpec((1, 1, 2 * D_MODEL), lambda i, be, nb: (be[i], 0, 0)),
                  pl.BlockSpec((1, D_MODEL, D_MODEL), lambda i, be, nb: (be[i], 0, 0)),
                  pl.BlockSpec((1, 1, D_MODEL), lambda i, be, nb: (be[i], 0, 0))],
        out_specs=pl.BlockSpec((bm, ROW_WORDS), lambda i, be, nb: (i, 0)),
        scratch_shapes=[pltpu.VMEM((D_MODEL, 2 * D_MODEL), BF16), pltpu.VMEM((D_MODEL, D_MODEL), BF16)],
    )
    return pl.pallas_call(
        _moe_kernel,
        grid_spec=grid_spec,
        out_shape=jax.ShapeDtypeStruct((p, ROW_WORDS), I32),
        compiler_params=pltpu.CompilerParams(dimension_semantics=("arbitrary",),
                                             vmem_limit_bytes=VMEM_LIMIT),
        name="moe",
    )(block_expert, n_used, x_rows, wgu, bgu, wd, bd)


def _final_kernel(x1_ref, yg_ref, gt_ref, p_ref, gpost_ref, wg_ref, bg_ref, wp_ref, gple_ref, o_ref):
    gt = gt_ref[...]
    lo, hi = _unpack_halves(yg_ref[0])
    y = gt[:, 0:1] * jnp.concatenate([lo, hi], axis=1)
    for k in range(1, TOP_K):
        lo, hi = _unpack_halves(yg_ref[k])
        y = y + gt[:, k:k + 1] * jnp.concatenate([lo, hi], axis=1)
    x2 = x1_ref[...] + _rms(y, gpost_ref[...])
    gate = jax.nn.sigmoid(_dot(x2.astype(BF16), wg_ref[...]) + bg_ref[...])
    pp = _dot(p_ref[...].astype(BF16), wp_ref[...])
    o_ref[...] = x2 + _rms(gate * pp, gple_ref[...])


def _final(x1, yg, gate_t, p, gpost, wg, bg, wp, gple):
    n = x1.shape[0]
    tok = lambda w: pl.BlockSpec((TM, w), lambda i: (i, 0))
    full = lambda a: pl.BlockSpec(a.shape, lambda i: (0,) * a.ndim)
    consts = (gpost, wg, bg, wp, gple)
    return pl.pallas_call(
        _final_kernel,
        grid=(n // TM,),
        in_specs=[tok(D_MODEL), pl.BlockSpec((TOP_K, TM, ROW_WORDS), lambda i: (0, i, 0)), tok(TOP_K),
                  tok(D_PLE)] + [full(a) for a in consts],
        out_specs=tok(D_MODEL),
        out_shape=jax.ShapeDtypeStruct((n, D_MODEL), F32),
        compiler_params=pltpu.CompilerParams(dimension_semantics=("arbitrary",),
                                             vmem_limit_bytes=VMEM_LIMIT),
        name="final",
    )(x1, yg, gate_t, p, *consts)


def _rope_tables(positions):
    inv_freq = 1.0 / (ROPE_THETA ** (jnp.arange(0, QK_ROPE, 2, dtype=F32) / QK_ROPE))
    freq = jnp.concatenate([jnp.zeros((QK_NOPE,), F32), inv_freq, inv_freq,
                            jnp.zeros((HEAD_PAD - QK_NOPE - QK_ROPE,), F32)])
    ang = positions.astype(F32).reshape(-1)[:, None] * freq
    return jnp.cos(ang), jnp.sin(ang)


def _rot_half_cols(w):
    half = w.shape[-1] // 2
    return jnp.concatenate([-w[..., half:], w[..., :half]], axis=-1)


def _pad_head(w):
    return jnp.pad(w, [(0, 0)] * (w.ndim - 1) + [(0, HEAD_PAD - w.shape[-1])])


def _layer(x, p_l, cos, sin, prm):
    (attn_pre_g, w_in, q_norm_g, w_uq, kv_norm_g, w_ukv, sg_norm_g, w_spatial, b_spatial, mla_out_g,
     sg_out_g, w_out, attn_post_g, ffn_pre_g, w_router, b_router, w_gate_up, b_gate_up, w_down, b_down,
     ffn_post_g, w_ple_gate, b_ple_gate, w_ple_proj, ple_norm_g) = prm
    b, s, _ = x.shape
    n = b * s
    xt = x.reshape(n, D_MODEL)
    row2 = lambda a: a.reshape(1, -1)

    w_kr = w_in[:, Q_LORA + KV_LORA:Q_LORA + KV_LORA + QK_ROPE]
    place = lambda w: jnp.pad(w, ((0, 0), (QK_NOPE, HEAD_PAD - QK_NOPE - QK_ROPE)))
    win_aug = jnp.concatenate(
        [w_in[:, :Q_LORA + KV_LORA], place(w_kr), place(_rot_half_cols(w_kr)),
         w_in[:, Q_LORA + KV_LORA + QK_ROPE:]], axis=-1).astype(BF16)
    w_q_rot = jnp.concatenate([jnp.zeros_like(w_uq[..., :QK_NOPE]), _rot_half_cols(w_uq[..., QK_NOPE:])],
                              axis=-1)
    wq_aug = jnp.concatenate([_pad_head(w_uq).reshape(Q_LORA, -1), _pad_head(w_q_rot).reshape(Q_LORA, -1)],
                             axis=-1).astype(BF16)
    wk_pad = _pad_head(w_ukv[..., :QK_NOPE]).reshape(KV_LORA, -1).astype(BF16)
    wv_t = w_ukv[..., QK_NOPE:].reshape(KV_LORA, -1).T.astype(BF16)
    head_of = jnp.arange(SG_WIDTH) // SG_HEAD_DIM
    gsum = (head_of[:, None] == head_of[None, :]).astype(BF16)
    wcat = w_spatial.transpose(1, 0, 2).reshape(SG_CHUNK, SG_HEADS * SG_CHUNK)
    bsp = jnp.repeat(b_spatial.T, SG_HEAD_DIM, axis=1)

    q, k, vt, mix_sg = _inproj(xt, cos, sin, row2(attn_pre_g), win_aug, row2(q_norm_g), wq_aug,
                               row2(kv_norm_g), wk_pad, wv_t, row2(sg_norm_g), gsum, wcat, bsp,
                               row2(sg_out_g))
    mix_mla = _attention(q, k, vt, row2(mla_out_g), b, s)

    w_out_b = w_out.astype(BF16)
    x1, h2, idx, gate, rank, cnt = _postattn(
        xt, mix_mla, mix_sg, w_out_b[:MLA_WIDTH], w_out_b[MLA_WIDTH:], row2(attn_post_g), row2(ffn_pre_g),
        w_router.T.astype(BF16), b_router.reshape(N_EXPERTS, 1))

    a = n * TOP_K
    counts = cnt[:, 0].astype(I32)
    padded = (counts + MOE_BM - 1) // MOE_BM * MOE_BM
    pad_end = jnp.cumsum(padded)
    pad_start = pad_end - padded
    start_of = jnp.sum(jnp.where(idx[..., None] == jnp.arange(N_EXPERTS, dtype=I32), pad_start, 0), axis=-1)
    dest = (start_of + rank).reshape(a)
    n_blocks = a // MOE_BM + N_EXPERTS
    block_start = jnp.arange(n_blocks, dtype=I32) * MOE_BM
    n_used = (pad_end[-1] // MOE_BM).astype(I32)
    block_expert = jnp.sum(pad_end[None, :] <= jnp.minimum(block_start, pad_end[-1] - MOE_BM)[:, None],
                           axis=-1).astype(I32)

    x_rows = _sc_dispatch(h2, dest, n_blocks * MOE_BM)
    y_rows = _moe(block_expert, n_used.reshape(1), x_rows, w_gate_up, b_gate_up[:, None, :],
                  w_down, b_down[:, None, :])
    yg = _sc_gather(y_rows, dest).reshape(TOP_K, n, ROW_WORDS)

    out = _final(x1, yg, gate.T, p_l.reshape(n, D_PLE), row2(ffn_post_g), w_ple_gate.astype(BF16),
                 row2(b_ple_gate), w_ple_proj.astype(BF16), row2(ple_norm_g))
    return out.reshape(b, s, D_MODEL)


def kernel(x, p, positions, attn_pre_g, w_in, q_norm_g, w_uq, kv_norm_g, w_ukv, sg_norm_g, w_spatial, b_spatial, mla_out_g, sg_out_g, w_out, attn_post_g, ffn_pre_g, w_router, b_router, w_gate_up, b_gate_up, w_down, b_down, ffn_post_g, w_ple_gate, b_ple_gate, w_ple_proj, ple_norm_g):
    cos, sin = _rope_tables(positions)
    params = (attn_pre_g, w_in, q_norm_g, w_uq, kv_norm_g, w_ukv, sg_norm_g, w_spatial, b_spatial, mla_out_g,
              sg_out_g, w_out, attn_post_g, ffn_pre_g, w_router, b_router, w_gate_up, b_gate_up, w_down, b_down,
              ffn_post_g, w_ple_gate, b_ple_gate, w_ple_proj, ple_norm_g)
    for layer in range(p.shape[0]):
        x = _layer(x, p[layer], cos, sin, tuple(a[layer] for a in params))
    return x
```

```python
import functools

import jax
import jax.numpy as jnp
from jax import lax
from jax.experimental import pallas as pl
from jax.experimental.pallas import tpu as pltpu
from jax.experimental.pallas import tpu_sc as plsc

F32 = jnp.float32
BF16 = jnp.bfloat16
I32 = jnp.int32

D_MODEL = 1024
HEADS = 8
QK_NOPE = 64
QK_ROPE = 32
V_HEAD = 64
Q_LORA = 256
KV_LORA = 128
ROPE_THETA = 10000.0
SG_HEADS = 8
SG_HEAD_DIM = 64
SG_CHUNK = 128
SG_WIDTH = SG_HEADS * SG_HEAD_DIM
MLA_WIDTH = HEADS * V_HEAD
N_EXPERTS = 32
TOP_K = 4
SWIGLU_LIMIT = 7.0
SWIGLU_ALPHA = 1.702
D_PLE = 256
EPS = 1e-6

LANES = 128
HEAD_PAD = LANES
QK_SCALE = (QK_NOPE + QK_ROPE) ** -0.5
LOG2E = 1.4426950408889634

COL_CQ = 0
COL_CKV = COL_CQ + Q_LORA
COL_KR = COL_CKV + KV_LORA
COL_KRS = COL_KR + LANES
COL_SG = COL_KRS + LANES
IN_COLS_AUG = COL_SG + 2 * SG_WIDTH

TM = 512
BQ = 512
ATTN_BK = 1024
ATTN_ONES_ROWS = 16
MOE_BM = 256
VMEM_LIMIT = 56 * 1024 * 1024

NT_DIMS = (((1,), (1,)), ((), ()))
ROW_WORDS = D_MODEL // 2
SC_CHUNK = 64
SC_DEST_CHUNK = 8192


def _rms(x, g):
    return x * lax.rsqrt(jnp.mean(x * x, axis=-1, keepdims=True) + EPS) * g


def _dot(a, b):
    return jnp.dot(a, b, preferred_element_type=F32)


def _dot_nt(a, b):
    return lax.dot_general(a, b, NT_DIMS, preferred_element_type=F32)


def _pack_halves(x):
    half = x.shape[1] // 2
    return pltpu.pack_elementwise([x[:, :half], x[:, half:]], packed_dtype=BF16)


def _unpack_halves(w):
    return (pltpu.unpack_elementwise(w, index=0, packed_dtype=BF16, unpacked_dtype=F32),
            pltpu.unpack_elementwise(w, index=1, packed_dtype=BF16, unpacked_dtype=F32))


def _inproj_kernel(x_ref, cos_ref, sin_ref, gpre_ref, win_ref, qg_ref, wq_ref, kvg_ref, wk_ref, wvt_ref,
                   sgg_ref, gsum_ref, wcat_ref, bsp_ref, sgo_ref,
                   q_out, k_out, vt_out, sg_out):
    x = x_ref[...]
    h = _rms(x, gpre_ref[...])
    z = _dot(h.astype(BF16), win_ref[...])
    cos = cos_ref[...]
    sin = sin_ref[...]

    cqn = _rms(z[:, COL_CQ:COL_CQ + Q_LORA], qg_ref[...])
    qq = _dot(cqn.astype(BF16), wq_ref[...])
    half = HEADS * HEAD_PAD
    for hd in range(HEADS):
        sl = slice(hd * HEAD_PAD, (hd + 1) * HEAD_PAD)
        sl2 = slice(half + hd * HEAD_PAD, half + (hd + 1) * HEAD_PAD)
        q_out[:, sl] = ((qq[:, sl] * cos + qq[:, sl2] * sin) * (QK_SCALE * LOG2E)).astype(BF16)

    ckvn = _rms(z[:, COL_CKV:COL_CKV + KV_LORA], kvg_ref[...]).astype(BF16)
    kk = _dot(ckvn, wk_ref[...])
    kr = z[:, COL_KR:COL_KR + LANES] * cos + z[:, COL_KRS:COL_KRS + LANES] * sin
    for hd in range(HEADS):
        sl = slice(hd * HEAD_PAD, (hd + 1) * HEAD_PAD)
        k_out[:, sl] = (kk[:, sl] + kr).astype(BF16)
    vt_out[...] = _dot_nt(wvt_ref[...], ckvn).astype(BF16)

    zg = jax.nn.gelu(z[:, COL_SG:COL_SG + 2 * SG_WIDTH])
    u = zg[:, :SG_WIDTH]
    v = zg[:, SG_WIDTH:]
    v2 = v * v
    v2_hi = v2.astype(BF16)
    v2_lo = (v2 - v2_hi.astype(F32)).astype(BF16)
    gsum = gsum_ref[...]
    ms = (_dot(v2_hi, gsum) + _dot(v2_lo, gsum)) * (1.0 / SG_HEAD_DIM)
    vn = v * lax.rsqrt(ms + EPS) * sgg_ref[...]

    row = lax.broadcasted_iota(I32, (SG_CHUNK, SG_HEADS * SG_CHUNK), 0)
    col = lax.broadcasted_iota(I32, (SG_CHUNK, SG_HEADS * SG_CHUNK), 1)
    wcat = jnp.where((col % SG_CHUNK) <= row, wcat_ref[...], 0.0).astype(BF16)
    lane_head = lax.broadcasted_iota(I32, (SG_CHUNK, SG_WIDTH), 1) // SG_HEAD_DIM
    bsp = bsp_ref[...]
    sgo = sgo_ref[...]
    for c in range(x.shape[0] // SG_CHUNK):
        rows = slice(c * SG_CHUNK, (c + 1) * SG_CHUNK)
        vc = vn[rows]
        vbd = jnp.concatenate(
            [jnp.where(lane_head == hd, vc, 0.0).astype(BF16) for hd in range(SG_HEADS)], axis=0)
        vm = _dot(wcat, vbd) + bsp
        sg_out[rows, :] = _rms(u[rows] * vm, sgo).astype(BF16)


def _inproj(x, cos, sin, gpre, win, qg, wq, kvg, wk, wvt, sgg, gsum, wcat, bsp, sgo):
    n = x.shape[0]
    tok = lambda w: pl.BlockSpec((TM, w), lambda i: (i, 0))
    full = lambda a: pl.BlockSpec(a.shape, lambda i: (0,) * a.ndim)
    consts = (gpre, win, qg, wq, kvg, wk, wvt, sgg, gsum, wcat, bsp, sgo)
    return pl.pallas_call(
        _inproj_kernel,
        grid=(n // TM,),
        in_specs=[tok(D_MODEL), tok(LANES), tok(LANES)] + [full(a) for a in consts],
        out_specs=[tok(HEADS * HEAD_PAD), tok(HEADS * HEAD_PAD),
                   pl.BlockSpec((MLA_WIDTH, TM), lambda i: (0, i)), tok(SG_WIDTH)],
        out_shape=[jax.ShapeDtypeStruct((n, HEADS * HEAD_PAD), BF16),
                   jax.ShapeDtypeStruct((n, HEADS * HEAD_PAD), BF16),
                   jax.ShapeDtypeStruct((MLA_WIDTH, n), BF16),
                   jax.ShapeDtypeStruct((n, SG_WIDTH), BF16)],
        compiler_params=pltpu.CompilerParams(dimension_semantics=("arbitrary",),
                                             vmem_limit_bytes=VMEM_LIMIT),
        name="inproj",
    )(x, cos, sin, *consts)


def _attn_kernel(q_ref, k_ref, vt_ref, g_ref, o_ref, m_sc, l_sc, acc_sc):
    i = pl.program_id(1)
    bq = q_ref.shape[0]
    m_sc[...] = jnp.full(m_sc.shape, -jnp.inf, F32)
    l_sc[...] = jnp.zeros(l_sc.shape, F32)
    acc_sc[...] = jnp.zeros(acc_sc.shape, F32)

    def qk(start, nk, hd):
        hs = slice(hd * HEAD_PAD, (hd + 1) * HEAD_PAD)
        return _dot_nt(k_ref[pl.ds(start, nk), hs], q_ref[:, hs])

    def block(start, nk, masked):
        ones = jnp.ones((ATTN_ONES_ROWS, nk), BF16)
        queue = [qk(start, nk, 0), qk(start, nk, 1)]
        for hd in range(HEADS):
            vs = slice(hd * V_HEAD, (hd + 1) * V_HEAD)
            st = queue.pop(0)
            if hd + 2 < HEADS:
                queue.append(qk(start, nk, hd + 2))
            if masked:
                causal = (lax.broadcasted_iota(I32, (nk, bq), 0) <= lax.broadcasted_iota(I32, (nk, bq), 1))
                st = jnp.where(causal, st, -jnp.inf)
            m_prev = m_sc[hd:hd + 1, :]
            m_new = jnp.maximum(m_prev, jnp.max(st, axis=0, keepdims=True))
            alpha = jnp.exp2(m_prev - m_new)
            p = jnp.exp2(st - m_new).astype(BF16)
            vta = jnp.concatenate([vt_ref[vs, pl.ds(start, nk)], ones], axis=0)
            pv = _dot(vta, p)
            l_sc[hd:hd + 1, :] = alpha * l_sc[hd:hd + 1, :] + pv[V_HEAD:V_HEAD + 1, :]
            acc_sc[vs, :] = alpha * acc_sc[vs, :] + pv[:V_HEAD, :]
            m_sc[hd:hd + 1, :] = m_new

    def body(kb, carry):
        block(pl.multiple_of(kb * ATTN_BK, ATTN_BK), ATTN_BK, False)
        return carry

    visible = i * bq
    lax.fori_loop(0, visible // ATTN_BK, body, 0)
    for r in range(ATTN_BK // bq - 1, 0, -1):

        @pl.when(visible % ATTN_BK >= r * bq)
        def _():
            block(pl.multiple_of((i - r) * bq, bq), bq, False)

    block(pl.multiple_of(i * bq, bq), bq, True)
    ot = jnp.concatenate(
        [acc_sc[hd * V_HEAD:(hd + 1) * V_HEAD, :] * (1.0 / l_sc[hd:hd + 1, :]) for hd in range(HEADS)], axis=0)
    o_ref[...] = _rms(ot.T, g_ref[...]).astype(BF16)


def _attention(q, k, vt, g, b, s):
    nq = s // BQ
    return pl.pallas_call(
        _attn_kernel,
        grid=(b, nq),
        in_specs=[pl.BlockSpec((BQ, HEADS * HEAD_PAD), lambda bi, i: (bi * nq + i, 0)),
                  pl.BlockSpec((s, HEADS * HEAD_PAD), lambda bi, i: (bi, 0), pipeline_mode=pl.Buffered(1)),
                  pl.BlockSpec((MLA_WIDTH, s), lambda bi, i: (0, bi), pipeline_mode=pl.Buffered(1)),
                  pl.BlockSpec(g.shape, lambda bi, i: (0, 0))],
        out_specs=pl.BlockSpec((BQ, MLA_WIDTH), lambda bi, i: (bi * nq + i, 0)),
        out_shape=jax.ShapeDtypeStruct((b * s, MLA_WIDTH), BF16),
        scratch_shapes=[pltpu.VMEM((HEADS, BQ), F32), pltpu.VMEM((HEADS, BQ), F32),
                        pltpu.VMEM((MLA_WIDTH, BQ), F32)],
        compiler_params=pltpu.CompilerParams(dimension_semantics=("arbitrary", "arbitrary"),
                                             vmem_limit_bytes=VMEM_LIMIT),
        name="attention",
    )(q, k, vt, g)


def _postattn_kernel(x_ref, mla_ref, sg_ref, wo1_ref, wo2_ref, gpost_ref, gffn_ref, wrt_ref, brt_ref,
                     x1_out, h2_out, idx_out, gate_out, rank_out, cnt_out, cnt_sc):
    i = pl.program_id(0)

    @pl.when(i == 0)
    def _():
        cnt_sc[...] = jnp.zeros(cnt_sc.shape, F32)

    a = _dot(mla_ref[...], wo1_ref[...]) + _dot(sg_ref[...], wo2_ref[...])
    x1 = x_ref[...] + _rms(a, gpost_ref[...])
    x1_out[...] = x1
    h2f = _rms(x1, gffn_ref[...])
    h2_out[...] = _pack_halves(h2f)
    h2 = h2f.astype(BF16)

    tm = h2.shape[0]
    logits = _dot_nt(wrt_ref[...], h2) + brt_ref[...]
    eidx = lax.broadcasted_iota(I32, (N_EXPERTS, tm), 0)
    vals, idxs, sels = [], [], []
    for _ in range(TOP_K):
        mx = jnp.max(logits, axis=0, keepdims=True)
        ik = jnp.min(jnp.where(logits == mx, eidx, N_EXPERTS), axis=0, keepdims=True)
        sel = eidx == ik
        vals.append(mx)
        idxs.append(ik)
        sels.append(sel)
        logits = jnp.where(sel, -jnp.inf, logits)
    ex = [jnp.exp(v - vals[0]) for v in vals]
    den = ex[0] + ex[1] + ex[2] + ex[3]
    gate_out[...] = jnp.concatenate([e / den for e in ex], axis=0)
    idx_out[...] = jnp.concatenate(idxs, axis=0)

    maskf = sum(jnp.where(s, 1.0, 0.0) for s in sels)
    before = (lax.broadcasted_iota(I32, (tm, tm), 0) < lax.broadcasted_iota(I32, (tm, tm), 1))
    prefix = _dot(maskf.astype(BF16), jnp.where(before, 1.0, 0.0).astype(BF16))
    base = cnt_sc[...]
    tot = base + prefix
    ranks = [jnp.sum(jnp.where(s, tot, 0.0), axis=0, keepdims=True) for s in sels]
    rank_out[...] = jnp.concatenate(ranks, axis=0).astype(I32)
    cnt = base + jnp.sum(maskf, axis=1, keepdims=True)
    cnt_sc[...] = cnt
    cnt_out[...] = jnp.broadcast_to(cnt, cnt_out.shape)


def _postattn(x, mla, sg, wo1, wo2, gpost, gffn, wrt, brt):
    n = x.shape[0]
    tok = lambda w: pl.BlockSpec((TM, w), lambda i: (i, 0))
    tokt = pl.BlockSpec((TOP_K, TM), lambda i: (0, i))
    full = lambda a: pl.BlockSpec(a.shape, lambda i: (0,) * a.ndim)
    consts = (wo1, wo2, gpost, gffn, wrt, brt)
    return pl.pallas_call(
        _postattn_kernel,
        grid=(n // TM,),
        in_specs=[tok(D_MODEL), tok(MLA_WIDTH), tok(SG_WIDTH)] + [full(a) for a in consts],
        out_specs=[tok(D_MODEL), tok(ROW_WORDS), tokt, tokt, tokt,
                   pl.BlockSpec((N_EXPERTS, LANES), lambda i: (0, 0))],
        out_shape=[jax.ShapeDtypeStruct((n, D_MODEL), F32),
                   jax.ShapeDtypeStruct((n, ROW_WORDS), I32),
                   jax.ShapeDtypeStruct((TOP_K, n), I32),
                   jax.ShapeDtypeStruct((TOP_K, n), F32),
                   jax.ShapeDtypeStruct((TOP_K, n), I32),
                   jax.ShapeDtypeStruct((N_EXPERTS, LANES), F32)],
        scratch_shapes=[pltpu.VMEM((N_EXPERTS, 1), F32)],
        compiler_params=pltpu.CompilerParams(dimension_semantics=("arbitrary",),
                                             vmem_limit_bytes=VMEM_LIMIT),
        name="postattn",
    )(x, mla, sg, *consts)


def _sc_workers():
    info = plsc.get_sparse_core_info()
    return info.num_cores, info.num_cores * info.num_subcores, info.num_lanes


def _sc_stream_rows(table_hbm, idx_v, out_hbm, base, n_chunks, buf, sem_g, sem_w):
    def gather(j, b):
        rows = idx_v.at[pl.ds(pl.multiple_of(j * SC_CHUNK, SC_CHUNK), SC_CHUNK)]
        return pltpu.make_async_copy(table_hbm.at[rows], buf.at[b], sem_g.at[b])

    def write(j, b):
        rows = pl.ds(pl.multiple_of(base + j * SC_CHUNK, SC_CHUNK), SC_CHUNK)
        return pltpu.make_async_copy(buf.at[b], out_hbm.at[rows], sem_w.at[b])

    gather(0, 0).start()

    @pl.loop(0, n_chunks, step=2)
    def _(j0):
        for b in range(2):
            j = j0 + b
            gather(j, b).wait()

            @pl.when(j >= 1)
            def _():
                write(j - 1, 1 - b).wait()

            @pl.when(j + 1 < n_chunks)
            def _():
                gather(j + 1, 1 - b).start()

            write(j, b).start()

    write(n_chunks - 1, 1).wait()


def _sc_gather(table, idx):
    n_out = idx.shape[0]
    width = table.shape[1]
    num_cores, workers, _ = _sc_workers()
    per_w = n_out // workers
    n_chunks = per_w // SC_CHUNK
    assert per_w * workers == n_out and n_chunks * SC_CHUNK == per_w and n_chunks % 2 == 0

    @functools.partial(
        pl.kernel, mesh=plsc.VectorSubcoreMesh(core_axis_name="c", subcore_axis_name="s"),
        out_type=jax.ShapeDtypeStruct((n_out, width), table.dtype),
        scratch_types=[pltpu.VMEM((per_w,), I32), pltpu.VMEM((2, SC_CHUNK, width), table.dtype),
                       pltpu.SemaphoreType.DMA((2,)), pltpu.SemaphoreType.DMA((2,))])
    def gather_kernel(table_hbm, idx_hbm, out_hbm, idx_v, buf, sem_g, sem_w):
        wid = lax.axis_index("s") * num_cores + lax.axis_index("c")
        base = pl.multiple_of(wid * per_w, SC_CHUNK)
        pltpu.sync_copy(idx_hbm.at[pl.ds(base, per_w)], idx_v)
        _sc_stream_rows(table_hbm, idx_v, out_hbm, base, n_chunks, buf, sem_g, sem_w)

    return gather_kernel(table, idx)


def _sc_dispatch(table, dest, n_out, row_offset):
    n_assign = dest.shape[0]
    n, width = table.shape
    num_cores, workers, lanes = _sc_workers()
    per_w = n_out // workers
    n_chunks = per_w // SC_CHUNK
    assert per_w * workers == n_out and n_chunks * SC_CHUNK == per_w and n_chunks % 2 == 0
    assert n_assign % SC_DEST_CHUNK == 0 and SC_DEST_CHUNK % lanes == 0 and per_w % lanes == 0

    @functools.partial(
        pl.kernel, mesh=plsc.VectorSubcoreMesh(core_axis_name="c", subcore_axis_name="s"),
        out_type=jax.ShapeDtypeStruct((n_out, width), table.dtype),
        scratch_types=[pltpu.VMEM((per_w,), I32), pltpu.VMEM((SC_DEST_CHUNK,), I32),
                       pltpu.VMEM((2, SC_CHUNK, width), table.dtype),
                       pltpu.SemaphoreType.DMA((2,)), pltpu.SemaphoreType.DMA((2,))],
        compiler_params=pltpu.CompilerParams(needs_layout_passes=False))
    def dispatch_kernel(table_hbm, dest_hbm, out_hbm, tok_v, dest_v, buf, sem_g, sem_w):
        wid = lax.axis_index("s") * num_cores + lax.axis_index("c")
        base = pl.multiple_of(wid * per_w, SC_CHUNK)
        lane = lax.iota(I32, lanes)

        @pl.loop(0, per_w, step=lanes)
        def _(r):
            tok_v[pl.ds(r, lanes)] = lax.rem(row_offset + base + r + lane, n)

        @pl.loop(0, n_assign // SC_DEST_CHUNK)
        def _(c):
            first = pl.multiple_of(c * SC_DEST_CHUNK, SC_DEST_CHUNK)
            pltpu.sync_copy(dest_hbm.at[pl.ds(first, SC_DEST_CHUNK)], dest_v)

            @pl.loop(0, SC_DEST_CHUNK, step=lanes)
            def _(i):
                local = dest_v[pl.ds(i, lanes)] - (row_offset + base)
                mine = (local >= 0) & (local < per_w)
                tok = lax.rem(first + i + lane, n)
                plsc.store_scatter(tok_v, [jnp.where(mine, local, 0)], tok, mask=mine)

        _sc_stream_rows(table_hbm, tok_v, out_hbm, base, n_chunks, buf, sem_g, sem_w)

    return dispatch_kernel(table, dest)


def _moe_kernel(block_offset, be_ref, nb_ref, x_ref, wgu_ref, bgu_ref, wd_ref, bd_ref, *rest):
    y_ref, wgu_sc, wd_sc = rest[-3:]
    i = pl.program_id(0)
    blk = i + block_offset

    @pl.when((i == 0) | (be_ref[blk] != be_ref[jnp.maximum(blk - 1, 0)]))
    def _():
        wgu_sc[...] = wgu_ref[0].astype(BF16)
        wd_sc[...] = wd_ref[0].astype(BF16)

    @pl.when(blk < nb_ref[0])
    def _():
        x = jnp.concatenate(_unpack_halves(x_ref[...]), axis=1).astype(BF16)
        gu = _dot(x, wgu_sc[...]) + bgu_ref[0]
        g = jnp.minimum(gu[:, :D_MODEL], SWIGLU_LIMIT)
        u = jnp.clip(gu[:, D_MODEL:], -SWIGLU_LIMIT, SWIGLU_LIMIT)
        act = (u + 1.0) * (g * jax.nn.sigmoid(SWIGLU_ALPHA * g))
        y_ref[...] = _pack_halves(_dot(act.astype(BF16), wd_sc[...]) + bd_ref[0])

    @pl.when(blk >= nb_ref[0])
    def _():
        y_ref[...] = jnp.zeros(y_ref.shape, y_ref.dtype)


def _moe(block_expert, n_used, x_part, block_offset, y_prev, wgu, bgu, wd, bd):
    bm = MOE_BM
    off = block_offset
    n_prefetch = 2
    in_specs = [pl.BlockSpec((bm, ROW_WORDS), lambda i, be, nb: (i, 0)),
                pl.BlockSpec((1, D_MODEL, 2 * D_MODEL), lambda i, be, nb: (be[i + off], 0, 0)),
                pl.BlockSpec((1, 1, 2 * D_MODEL), lambda i, be, nb: (be[i + off], 0, 0)),
                pl.BlockSpec((1, D_MODEL, D_MODEL), lambda i, be, nb: (be[i + off], 0, 0)),
                pl.BlockSpec((1, 1, D_MODEL), lambda i, be, nb: (be[i + off], 0, 0))]
    args = [block_expert, n_used, x_part, wgu, bgu, wd, bd]
    aliases = {}
    if y_prev is not None:
        in_specs.append(pl.BlockSpec(memory_space=pl.ANY))
        aliases = {len(args): 0}
        args.append(y_prev)
    grid_spec = pltpu.PrefetchScalarGridSpec(
        num_scalar_prefetch=n_prefetch,
        grid=(x_part.shape[0] // bm,),
        in_specs=in_specs,
        out_specs=pl.BlockSpec((bm, ROW_WORDS), lambda i, be, nb: (i + off, 0)),
        scratch_shapes=[pltpu.VMEM((D_MODEL, 2 * D_MODEL), BF16), pltpu.VMEM((D_MODEL, D_MODEL), BF16)],
    )
    return pl.pallas_call(
        functools.partial(_moe_kernel, off),
        grid_spec=grid_spec,
        out_shape=jax.ShapeDtypeStruct((block_expert.shape[0] * bm, ROW_WORDS), I32),
        input_output_aliases=aliases,
        compiler_params=pltpu.CompilerParams(dimension_semantics=("arbitrary",),
                                             vmem_limit_bytes=VMEM_LIMIT),
        name="moe",
    )(*args)


def _final_kernel(x1_ref, yg_ref, gt_ref, p_ref, gpost_ref, wg_ref, bg_ref, wp_ref, gple_ref, *rest):
    o_ref = rest[-1]
    gt = gt_ref[...]
    lo, hi = _unpack_halves(yg_ref[0])
    y = gt[:, 0:1] * jnp.concatenate([lo, hi], axis=1)
    for k in range(1, TOP_K):
        lo, hi = _unpack_halves(yg_ref[k])
        y = y + gt[:, k:k + 1] * jnp.concatenate([lo, hi], axis=1)
    x2 = x1_ref[...] + _rms(y, gpost_ref[...])
    gate = jax.nn.sigmoid(_dot(x2.astype(BF16), wg_ref[...]) + bg_ref[...])
    pp = _dot(p_ref[...].astype(BF16), wp_ref[...])
    o_ref[...] = x2 + _rms(gate * pp, gple_ref[...])


def _final(x1, yg_part, tile_offset, out_prev, gate_t, p, gpost, wg, bg, wp, gple):
    n = x1.shape[0]
    off = tile_offset
    tok = lambda w: pl.BlockSpec((TM, w), lambda i: (i + off, 0))
    full = lambda a: pl.BlockSpec(a.shape, lambda i: (0,) * a.ndim)
    consts = (gpost, wg, bg, wp, gple)
    in_specs = [tok(D_MODEL), pl.BlockSpec((TOP_K, TM, ROW_WORDS), lambda i: (0, i, 0)), tok(TOP_K),
                tok(D_PLE)] + [full(a) for a in consts]
    args = [x1, yg_part, gate_t, p, *consts]
    aliases = {}
    if out_prev is not None:
        in_specs.append(pl.BlockSpec(memory_space=pl.ANY))
        aliases = {len(args): 0}
        args.append(out_prev)
    return pl.pallas_call(
        _final_kernel,
        grid=(yg_part.shape[1] // TM,),
        in_specs=in_specs,
        out_specs=tok(D_MODEL),
        out_shape=jax.ShapeDtypeStruct((n, D_MODEL), F32),
        input_output_aliases=aliases,
        compiler_params=pltpu.CompilerParams(dimension_semantics=("arbitrary",),
                                             vmem_limit_bytes=VMEM_LIMIT),
        name="final",
    )(*args)


def _rope_tables(positions):
    inv_freq = 1.0 / (ROPE_THETA ** (jnp.arange(0, QK_ROPE, 2, dtype=F32) / QK_ROPE))
    freq = jnp.concatenate([jnp.zeros((QK_NOPE,), F32), inv_freq, inv_freq,
                            jnp.zeros((HEAD_PAD - QK_NOPE - QK_ROPE,), F32)])
    ang = positions.astype(F32).reshape(-1)[:, None] * freq
    return jnp.cos(ang), jnp.sin(ang)


def _rot_half_cols(w):
    half = w.shape[-1] // 2
    return jnp.concatenate([-w[..., half:], w[..., :half]], axis=-1)


def _pad_head(w):
    return jnp.pad(w, [(0, 0)] * (w.ndim - 1) + [(0, HEAD_PAD - w.shape[-1])])


def _layer(x, p_l, cos, sin, prm):
    (attn_pre_g, w_in, q_norm_g, w_uq, kv_norm_g, w_ukv, sg_norm_g, w_spatial, b_spatial, mla_out_g,
     sg_out_g, w_out, attn_post_g, ffn_pre_g, w_router, b_router, w_gate_up, b_gate_up, w_down, b_down,
     ffn_post_g, w_ple_gate, b_ple_gate, w_ple_proj, ple_norm_g) = prm
    b, s, _ = x.shape
    n = b * s
    xt = x.reshape(n, D_MODEL)
    row2 = lambda a: a.reshape(1, -1)

    w_kr = w_in[:, Q_LORA + KV_LORA:Q_LORA + KV_LORA + QK_ROPE]
    place = lambda w: jnp.pad(w, ((0, 0), (QK_NOPE, HEAD_PAD - QK_NOPE - QK_ROPE)))
    win_aug = jnp.concatenate(
        [w_in[:, :Q_LORA + KV_LORA], place(w_kr), place(_rot_half_cols(w_kr)),
         w_in[:, Q_LORA + KV_LORA + QK_ROPE:]], axis=-1).astype(BF16)
    w_q_rot = jnp.concatenate([jnp.zeros_like(w_uq[..., :QK_NOPE]), _rot_half_cols(w_uq[..., QK_NOPE:])],
                              axis=-1)
    wq_aug = jnp.concatenate([_pad_head(w_uq).reshape(Q_LORA, -1), _pad_head(w_q_rot).reshape(Q_LORA, -1)],
                             axis=-1).astype(BF16)
    wk_pad = _pad_head(w_ukv[..., :QK_NOPE]).reshape(KV_LORA, -1).astype(BF16)
    wv_t = w_ukv[..., QK_NOPE:].reshape(KV_LORA, -1).T.astype(BF16)
    head_of = jnp.arange(SG_WIDTH) // SG_HEAD_DIM
    gsum = (head_of[:, None] == head_of[None, :]).astype(BF16)
    wcat = w_spatial.transpose(1, 0, 2).reshape(SG_CHUNK, SG_HEADS * SG_CHUNK)
    bsp = jnp.repeat(b_spatial.T, SG_HEAD_DIM, axis=1)

    q, k, vt, mix_sg = _inproj(xt, cos, sin, row2(attn_pre_g), win_aug, row2(q_norm_g), wq_aug,
                               row2(kv_norm_g), wk_pad, wv_t, row2(sg_norm_g), gsum, wcat, bsp,
                               row2(sg_out_g))
    mix_mla = _attention(q, k, vt, row2(mla_out_g), b, s)

    w_out_b = w_out.astype(BF16)
    x1, h2, idx, gate, rank, cnt = _postattn(
        xt, mix_mla, mix_sg, w_out_b[:MLA_WIDTH], w_out_b[MLA_WIDTH:], row2(attn_post_g), row2(ffn_pre_g),
        w_router.T.astype(BF16), b_router.reshape(N_EXPERTS, 1))

    a = n * TOP_K
    counts = cnt[:, 0].astype(I32)
    padded = (counts + MOE_BM - 1) // MOE_BM * MOE_BM
    pad_end = jnp.cumsum(padded)
    pad_start = pad_end - padded
    start_of = jnp.sum(jnp.where(idx[..., None] == jnp.arange(N_EXPERTS, dtype=I32), pad_start, 0), axis=-1)
    dest = (start_of + rank).reshape(a)
    n_blocks = a // MOE_BM + N_EXPERTS
    block_start = jnp.arange(n_blocks, dtype=I32) * MOE_BM
    n_used = (pad_end[-1] // MOE_BM).astype(I32)
    block_expert = jnp.sum(pad_end[None, :] <= jnp.minimum(block_start, pad_end[-1] - MOE_BM)[:, None],
                           axis=-1).astype(I32)

    half_blocks = n_blocks // 2
    half_rows = half_blocks * MOE_BM
    y_rows = None
    for part in range(2):
        x_part = _sc_dispatch(h2, dest, half_rows, part * half_rows)
        y_rows = _moe(block_expert, n_used.reshape(1), x_part, part * half_blocks, y_rows,
                      w_gate_up, b_gate_up[:, None, :], w_down, b_down[:, None, :])

    half_tok = n // 2
    dest_kt = dest.reshape(TOP_K, n)
    gate_t = gate.T
    p_tok = p_l.reshape(n, D_PLE)
    out = None
    for part in range(2):
        rows = dest_kt[:, part * half_tok:(part + 1) * half_tok].reshape(TOP_K * half_tok)
        yg = _sc_gather(y_rows, rows).reshape(TOP_K, half_tok, ROW_WORDS)
        out = _final(x1, yg, part * (half_tok // TM), out, gate_t, p_tok, row2(ffn_post_g),
                     w_ple_gate.astype(BF16), row2(b_ple_gate), w_ple_proj.astype(BF16), row2(ple_norm_g))
    return out.reshape(b, s, D_MODEL)


def kernel(x, p, positions, attn_pre_g, w_in, q_norm_g, w_uq, kv_norm_g, w_ukv, sg_norm_g, w_spatial, b_spatial, mla_out_g, sg_out_g, w_out, attn_post_g, ffn_pre_g, w_router, b_router, w_gate_up, b_gate_up, w_down, b_down, ffn_post_g, w_ple_gate, b_ple_gate, w_ple_proj, ple_norm_g):
    cos, sin = _rope_tables(positions)
    params = (attn_pre_g, w_in, q_norm_g, w_uq, kv_norm_g, w_ukv, sg_norm_g, w_spatial, b_spatial, mla_out_g,
              sg_out_g, w_out, attn_post_g, ffn_pre_g, w_router, b_router, w_gate_up, b_gate_up, w_down, b_down,
              ffn_post_g, w_ple_gate, b_ple_gate, w_ple_proj, ple_norm_g)
    for layer in range(p.shape[0]):
        x = _layer(x, p[layer], cos, sin, tuple(a[layer] for a in params))
    return x
```

```python
import functools

import jax
import jax.numpy as jnp
from jax import lax
from jax.experimental import pallas as pl
from jax.experimental.pallas import tpu as pltpu
from jax.experimental.pallas import tpu_sc as plsc

F32 = jnp.float32
BF16 = jnp.bfloat16
I32 = jnp.int32

D_MODEL = 1024
HEADS = 8
QK_NOPE = 64
QK_ROPE = 32
V_HEAD = 64
Q_LORA = 256
KV_LORA = 128
ROPE_THETA = 10000.0
SG_HEADS = 8
SG_HEAD_DIM = 64
SG_CHUNK = 128
SG_WIDTH = SG_HEADS * SG_HEAD_DIM
MLA_WIDTH = HEADS * V_HEAD
N_EXPERTS = 32
TOP_K = 4
SWIGLU_LIMIT = 7.0
SWIGLU_ALPHA = 1.702
D_PLE = 256
EPS = 1e-6

LANES = 128
HEAD_PAD = LANES
QK_SCALE = (QK_NOPE + QK_ROPE) ** -0.5
LOG2E = 1.4426950408889634

COL_CQ = 0
COL_CKV = COL_CQ + Q_LORA
COL_KR = COL_CKV + KV_LORA
COL_KRS = COL_KR + LANES
COL_SG = COL_KRS + LANES
IN_COLS_AUG = COL_SG + 2 * SG_WIDTH

TM = 512
BQ = 512
ATTN_BK = 1024
ATTN_ONES_ROWS = 16
MOE_BM = 256
MOE_W_CHUNKS = 8
VMEM_LIMIT = 56 * 1024 * 1024

NT_DIMS = (((1,), (1,)), ((), ()))
ROW_WORDS = D_MODEL // 2
SC_CHUNK = 64
SC_DEST_CHUNK = 8192


def _rms(x, g):
    return x * lax.rsqrt(jnp.mean(x * x, axis=-1, keepdims=True) + EPS) * g


def _dot(a, b):
    return jnp.dot(a, b, preferred_element_type=F32)


def _dot_nt(a, b):
    return lax.dot_general(a, b, NT_DIMS, preferred_element_type=F32)


def _pack_halves(x):
    half = x.shape[1] // 2
    return pltpu.pack_elementwise([x[:, :half], x[:, half:]], packed_dtype=BF16)


def _unpack_halves(w):
    return (pltpu.unpack_elementwise(w, index=0, packed_dtype=BF16, unpacked_dtype=F32),
            pltpu.unpack_elementwise(w, index=1, packed_dtype=BF16, unpacked_dtype=F32))


def _inproj_kernel(x_ref, cos_ref, sin_ref, gpre_ref, win_ref, qg_ref, wq_ref, kvg_ref, wk_ref, wvt_ref,
                   sgg_ref, gsum_ref, wcat_ref, bsp_ref, sgo_ref,
                   q_out, k_out, vt_out, sg_out):
    x = x_ref[...]
    h = _rms(x, gpre_ref[...])
    z = _dot(h.astype(BF16), win_ref[...])
    cos = cos_ref[...]
    sin = sin_ref[...]

    cqn = _rms(z[:, COL_CQ:COL_CQ + Q_LORA], qg_ref[...])
    qq = _dot(cqn.astype(BF16), wq_ref[...])
    half = HEADS * HEAD_PAD
    for hd in range(HEADS):
        sl = slice(hd * HEAD_PAD, (hd + 1) * HEAD_PAD)
        sl2 = slice(half + hd * HEAD_PAD, half + (hd + 1) * HEAD_PAD)
        q_out[:, sl] = ((qq[:, sl] * cos + qq[:, sl2] * sin) * (QK_SCALE * LOG2E)).astype(BF16)

    ckvn = _rms(z[:, COL_CKV:COL_CKV + KV_LORA], kvg_ref[...]).astype(BF16)
    kk = _dot(ckvn, wk_ref[...])
    kr = z[:, COL_KR:COL_KR + LANES] * cos + z[:, COL_KRS:COL_KRS + LANES] * sin
    for hd in range(HEADS):
        sl = slice(hd * HEAD_PAD, (hd + 1) * HEAD_PAD)
        k_out[:, sl] = (kk[:, sl] + kr).astype(BF16)
    vt_out[...] = _dot_nt(wvt_ref[...], ckvn).astype(BF16)

    zg = jax.nn.gelu(z[:, COL_SG:COL_SG + 2 * SG_WIDTH])
    u = zg[:, :SG_WIDTH]
    v = zg[:, SG_WIDTH:]
    v2 = v * v
    v2_hi = v2.astype(BF16)
    v2_lo = (v2 - v2_hi.astype(F32)).astype(BF16)
    gsum = gsum_ref[...]
    ms = (_dot(v2_hi, gsum) + _dot(v2_lo, gsum)) * (1.0 / SG_HEAD_DIM)
    vn = v * lax.rsqrt(ms + EPS) * sgg_ref[...]

    row = lax.broadcasted_iota(I32, (SG_CHUNK, SG_HEADS * SG_CHUNK), 0)
    col = lax.broadcasted_iota(I32, (SG_CHUNK, SG_HEADS * SG_CHUNK), 1)
    wcat = jnp.where((col % SG_CHUNK) <= row, wcat_ref[...], 0.0).astype(BF16)
    lane_head = lax.broadcasted_iota(I32, (SG_CHUNK, SG_WIDTH), 1) // SG_HEAD_DIM
    bsp = bsp_ref[...]
    sgo = sgo_ref[...]
    for c in range(x.shape[0] // SG_CHUNK):
        rows = slice(c * SG_CHUNK, (c + 1) * SG_CHUNK)
        vc = vn[rows]
        vbd = jnp.concatenate(
            [jnp.where(lane_head == hd, vc, 0.0).astype(BF16) for hd in range(SG_HEADS)], axis=0)
        vm = _dot(wcat, vbd) + bsp
        sg_out[rows, :] = _rms(u[rows] * vm, sgo).astype(BF16)


def _inproj(x, cos, sin, gpre, win, qg, wq, kvg, wk, wvt, sgg, gsum, wcat, bsp, sgo):
    n = x.shape[0]
    tok = lambda w: pl.BlockSpec((TM, w), lambda i: (i, 0))
    full = lambda a: pl.BlockSpec(a.shape, lambda i: (0,) * a.ndim)
    consts = (gpre, win, qg, wq, kvg, wk, wvt, sgg, gsum, wcat, bsp, sgo)
    return pl.pallas_call(
        _inproj_kernel,
        grid=(n // TM,),
        in_specs=[tok(D_MODEL), tok(LANES), tok(LANES)] + [full(a) for a in consts],
        out_specs=[tok(HEADS * HEAD_PAD), tok(HEADS * HEAD_PAD),
                   pl.BlockSpec((MLA_WIDTH, TM), lambda i: (0, i)), tok(SG_WIDTH)],
        out_shape=[jax.ShapeDtypeStruct((n, HEADS * HEAD_PAD), BF16),
                   jax.ShapeDtypeStruct((n, HEADS * HEAD_PAD), BF16),
                   jax.ShapeDtypeStruct((MLA_WIDTH, n), BF16),
                   jax.ShapeDtypeStruct((n, SG_WIDTH), BF16)],
        compiler_params=pltpu.CompilerParams(dimension_semantics=("arbitrary",),
                                             vmem_limit_bytes=VMEM_LIMIT),
        name="inproj",
    )(x, cos, sin, *consts)


def _attn_kernel(q_ref, k_ref, vt_ref, g_ref, o_ref, m_sc, l_sc, acc_sc):
    i = pl.program_id(1)
    bq = q_ref.shape[0]
    m_sc[...] = jnp.full(m_sc.shape, -jnp.inf, F32)
    l_sc[...] = jnp.zeros(l_sc.shape, F32)
    acc_sc[...] = jnp.zeros(acc_sc.shape, F32)

    def qk(start, nk, hd):
        hs = slice(hd * HEAD_PAD, (hd + 1) * HEAD_PAD)
        return _dot_nt(k_ref[pl.ds(start, nk), hs], q_ref[:, hs])

    def block(start, nk, masked):
        ones = jnp.ones((ATTN_ONES_ROWS, nk), BF16)
        queue = [qk(start, nk, 0), qk(start, nk, 1)]
        for hd in range(HEADS):
            vs = slice(hd * V_HEAD, (hd + 1) * V_HEAD)
            st = queue.pop(0)
            if hd + 2 < HEADS:
                queue.append(qk(start, nk, hd + 2))
            if masked:
                causal = (lax.broadcasted_iota(I32, (nk, bq), 0) <= lax.broadcasted_iota(I32, (nk, bq), 1))
                st = jnp.where(causal, st, -jnp.inf)
            m_prev = m_sc[hd:hd + 1, :]
            m_new = jnp.maximum(m_prev, jnp.max(st, axis=0, keepdims=True))
            alpha = jnp.exp2(m_prev - m_new)
            p = jnp.exp2(st - m_new).astype(BF16)
            vta = jnp.concatenate([vt_ref[vs, pl.ds(start, nk)], ones], axis=0)
            pv = _dot(vta, p)
            l_sc[hd:hd + 1, :] = alpha * l_sc[hd:hd + 1, :] + pv[V_HEAD:V_HEAD + 1, :]
            acc_sc[vs, :] = alpha * acc_sc[vs, :] + pv[:V_HEAD, :]
            m_sc[hd:hd + 1, :] = m_new

    def body(kb, carry):
        block(pl.multiple_of(kb * ATTN_BK, ATTN_BK), ATTN_BK, False)
        return carry

    visible = i * bq
    lax.fori_loop(0, visible // ATTN_BK, body, 0)
    for r in range(ATTN_BK // bq - 1, 0, -1):

        @pl.when(visible % ATTN_BK >= r * bq)
        def _():
            block(pl.multiple_of((i - r) * bq, bq), bq, False)

    block(pl.multiple_of(i * bq, bq), bq, True)
    ot = jnp.concatenate(
        [acc_sc[hd * V_HEAD:(hd + 1) * V_HEAD, :] * (1.0 / l_sc[hd:hd + 1, :]) for hd in range(HEADS)], axis=0)
    o_ref[...] = _rms(ot.T, g_ref[...]).astype(BF16)


def _attention(q, k, vt, g, b, s):
    nq = s // BQ
    return pl.pallas_call(
        _attn_kernel,
        grid=(b, nq),
        in_specs=[pl.BlockSpec((BQ, HEADS * HEAD_PAD), lambda bi, i: (bi * nq + i, 0)),
                  pl.BlockSpec((s, HEADS * HEAD_PAD), lambda bi, i: (bi, 0), pipeline_mode=pl.Buffered(1)),
                  pl.BlockSpec((MLA_WIDTH, s), lambda bi, i: (0, bi), pipeline_mode=pl.Buffered(1)),
                  pl.BlockSpec(g.shape, lambda bi, i: (0, 0))],
        out_specs=pl.BlockSpec((BQ, MLA_WIDTH), lambda bi, i: (bi * nq + i, 0)),
        out_shape=jax.ShapeDtypeStruct((b * s, MLA_WIDTH), BF16),
        scratch_shapes=[pltpu.VMEM((HEADS, BQ), F32), pltpu.VMEM((HEADS, BQ), F32),
                        pltpu.VMEM((MLA_WIDTH, BQ), F32)],
        compiler_params=pltpu.CompilerParams(dimension_semantics=("arbitrary", "arbitrary"),
                                             vmem_limit_bytes=VMEM_LIMIT),
        name="attention",
    )(q, k, vt, g)


def _postattn_kernel(x_ref, mla_ref, sg_ref, wo1_ref, wo2_ref, gpost_ref, gffn_ref, wrt_ref, brt_ref,
                     x1_out, h2_out, idx_out, gate_out, rank_out, cnt_out, cnt_sc):
    i = pl.program_id(0)

    @pl.when(i == 0)
    def _():
        cnt_sc[...] = jnp.zeros(cnt_sc.shape, F32)

    a = _dot(mla_ref[...], wo1_ref[...]) + _dot(sg_ref[...], wo2_ref[...])
    x1 = x_ref[...] + _rms(a, gpost_ref[...])
    x1_out[...] = x1
    h2f = _rms(x1, gffn_ref[...])
    h2_out[...] = _pack_halves(h2f)
    h2 = h2f.astype(BF16)

    tm = h2.shape[0]
    logits = _dot_nt(wrt_ref[...], h2) + brt_ref[...]
    eidx = lax.broadcasted_iota(I32, (N_EXPERTS, tm), 0)
    vals, idxs, sels = [], [], []
    for _ in range(TOP_K):
        mx = jnp.max(logits, axis=0, keepdims=True)
        ik = jnp.min(jnp.where(logits == mx, eidx, N_EXPERTS), axis=0, keepdims=True)
        sel = eidx == ik
        vals.append(mx)
        idxs.append(ik)
        sels.append(sel)
        logits = jnp.where(sel, -jnp.inf, logits)
    ex = [jnp.exp(v - vals[0]) for v in vals]
    den = ex[0] + ex[1] + ex[2] + ex[3]
    gate_out[...] = jnp.concatenate([e / den for e in ex], axis=0)
    idx_out[...] = jnp.concatenate(idxs, axis=0)

    maskf = sum(jnp.where(s, 1.0, 0.0) for s in sels)
    before = (lax.broadcasted_iota(I32, (tm, tm), 0) < lax.broadcasted_iota(I32, (tm, tm), 1))
    prefix = _dot(maskf.astype(BF16), jnp.where(before, 1.0, 0.0).astype(BF16))
    base = cnt_sc[...]
    tot = base + prefix
    ranks = [jnp.sum(jnp.where(s, tot, 0.0), axis=0, keepdims=True) for s in sels]
    rank_out[...] = jnp.concatenate(ranks, axis=0).astype(I32)
    cnt = base + jnp.sum(maskf, axis=1, keepdims=True)
    cnt_sc[...] = cnt
    cnt_out[...] = jnp.broadcast_to(cnt, cnt_out.shape)


def _postattn(x, mla, sg, wo1, wo2, gpost, gffn, wrt, brt):
    n = x.shape[0]
    tok = lambda w: pl.BlockSpec((TM, w), lambda i: (i, 0))
    tokt = pl.BlockSpec((TOP_K, TM), lambda i: (0, i))
    full = lambda a: pl.BlockSpec(a.shape, lambda i: (0,) * a.ndim)
    consts = (wo1, wo2, gpost, gffn, wrt, brt)
    return pl.pallas_call(
        _postattn_kernel,
        grid=(n // TM,),
        in_specs=[tok(D_MODEL), tok(MLA_WIDTH), tok(SG_WIDTH)] + [full(a) for a in consts],
        out_specs=[tok(D_MODEL), tok(ROW_WORDS), tokt, tokt, tokt,
                   pl.BlockSpec((N_EXPERTS, LANES), lambda i: (0, 0))],
        out_shape=[jax.ShapeDtypeStruct((n, D_MODEL), F32),
                   jax.ShapeDtypeStruct((n, ROW_WORDS), I32),
                   jax.ShapeDtypeStruct((TOP_K, n), I32),
                   jax.ShapeDtypeStruct((TOP_K, n), F32),
                   jax.ShapeDtypeStruct((TOP_K, n), I32),
                   jax.ShapeDtypeStruct((N_EXPERTS, LANES), F32)],
        scratch_shapes=[pltpu.VMEM((N_EXPERTS, 1), F32)],
        compiler_params=pltpu.CompilerParams(dimension_semantics=("arbitrary",),
                                             vmem_limit_bytes=VMEM_LIMIT),
        name="postattn",
    )(x, mla, sg, *consts)


def _sc_workers():
    info = plsc.get_sparse_core_info()
    return info.num_cores, info.num_cores * info.num_subcores, info.num_lanes


def _sc_stream_rows(table_hbm, idx_v, out_hbm, base, n_chunks, buf, sem_g, sem_w):
    def gather(j, b):
        rows = idx_v.at[pl.ds(pl.multiple_of(j * SC_CHUNK, SC_CHUNK), SC_CHUNK)]
        return pltpu.make_async_copy(table_hbm.at[rows], buf.at[b], sem_g.at[b])

    def write(j, b):
        rows = pl.ds(pl.multiple_of(base + j * SC_CHUNK, SC_CHUNK), SC_CHUNK)
        return pltpu.make_async_copy(buf.at[b], out_hbm.at[rows], sem_w.at[b])

    gather(0, 0).start()

    @pl.loop(0, n_chunks, step=2)
    def _(j0):
        for b in range(2):
            j = j0 + b
            gather(j, b).wait()

            @pl.when(j >= 1)
            def _():
                write(j - 1, 1 - b).wait()

            @pl.when(j + 1 < n_chunks)
            def _():
                gather(j + 1, 1 - b).start()

            write(j, b).start()

    write(n_chunks - 1, 1).wait()


def _sc_gather(table, idx):
    n_out = idx.shape[0]
    width = table.shape[1]
    num_cores, workers, _ = _sc_workers()
    per_w = n_out // workers
    n_chunks = per_w // SC_CHUNK
    assert per_w * workers == n_out and n_chunks * SC_CHUNK == per_w and n_chunks % 2 == 0

    @functools.partial(
        pl.kernel, mesh=plsc.VectorSubcoreMesh(core_axis_name="c", subcore_axis_name="s"),
        out_type=jax.ShapeDtypeStruct((n_out, width), table.dtype),
        scratch_types=[pltpu.VMEM((per_w,), I32), pltpu.VMEM((2, SC_CHUNK, width), table.dtype),
                       pltpu.SemaphoreType.DMA((2,)), pltpu.SemaphoreType.DMA((2,))])
    def gather_kernel(table_hbm, idx_hbm, out_hbm, idx_v, buf, sem_g, sem_w):
        wid = lax.axis_index("s") * num_cores + lax.axis_index("c")
        base = pl.multiple_of(wid * per_w, SC_CHUNK)
        pltpu.sync_copy(idx_hbm.at[pl.ds(base, per_w)], idx_v)
        _sc_stream_rows(table_hbm, idx_v, out_hbm, base, n_chunks, buf, sem_g, sem_w)

    return gather_kernel(table, idx)


def _sc_dispatch(table, dest, n_out):
    n_assign = dest.shape[0]
    n, width = table.shape
    num_cores, workers, lanes = _sc_workers()
    per_w = n_out // workers
    n_chunks = per_w // SC_CHUNK
    assert per_w * workers == n_out and n_chunks * SC_CHUNK == per_w and n_chunks % 2 == 0
    assert n_assign % SC_DEST_CHUNK == 0 and SC_DEST_CHUNK % lanes == 0 and per_w % lanes == 0

    @functools.partial(
        pl.kernel, mesh=plsc.VectorSubcoreMesh(core_axis_name="c", subcore_axis_name="s"),
        out_type=jax.ShapeDtypeStruct((n_out, width), table.dtype),
        scratch_types=[pltpu.VMEM((per_w,), I32), pltpu.VMEM((SC_DEST_CHUNK,), I32),
                       pltpu.VMEM((2, SC_CHUNK, width), table.dtype),
                       pltpu.SemaphoreType.DMA((2,)), pltpu.SemaphoreType.DMA((2,))],
        compiler_params=pltpu.CompilerParams(needs_layout_passes=False))
    def dispatch_kernel(table_hbm, dest_hbm, out_hbm, tok_v, dest_v, buf, sem_g, sem_w):
        wid = lax.axis_index("s") * num_cores + lax.axis_index("c")
        base = pl.multiple_of(wid * per_w, SC_CHUNK)
        lane = lax.iota(I32, lanes)

        @pl.loop(0, per_w, step=lanes)
        def _(r):
            tok_v[pl.ds(r, lanes)] = lax.rem(base + r + lane, n)

        @pl.loop(0, n_assign // SC_DEST_CHUNK)
        def _(c):
            first = pl.multiple_of(c * SC_DEST_CHUNK, SC_DEST_CHUNK)
            pltpu.sync_copy(dest_hbm.at[pl.ds(first, SC_DEST_CHUNK)], dest_v)

            @pl.loop(0, SC_DEST_CHUNK, step=lanes)
            def _(i):
                local = dest_v[pl.ds(i, lanes)] - base
                mine = (local >= 0) & (local < per_w)
                tok = lax.rem(first + i + lane, n)
                plsc.store_scatter(tok_v, [jnp.where(mine, local, 0)], tok, mask=mine)

        _sc_stream_rows(table_hbm, tok_v, out_hbm, base, n_chunks, buf, sem_g, sem_w)

    return dispatch_kernel(table, dest)


def _moe_kernel(nb_ref, row0_ref, x_hbm, wgu_hbm, bgu_ref, wd_hbm, bd_ref, y_hbm,
                wgu_f, wd_f, wgu_sc, wd_sc, xbuf, ybuf, sem_x, sem_y, sem_w):
    e = pl.program_id(0)
    n_e = pl.num_programs(0)
    nb = nb_ref[e]
    row0 = row0_ref[e]
    wslot = e % 2
    chunk_rows = D_MODEL // MOE_W_CHUNKS

    def w_copies(expert, c, slot):
        r = pl.ds(c * chunk_rows, chunk_rows)
        return (pltpu.make_async_copy(wgu_hbm.at[expert, r, :], wgu_f.at[slot, r, :], sem_w.at[slot]),
                pltpu.make_async_copy(wd_hbm.at[expert, r, :], wd_f.at[slot, r, :], sem_w.at[slot]))

    def start_chunk(expert, c, slot):
        for cp in w_copies(expert, c, slot):
            cp.start()

    def rows(j):
        return pl.ds(pl.multiple_of(row0 + j * MOE_BM, MOE_BM), MOE_BM)

    def x_copy(j, slot):
        return pltpu.make_async_copy(x_hbm.at[rows(j)], xbuf.at[slot], sem_x.at[slot])

    def y_copy(j, slot):
        return pltpu.make_async_copy(ybuf.at[slot], y_hbm.at[rows(j)], sem_y.at[slot])

    @pl.when(e == 0)
    def _():
        for c in range(MOE_W_CHUNKS):
            start_chunk(0, c, 0)

    @pl.when(nb > 0)
    def _():
        x_copy(0, 0).start()

    for c in range(MOE_W_CHUNKS):
        for cp in w_copies(e, c, wslot):
            cp.wait()
    wgu_sc[...] = wgu_f[wslot].astype(BF16)
    wd_sc[...] = wd_f[wslot].astype(BF16)
    nxt = jnp.minimum(e + 1, n_e - 1)
    has_next = e + 1 < n_e

    def body(j, carry):
        slot = j % 2

        @pl.when(has_next & (j < MOE_W_CHUNKS))
        def _():
            start_chunk(nxt, j, 1 - wslot)

        x_copy(j, slot).wait()

        @pl.when(j + 1 < nb)
        def _():
            x_copy(j + 1, 1 - slot).start()

        @pl.when(j >= 2)
        def _():
            y_copy(j - 2, slot).wait()

        x = jnp.concatenate(_unpack_halves(xbuf[slot]), axis=1).astype(BF16)
        gu = _dot(x, wgu_sc[...]) + bgu_ref[0]
        g = jnp.minimum(gu[:, :D_MODEL], SWIGLU_LIMIT)
        u = jnp.clip(gu[:, D_MODEL:], -SWIGLU_LIMIT, SWIGLU_LIMIT)
        act = (u + 1.0) * (g * jax.nn.sigmoid(SWIGLU_ALPHA * g))
        ybuf[slot] = _pack_halves(_dot(act.astype(BF16), wd_sc[...]) + bd_ref[0])
        y_copy(j, slot).start()
        return carry

    lax.fori_loop(0, nb, body, 0)

    def rest(c, carry):
        @pl.when(has_next)
        def _():
            start_chunk(nxt, c, 1 - wslot)
        return carry

    lax.fori_loop(jnp.minimum(nb, MOE_W_CHUNKS), MOE_W_CHUNKS, rest, 0)

    @pl.when(nb >= 2)
    def _():
        y_copy(nb - 2, nb % 2).wait()

    @pl.when(nb >= 1)
    def _():
        y_copy(nb - 1, (nb - 1) % 2).wait()


def _moe(blocks_of, row0_of, x_rows, wgu, bgu, wd, bd):
    hbm = pl.BlockSpec(memory_space=pl.ANY)
    grid_spec = pltpu.PrefetchScalarGridSpec(
        num_scalar_prefetch=2,
        grid=(N_EXPERTS,),
        in_specs=[hbm, hbm,
                  pl.BlockSpec((1, 1, 2 * D_MODEL), lambda e, nb, r0: (e, 0, 0)),
                  hbm,
                  pl.BlockSpec((1, 1, D_MODEL), lambda e, nb, r0: (e, 0, 0))],
        out_specs=hbm,
        scratch_shapes=[pltpu.VMEM((2, D_MODEL, 2 * D_MODEL), F32), pltpu.VMEM((2, D_MODEL, D_MODEL), F32),
                        pltpu.VMEM((D_MODEL, 2 * D_MODEL), BF16), pltpu.VMEM((D_MODEL, D_MODEL), BF16),
                        pltpu.VMEM((2, MOE_BM, ROW_WORDS), I32), pltpu.VMEM((2, MOE_BM, ROW_WORDS), I32),
                        pltpu.SemaphoreType.DMA((2,)), pltpu.SemaphoreType.DMA((2,)),
                        pltpu.SemaphoreType.DMA((2,))],
    )
    return pl.pallas_call(
        _moe_kernel,
        grid_spec=grid_spec,
        out_shape=jax.ShapeDtypeStruct(x_rows.shape, I32),
        compiler_params=pltpu.CompilerParams(dimension_semantics=("arbitrary",),
                                             vmem_limit_bytes=VMEM_LIMIT),
        name="moe",
    )(blocks_of, row0_of, x_rows, wgu, bgu, wd, bd)


def _final_kernel(x1_ref, yg_ref, gt_ref, p_ref, gpost_ref, wg_ref, bg_ref, wp_ref, gple_ref, o_ref):
    gt = gt_ref[...]
    lo, hi = _unpack_halves(yg_ref[0])
    y = gt[:, 0:1] * jnp.concatenate([lo, hi], axis=1)
    for k in range(1, TOP_K):
        lo, hi = _unpack_halves(yg_ref[k])
        y = y + gt[:, k:k + 1] * jnp.concatenate([lo, hi], axis=1)
    x2 = x1_ref[...] + _rms(y, gpost_ref[...])
    gate = jax.nn.sigmoid(_dot(x2.astype(BF16), wg_ref[...]) + bg_ref[...])
    pp = _dot(p_ref[...].astype(BF16), wp_ref[...])
    o_ref[...] = x2 + _rms(gate * pp, gple_ref[...])


def _final(x1, yg, gate_t, p, gpost, wg, bg, wp, gple):
    n = x1.shape[0]
    tok = lambda w: pl.BlockSpec((TM, w), lambda i: (i, 0))
    full = lambda a: pl.BlockSpec(a.shape, lambda i: (0,) * a.ndim)
    consts = (gpost, wg, bg, wp, gple)
    return pl.pallas_call(
        _final_kernel,
        grid=(n // TM,),
        in_specs=[tok(D_MODEL), pl.BlockSpec((TOP_K, TM, ROW_WORDS), lambda i: (0, i, 0)), tok(TOP_K),
                  tok(D_PLE)] + [full(a) for a in consts],
        out_specs=tok(D_MODEL),
        out_shape=jax.ShapeDtypeStruct((n, D_MODEL), F32),
        compiler_params=pltpu.CompilerParams(dimension_semantics=("arbitrary",),
                                             vmem_limit_bytes=VMEM_LIMIT),
        name="final",
    )(x1, yg, gate_t, p, *consts)


def _rope_tables(positions):
    inv_freq = 1.0 / (ROPE_THETA ** (jnp.arange(0, QK_ROPE, 2, dtype=F32) / QK_ROPE))
    freq = jnp.concatenate([jnp.zeros((QK_NOPE,), F32), inv_freq, inv_freq,
                            jnp.zeros((HEAD_PAD - QK_NOPE - QK_ROPE,), F32)])
    ang = positions.astype(F32).reshape(-1)[:, None] * freq
    return jnp.cos(ang), jnp.sin(ang)


def _rot_half_cols(w):
    half = w.shape[-1] // 2
    return jnp.concatenate([-w[..., half:], w[..., :half]], axis=-1)


def _pad_head(w):
    return jnp.pad(w, [(0, 0)] * (w.ndim - 1) + [(0, HEAD_PAD - w.shape[-1])])


def _layer(x, p_l, cos, sin, prm):
    (attn_pre_g, w_in, q_norm_g, w_uq, kv_norm_g, w_ukv, sg_norm_g, w_spatial, b_spatial, mla_out_g,
     sg_out_g, w_out, attn_post_g, ffn_pre_g, w_router, b_router, w_gate_up, b_gate_up, w_down, b_down,
     ffn_post_g, w_ple_gate, b_ple_gate, w_ple_proj, ple_norm_g) = prm
    b, s, _ = x.shape
    n = b * s
    xt = x.reshape(n, D_MODEL)
    row2 = lambda a: a.reshape(1, -1)

    w_kr = w_in[:, Q_LORA + KV_LORA:Q_LORA + KV_LORA + QK_ROPE]
    place = lambda w: jnp.pad(w, ((0, 0), (QK_NOPE, HEAD_PAD - QK_NOPE - QK_ROPE)))
    win_aug = jnp.concatenate(
        [w_in[:, :Q_LORA + KV_LORA], place(w_kr), place(_rot_half_cols(w_kr)),
         w_in[:, Q_LORA + KV_LORA + QK_ROPE:]], axis=-1).astype(BF16)
    w_q_rot = jnp.concatenate([jnp.zeros_like(w_uq[..., :QK_NOPE]), _rot_half_cols(w_uq[..., QK_NOPE:])],
                              axis=-1)
    wq_aug = jnp.concatenate([_pad_head(w_uq).reshape(Q_LORA, -1), _pad_head(w_q_rot).reshape(Q_LORA, -1)],
                             axis=-1).astype(BF16)
    wk_pad = _pad_head(w_ukv[..., :QK_NOPE]).reshape(KV_LORA, -1).astype(BF16)
    wv_t = w_ukv[..., QK_NOPE:].reshape(KV_LORA, -1).T.astype(BF16)
    head_of = jnp.arange(SG_WIDTH) // SG_HEAD_DIM
    gsum = (head_of[:, None] == head_of[None, :]).astype(BF16)
    wcat = w_spatial.transpose(1, 0, 2).reshape(SG_CHUNK, SG_HEADS * SG_CHUNK)
    bsp = jnp.repeat(b_spatial.T, SG_HEAD_DIM, axis=1)

    q, k, vt, mix_sg = _inproj(xt, cos, sin, row2(attn_pre_g), win_aug, row2(q_norm_g), wq_aug,
                               row2(kv_norm_g), wk_pad, wv_t, row2(sg_norm_g), gsum, wcat, bsp,
                               row2(sg_out_g))
    mix_mla = _attention(q, k, vt, row2(mla_out_g), b, s)

    w_out_b = w_out.astype(BF16)
    x1, h2, idx, gate, rank, cnt = _postattn(
        xt, mix_mla, mix_sg, w_out_b[:MLA_WIDTH], w_out_b[MLA_WIDTH:], row2(attn_post_g), row2(ffn_pre_g),
        w_router.T.astype(BF16), b_router.reshape(N_EXPERTS, 1))

    a = n * TOP_K
    counts = cnt[:, 0].astype(I32)
    padded = (counts + MOE_BM - 1) // MOE_BM * MOE_BM
    pad_end = jnp.cumsum(padded)
    pad_start = pad_end - padded
    start_of = jnp.sum(jnp.where(idx[..., None] == jnp.arange(N_EXPERTS, dtype=I32), pad_start, 0), axis=-1)
    dest = (start_of + rank).reshape(a)
    n_rows = a + N_EXPERTS * MOE_BM

    x_rows = _sc_dispatch(h2, dest, n_rows)
    y_rows = _moe(padded // MOE_BM, pad_start, x_rows, w_gate_up, b_gate_up[:, None, :],
                  w_down, b_down[:, None, :])
    yg = _sc_gather(y_rows, dest).reshape(TOP_K, n, ROW_WORDS)

    out = _final(x1, yg, gate.T, p_l.reshape(n, D_PLE), row2(ffn_post_g), w_ple_gate.astype(BF16),
                 row2(b_ple_gate), w_ple_proj.astype(BF16), row2(ple_norm_g))
    return out.reshape(b, s, D_MODEL)


def kernel(x, p, positions, attn_pre_g, w_in, q_norm_g, w_uq, kv_norm_g, w_ukv, sg_norm_g, w_spatial, b_spatial, mla_out_g, sg_out_g, w_out, attn_post_g, ffn_pre_g, w_router, b_router, w_gate_up, b_gate_up, w_down, b_down, ffn_post_g, w_ple_gate, b_ple_gate, w_ple_proj, ple_norm_g):
    cos, sin = _rope_tables(positions)
    params = (attn_pre_g, w_in, q_norm_g, w_uq, kv_norm_g, w_ukv, sg_norm_g, w_spatial, b_spatial, mla_out_g,
              sg_out_g, w_out, attn_post_g, ffn_pre_g, w_router, b_router, w_gate_up, b_gate_up, w_down, b_down,
              ffn_post_g, w_ple_gate, b_ple_gate, w_ple_proj, ple_norm_g)
    for layer in range(p.shape[0]):
        x = _layer(x, p[layer], cos, sin, tuple(a[layer] for a in params))
    return x
```

```python
import functools

import jax
import jax.numpy as jnp
from jax import lax
from jax.experimental import pallas as pl
from jax.experimental.pallas import tpu as pltpu
from jax.experimental.pallas import tpu_sc as plsc

F32 = jnp.float32
BF16 = jnp.bfloat16
I32 = jnp.int32

D_MODEL = 1024
HEADS = 8
QK_NOPE = 64
QK_ROPE = 32
V_HEAD = 64
Q_LORA = 256
KV_LORA = 128
ROPE_THETA = 10000.0
SG_HEADS = 8
SG_HEAD_DIM = 64
SG_CHUNK = 128
SG_WIDTH = SG_HEADS * SG_HEAD_DIM
MLA_WIDTH = HEADS * V_HEAD
N_EXPERTS = 32
TOP_K = 4
SWIGLU_LIMIT = 7.0
SWIGLU_ALPHA = 1.702
D_PLE = 256
EPS = 1e-6

LANES = 128
HEAD_PAD = LANES
QK_SCALE = (QK_NOPE + QK_ROPE) ** -0.5
LOG2E = 1.4426950408889634

COL_CQ = 0
COL_CKV = COL_CQ + Q_LORA
COL_KR = COL_CKV + KV_LORA
COL_KRS = COL_KR + LANES
COL_SG = COL_KRS + LANES
IN_COLS_AUG = COL_SG + 2 * SG_WIDTH

TM = 512
BQ = 512
ATTN_BK = 1024
ATTN_ONES_ROWS = 16
MOE_BM = 256
MOE_W_CHUNKS = 8
VMEM_LIMIT = 56 * 1024 * 1024

NT_DIMS = (((1,), (1,)), ((), ()))
ROW_WORDS = D_MODEL // 2
SC_CHUNK = 64
SC_DEST_CHUNK = 8192


def _rms(x, g):
    return x * lax.rsqrt(jnp.mean(x * x, axis=-1, keepdims=True) + EPS) * g


def _dot(a, b):
    return jnp.dot(a, b, preferred_element_type=F32)


def _dot_nt(a, b):
    return lax.dot_general(a, b, NT_DIMS, preferred_element_type=F32)


def _pack_halves(x):
    half = x.shape[1] // 2
    return pltpu.pack_elementwise([x[:, :half], x[:, half:]], packed_dtype=BF16)


def _unpack_halves(w):
    return (pltpu.unpack_elementwise(w, index=0, packed_dtype=BF16, unpacked_dtype=F32),
            pltpu.unpack_elementwise(w, index=1, packed_dtype=BF16, unpacked_dtype=F32))


def _inproj_kernel(x_ref, cos_ref, sin_ref, gpre_ref, win_ref, qg_ref, wq_ref, kvg_ref, wk_ref, wvt_ref,
                   sgg_ref, gsum_ref, wcat_ref, bsp_ref, sgo_ref,
                   q_out, k_out, vt_out, sg_out):
    x = x_ref[...]
    h = _rms(x, gpre_ref[...])
    z = _dot(h.astype(BF16), win_ref[...])
    cos = cos_ref[...]
    sin = sin_ref[...]

    cqn = _rms(z[:, COL_CQ:COL_CQ + Q_LORA], qg_ref[...])
    qq = _dot(cqn.astype(BF16), wq_ref[...])
    half = HEADS * HEAD_PAD
    for hd in range(HEADS):
        sl = slice(hd * HEAD_PAD, (hd + 1) * HEAD_PAD)
        sl2 = slice(half + hd * HEAD_PAD, half + (hd + 1) * HEAD_PAD)
        q_out[:, sl] = ((qq[:, sl] * cos + qq[:, sl2] * sin) * (QK_SCALE * LOG2E)).astype(BF16)

    ckvn = _rms(z[:, COL_CKV:COL_CKV + KV_LORA], kvg_ref[...]).astype(BF16)
    kk = _dot(ckvn, wk_ref[...])
    kr = z[:, COL_KR:COL_KR + LANES] * cos + z[:, COL_KRS:COL_KRS + LANES] * sin
    for hd in range(HEADS):
        sl = slice(hd * HEAD_PAD, (hd + 1) * HEAD_PAD)
        k_out[:, sl] = (kk[:, sl] + kr).astype(BF16)
    vt_out[...] = _dot_nt(wvt_ref[...], ckvn).astype(BF16)

    zg = jax.nn.gelu(z[:, COL_SG:COL_SG + 2 * SG_WIDTH])
    u = zg[:, :SG_WIDTH]
    v = zg[:, SG_WIDTH:]
    ms = _dot((v * v).astype(BF16), gsum_ref[...]) * (1.0 / SG_HEAD_DIM)
    vn = v * lax.rsqrt(ms + EPS) * sgg_ref[...]

    row = lax.broadcasted_iota(I32, (SG_CHUNK, SG_HEADS * SG_CHUNK), 0)
    col = lax.broadcasted_iota(I32, (SG_CHUNK, SG_HEADS * SG_CHUNK), 1)
    wcat = jnp.where((col % SG_CHUNK) <= row, wcat_ref[...], 0.0).astype(BF16)
    lane_head = lax.broadcasted_iota(I32, (SG_CHUNK, SG_WIDTH), 1) // SG_HEAD_DIM
    bsp = bsp_ref[...]
    sgo = sgo_ref[...]
    for c in range(x.shape[0] // SG_CHUNK):
        rows = slice(c * SG_CHUNK, (c + 1) * SG_CHUNK)
        vc = vn[rows]
        vbd = jnp.concatenate(
            [jnp.where(lane_head == hd, vc, 0.0).astype(BF16) for hd in range(SG_HEADS)], axis=0)
        vm = _dot(wcat, vbd) + bsp
        sg_out[rows, :] = _rms(u[rows] * vm, sgo).astype(BF16)


def _inproj(x, cos, sin, gpre, win, qg, wq, kvg, wk, wvt, sgg, gsum, wcat, bsp, sgo):
    n = x.shape[0]
    tok = lambda w: pl.BlockSpec((TM, w), lambda i: (i, 0))
    full = lambda a: pl.BlockSpec(a.shape, lambda i: (0,) * a.ndim)
    consts = (gpre, win, qg, wq, kvg, wk, wvt, sgg, gsum, wcat, bsp, sgo)
    return pl.pallas_call(
        _inproj_kernel,
        grid=(n // TM,),
        in_specs=[tok(D_MODEL), tok(LANES), tok(LANES)] + [full(a) for a in consts],
        out_specs=[tok(HEADS * HEAD_PAD), tok(HEADS * HEAD_PAD),
                   pl.BlockSpec((MLA_WIDTH, TM), lambda i: (0, i)), tok(SG_WIDTH)],
        out_shape=[jax.ShapeDtypeStruct((n, HEADS * HEAD_PAD), BF16),
                   jax.ShapeDtypeStruct((n, HEADS * HEAD_PAD), BF16),
                   jax.ShapeDtypeStruct((MLA_WIDTH, n), BF16),
                   jax.ShapeDtypeStruct((n, SG_WIDTH), BF16)],
        compiler_params=pltpu.CompilerParams(dimension_semantics=("arbitrary",),
                                             vmem_limit_bytes=VMEM_LIMIT),
        name="inproj",
    )(x, cos, sin, *consts)


def _attn_kernel(q_ref, k_ref, vt_ref, g_ref, o_ref, m_sc, l_sc, acc_sc):
    i = pl.program_id(1)
    bq = q_ref.shape[0]
    m_sc[...] = jnp.full(m_sc.shape, -jnp.inf, F32)
    l_sc[...] = jnp.zeros(l_sc.shape, F32)
    acc_sc[...] = jnp.zeros(acc_sc.shape, F32)

    def qk(start, nk, hd):
        hs = slice(hd * HEAD_PAD, (hd + 1) * HEAD_PAD)
        return _dot_nt(k_ref[pl.ds(start, nk), hs], q_ref[:, hs])

    def block(start, nk, masked):
        ones = jnp.ones((ATTN_ONES_ROWS, nk), BF16)
        queue = [qk(start, nk, 0), qk(start, nk, 1)]
        for hd in range(HEADS):
            vs = slice(hd * V_HEAD, (hd + 1) * V_HEAD)
            st = queue.pop(0)
            if hd + 2 < HEADS:
                queue.append(qk(start, nk, hd + 2))
            if masked:
                causal = (lax.broadcasted_iota(I32, (nk, bq), 0) <= lax.broadcasted_iota(I32, (nk, bq), 1))
                st = jnp.where(causal, st, -jnp.inf)
            m_prev = m_sc[hd:hd + 1, :]
            m_new = jnp.maximum(m_prev, jnp.max(st, axis=0, keepdims=True))
            alpha = jnp.exp2(m_prev - m_new)
            p = jnp.exp2(st - m_new).astype(BF16)
            vta = jnp.concatenate([vt_ref[vs, pl.ds(start, nk)], ones], axis=0)
            pv = _dot(vta, p)
            l_sc[hd:hd + 1, :] = alpha * l_sc[hd:hd + 1, :] + pv[V_HEAD:V_HEAD + 1, :]
            acc_sc[vs, :] = alpha * acc_sc[vs, :] + pv[:V_HEAD, :]
            m_sc[hd:hd + 1, :] = m_new

    def body(kb, carry):
        block(pl.multiple_of(kb * ATTN_BK, ATTN_BK), ATTN_BK, False)
        return carry

    visible = i * bq
    lax.fori_loop(0, visible // ATTN_BK, body, 0)
    for r in range(ATTN_BK // bq - 1, 0, -1):

        @pl.when(visible % ATTN_BK >= r * bq)
        def _():
            block(pl.multiple_of((i - r) * bq, bq), bq, False)

    block(pl.multiple_of(i * bq, bq), bq, True)
    ot = jnp.concatenate(
        [acc_sc[hd * V_HEAD:(hd + 1) * V_HEAD, :] * (1.0 / l_sc[hd:hd + 1, :]) for hd in range(HEADS)], axis=0)
    o_ref[...] = _rms(ot.T, g_ref[...]).astype(BF16)


def _attention(q, k, vt, g, b, s):
    nq = s // BQ
    return pl.pallas_call(
        _attn_kernel,
        grid=(b, nq),
        in_specs=[pl.BlockSpec((BQ, HEADS * HEAD_PAD), lambda bi, i: (bi * nq + i, 0)),
                  pl.BlockSpec((s, HEADS * HEAD_PAD), lambda bi, i: (bi, 0), pipeline_mode=pl.Buffered(1)),
                  pl.BlockSpec((MLA_WIDTH, s), lambda bi, i: (0, bi), pipeline_mode=pl.Buffered(1)),
                  pl.BlockSpec(g.shape, lambda bi, i: (0, 0))],
        out_specs=pl.BlockSpec((BQ, MLA_WIDTH), lambda bi, i: (bi * nq + i, 0)),
        out_shape=jax.ShapeDtypeStruct((b * s, MLA_WIDTH), BF16),
        scratch_shapes=[pltpu.VMEM((HEADS, BQ), F32), pltpu.VMEM((HEADS, BQ), F32),
                        pltpu.VMEM((MLA_WIDTH, BQ), F32)],
        compiler_params=pltpu.CompilerParams(dimension_semantics=("arbitrary", "arbitrary"),
                                             vmem_limit_bytes=VMEM_LIMIT),
        name="attention",
    )(q, k, vt, g)


def _postattn_kernel(x_ref, mla_ref, sg_ref, wo1_ref, wo2_ref, gpost_ref, gffn_ref, wrt_ref, brt_ref,
                     x1_out, h2_out, idx_out, gate_out, rank_out, cnt_out, cnt_sc):
    i = pl.program_id(0)

    @pl.when(i == 0)
    def _():
        cnt_sc[...] = jnp.zeros(cnt_sc.shape, F32)

    a = _dot(mla_ref[...], wo1_ref[...]) + _dot(sg_ref[...], wo2_ref[...])
    x1 = x_ref[...] + _rms(a, gpost_ref[...])
    x1_out[...] = x1
    h2f = _rms(x1, gffn_ref[...])
    h2_out[...] = _pack_halves(h2f)
    h2 = h2f.astype(BF16)

    tm = h2.shape[0]
    logits = _dot_nt(wrt_ref[...], h2) + brt_ref[...]
    eidx = lax.broadcasted_iota(I32, (N_EXPERTS, tm), 0)
    vals, idxs, sels = [], [], []
    for _ in range(TOP_K):
        mx = jnp.max(logits, axis=0, keepdims=True)
        ik = jnp.min(jnp.where(logits == mx, eidx, N_EXPERTS), axis=0, keepdims=True)
        sel = eidx == ik
        vals.append(mx)
        idxs.append(ik)
        sels.append(sel)
        logits = jnp.where(sel, -jnp.inf, logits)
    ex = [jnp.exp(v - vals[0]) for v in vals]
    den = ex[0] + ex[1] + ex[2] + ex[3]
    gate_out[...] = jnp.concatenate([e / den for e in ex], axis=0)
    idx_out[...] = jnp.concatenate(idxs, axis=0)

    maskf = sum(jnp.where(s, 1.0, 0.0) for s in sels)
    before = (lax.broadcasted_iota(I32, (tm, tm), 0) < lax.broadcasted_iota(I32, (tm, tm), 1))
    prefix = _dot(maskf.astype(BF16), jnp.where(before, 1.0, 0.0).astype(BF16))
    base = cnt_sc[...]
    tot = base + prefix
    ranks = [jnp.sum(jnp.where(s, tot, 0.0), axis=0, keepdims=True) for s in sels]
    rank_out[...] = jnp.concatenate(ranks, axis=0).astype(I32)
    cnt = base + jnp.sum(maskf, axis=1, keepdims=True)
    cnt_sc[...] = cnt
    cnt_out[...] = jnp.broadcast_to(cnt, cnt_out.shape)


def _postattn(x, mla, sg, wo1, wo2, gpost, gffn, wrt, brt):
    n = x.shape[0]
    tok = lambda w: pl.BlockSpec((TM, w), lambda i: (i, 0))
    tokt = pl.BlockSpec((TOP_K, TM), lambda i: (0, i))
    full = lambda a: pl.BlockSpec(a.shape, lambda i: (0,) * a.ndim)
    consts = (wo1, wo2, gpost, gffn, wrt, brt)
    return pl.pallas_call(
        _postattn_kernel,
        grid=(n // TM,),
        in_specs=[tok(D_MODEL), tok(MLA_WIDTH), tok(SG_WIDTH)] + [full(a) for a in consts],
        out_specs=[tok(D_MODEL), tok(ROW_WORDS), tokt, tokt, tokt,
                   pl.BlockSpec((N_EXPERTS, LANES), lambda i: (0, 0))],
        out_shape=[jax.ShapeDtypeStruct((n, D_MODEL), F32),
                   jax.ShapeDtypeStruct((n, ROW_WORDS), I32),
                   jax.ShapeDtypeStruct((TOP_K, n), I32),
                   jax.ShapeDtypeStruct((TOP_K, n), F32),
                   jax.ShapeDtypeStruct((TOP_K, n), I32),
                   jax.ShapeDtypeStruct((N_EXPERTS, LANES), F32)],
        scratch_shapes=[pltpu.VMEM((N_EXPERTS, 1), F32)],
        compiler_params=pltpu.CompilerParams(dimension_semantics=("arbitrary",),
                                             vmem_limit_bytes=VMEM_LIMIT),
        name="postattn",
    )(x, mla, sg, *consts)


def _sc_workers():
    info = plsc.get_sparse_core_info()
    return info.num_cores, info.num_cores * info.num_subcores, info.num_lanes


def _sc_stream_rows(table_hbm, idx_v, out_hbm, base, n_chunks, buf, sem_g, sem_w):
    def gather(j, b):
        rows = idx_v.at[pl.ds(pl.multiple_of(j * SC_CHUNK, SC_CHUNK), SC_CHUNK)]
        return pltpu.make_async_copy(table_hbm.at[rows], buf.at[b], sem_g.at[b])

    def write(j, b):
        rows = pl.ds(pl.multiple_of(base + j * SC_CHUNK, SC_CHUNK), SC_CHUNK)
        return pltpu.make_async_copy(buf.at[b], out_hbm.at[rows], sem_w.at[b])

    gather(0, 0).start()

    @pl.loop(0, n_chunks, step=2)
    def _(j0):
        for b in range(2):
            j = j0 + b
            gather(j, b).wait()

            @pl.when(j >= 1)
            def _():
                write(j - 1, 1 - b).wait()

            @pl.when(j + 1 < n_chunks)
            def _():
                gather(j + 1, 1 - b).start()

            write(j, b).start()

    write(n_chunks - 1, 1).wait()


def _sc_gather(table, idx):
    n_out = idx.shape[0]
    width = table.shape[1]
    num_cores, workers, _ = _sc_workers()
    per_w = n_out // workers
    n_chunks = per_w // SC_CHUNK
    assert per_w * workers == n_out and n_chunks * SC_CHUNK == per_w and n_chunks % 2 == 0

    @functools.partial(
        pl.kernel, mesh=plsc.VectorSubcoreMesh(core_axis_name="c", subcore_axis_name="s"),
        out_type=jax.ShapeDtypeStruct((n_out, width), table.dtype),
        scratch_types=[pltpu.VMEM((per_w,), I32), pltpu.VMEM((2, SC_CHUNK, width), table.dtype),
                       pltpu.SemaphoreType.DMA((2,)), pltpu.SemaphoreType.DMA((2,))])
    def gather_kernel(table_hbm, idx_hbm, out_hbm, idx_v, buf, sem_g, sem_w):
        wid = lax.axis_index("s") * num_cores + lax.axis_index("c")
        base = pl.multiple_of(wid * per_w, SC_CHUNK)
        pltpu.sync_copy(idx_hbm.at[pl.ds(base, per_w)], idx_v)
        _sc_stream_rows(table_hbm, idx_v, out_hbm, base, n_chunks, buf, sem_g, sem_w)

    return gather_kernel(table, idx)


def _sc_dispatch(table, dest, n_out):
    n_assign = dest.shape[0]
    n, width = table.shape
    num_cores, workers, lanes = _sc_workers()
    per_w = n_out // workers
    n_chunks = per_w // SC_CHUNK
    assert per_w * workers == n_out and n_chunks * SC_CHUNK == per_w and n_chunks % 2 == 0
    assert n_assign % SC_DEST_CHUNK == 0 and SC_DEST_CHUNK % lanes == 0 and per_w % lanes == 0

    @functools.partial(
        pl.kernel, mesh=plsc.VectorSubcoreMesh(core_axis_name="c", subcore_axis_name="s"),
        out_type=jax.ShapeDtypeStruct((n_out, width), table.dtype),
        scratch_types=[pltpu.VMEM((per_w,), I32), pltpu.VMEM((SC_DEST_CHUNK,), I32),
                       pltpu.VMEM((2, SC_CHUNK, width), table.dtype),
                       pltpu.SemaphoreType.DMA((2,)), pltpu.SemaphoreType.DMA((2,))],
        compiler_params=pltpu.CompilerParams(needs_layout_passes=False))
    def dispatch_kernel(table_hbm, dest_hbm, out_hbm, tok_v, dest_v, buf, sem_g, sem_w):
        wid = lax.axis_index("s") * num_cores + lax.axis_index("c")
        base = pl.multiple_of(wid * per_w, SC_CHUNK)
        lane = lax.iota(I32, lanes)

        @pl.loop(0, per_w, step=lanes)
        def _(r):
            tok_v[pl.ds(r, lanes)] = lax.rem(base + r + lane, n)

        @pl.loop(0, n_assign // SC_DEST_CHUNK)
        def _(c):
            first = pl.multiple_of(c * SC_DEST_CHUNK, SC_DEST_CHUNK)
            pltpu.sync_copy(dest_hbm.at[pl.ds(first, SC_DEST_CHUNK)], dest_v)

            @plsc.parallel_loop(0, SC_DEST_CHUNK, step=lanes, unroll=4)
            def _(i):
                local = dest_v[pl.ds(i, lanes)] - base
                mine = (local >= 0) & (local < per_w)
                tok = lax.rem(first + i + lane, n)
                plsc.store_scatter(tok_v, [jnp.where(mine, local, 0)], tok, mask=mine)

        _sc_stream_rows(table_hbm, tok_v, out_hbm, base, n_chunks, buf, sem_g, sem_w)

    return dispatch_kernel(table, dest)


def _moe_kernel(nb_ref, row0_ref, x_hbm, wgu_hbm, bgu_ref, wd_hbm, bd_ref, y_hbm,
                wgu_f, wd_f, wgu_sc, wd_sc, xbuf, ybuf, sem_x, sem_y, sem_w):
    e = pl.program_id(0)
    n_e = pl.num_programs(0)
    nb = nb_ref[e]
    row0 = row0_ref[e]
    wslot = e % 2
    chunk_rows = D_MODEL // MOE_W_CHUNKS

    def w_copies(expert, c, slot):
        r = pl.ds(c * chunk_rows, chunk_rows)
        return (pltpu.make_async_copy(wgu_hbm.at[expert, r, :], wgu_f.at[slot, r, :], sem_w.at[slot]),
                pltpu.make_async_copy(wd_hbm.at[expert, r, :], wd_f.at[slot, r, :], sem_w.at[slot]))

    def start_chunk(expert, c, slot):
        for cp in w_copies(expert, c, slot):
            cp.start()

    def rows(j):
        return pl.ds(pl.multiple_of(row0 + j * MOE_BM, MOE_BM), MOE_BM)

    def x_copy(j, slot):
        return pltpu.make_async_copy(x_hbm.at[rows(j)], xbuf.at[slot], sem_x.at[slot])

    def y_copy(j, slot):
        return pltpu.make_async_copy(ybuf.at[slot], y_hbm.at[rows(j)], sem_y.at[slot])

    @pl.when(e == 0)
    def _():
        for c in range(MOE_W_CHUNKS):
            start_chunk(0, c, 0)

    @pl.when(nb > 0)
    def _():
        x_copy(0, 0).start()

    for c in range(MOE_W_CHUNKS):
        for cp in w_copies(e, c, wslot):
            cp.wait()
    wgu_sc[...] = wgu_f[wslot].astype(BF16)
    wd_sc[...] = wd_f[wslot].astype(BF16)
    nxt = jnp.minimum(e + 1, n_e - 1)
    has_next = e + 1 < n_e

    def body(j, carry):
        slot = j % 2

        @pl.when(has_next & (j < MOE_W_CHUNKS))
        def _():
            start_chunk(nxt, j, 1 - wslot)

        x_copy(j, slot).wait()

        @pl.when(j + 1 < nb)
        def _():
            x_copy(j + 1, 1 - slot).start()

        @pl.when(j >= 2)
        def _():
            y_copy(j - 2, slot).wait()

        x = jnp.concatenate(_unpack_halves(xbuf[slot]), axis=1).astype(BF16)
        gu = _dot(x, wgu_sc[...]) + bgu_ref[0]
        g = jnp.minimum(gu[:, :D_MODEL], SWIGLU_LIMIT)
        u = jnp.clip(gu[:, D_MODEL:], -SWIGLU_LIMIT, SWIGLU_LIMIT)
        act = (u + 1.0) * (g * jax.nn.sigmoid(SWIGLU_ALPHA * g))
        ybuf[slot] = _pack_halves(_dot(act.astype(BF16), wd_sc[...]) + bd_ref[0])
        y_copy(j, slot).start()
        return carry

    lax.fori_loop(0, nb, body, 0)

    def rest(c, carry):
        @pl.when(has_next)
        def _():
            start_chunk(nxt, c, 1 - wslot)
        return carry

    lax.fori_loop(jnp.minimum(nb, MOE_W_CHUNKS), MOE_W_CHUNKS, rest, 0)

    @pl.when(nb >= 2)
    def _():
        y_copy(nb - 2, nb % 2).wait()

    @pl.when(nb >= 1)
    def _():
        y_copy(nb - 1, (nb - 1) % 2).wait()


def _moe(blocks_of, row0_of, x_rows, wgu, bgu, wd, bd):
    hbm = pl.BlockSpec(memory_space=pl.ANY)
    grid_spec = pltpu.PrefetchScalarGridSpec(
        num_scalar_prefetch=2,
        grid=(N_EXPERTS,),
        in_specs=[hbm, hbm,
                  pl.BlockSpec((1, 1, 2 * D_MODEL), lambda e, nb, r0: (e, 0, 0)),
                  hbm,
                  pl.BlockSpec((1, 1, D_MODEL), lambda e, nb, r0: (e, 0, 0))],
        out_specs=hbm,
        scratch_shapes=[pltpu.VMEM((2, D_MODEL, 2 * D_MODEL), F32), pltpu.VMEM((2, D_MODEL, D_MODEL), F32),
                        pltpu.VMEM((D_MODEL, 2 * D_MODEL), BF16), pltpu.VMEM((D_MODEL, D_MODEL), BF16),
                        pltpu.VMEM((2, MOE_BM, ROW_WORDS), I32), pltpu.VMEM((2, MOE_BM, ROW_WORDS), I32),
                        pltpu.SemaphoreType.DMA((2,)), pltpu.SemaphoreType.DMA((2,)),
                        pltpu.SemaphoreType.DMA((2,))],
    )
    return pl.pallas_call(
        _moe_kernel,
        grid_spec=grid_spec,
        out_shape=jax.ShapeDtypeStruct(x_rows.shape, I32),
        compiler_params=pltpu.CompilerParams(dimension_semantics=("arbitrary",),
                                             vmem_limit_bytes=VMEM_LIMIT),
        name="moe",
    )(blocks_of, row0_of, x_rows, wgu, bgu, wd, bd)


def _final_kernel(x1_ref, yg_ref, gt_ref, p_ref, gpost_ref, wg_ref, bg_ref, wp_ref, gple_ref, o_ref):
    gt = gt_ref[...]
    lo, hi = _unpack_halves(yg_ref[0])
    y = gt[:, 0:1] * jnp.concatenate([lo, hi], axis=1)
    for k in range(1, TOP_K):
        lo, hi = _unpack_halves(yg_ref[k])
        y = y + gt[:, k:k + 1] * jnp.concatenate([lo, hi], axis=1)
    x2 = x1_ref[...] + _rms(y, gpost_ref[...])
    gate = jax.nn.sigmoid(_dot(x2.astype(BF16), wg_ref[...]) + bg_ref[...])
    pp = _dot(p_ref[...].astype(BF16), wp_ref[...])
    o_ref[...] = x2 + _rms(gate * pp, gple_ref[...])


def _final(x1, yg, gate_t, p, gpost, wg, bg, wp, gple):
    n = x1.shape[0]
    tok = lambda w: pl.BlockSpec((TM, w), lambda i: (i, 0))
    full = lambda a: pl.BlockSpec(a.shape, lambda i: (0,) * a.ndim)
    consts = (gpost, wg, bg, wp, gple)
    return pl.pallas_call(
        _final_kernel,
        grid=(n // TM,),
        in_specs=[tok(D_MODEL), pl.BlockSpec((TOP_K, TM, ROW_WORDS), lambda i: (0, i, 0)), tok(TOP_K),
                  tok(D_PLE)] + [full(a) for a in consts],
        out_specs=tok(D_MODEL),
        out_shape=jax.ShapeDtypeStruct((n, D_MODEL), F32),
        compiler_params=pltpu.CompilerParams(dimension_semantics=("arbitrary",),
                                             vmem_limit_bytes=VMEM_LIMIT),
        name="final",
    )(x1, yg, gate_t, p, *consts)


def _rope_tables(positions):
    inv_freq = 1.0 / (ROPE_THETA ** (jnp.arange(0, QK_ROPE, 2, dtype=F32) / QK_ROPE))
    freq = jnp.concatenate([jnp.zeros((QK_NOPE,), F32), inv_freq, inv_freq,
                            jnp.zeros((HEAD_PAD - QK_NOPE - QK_ROPE,), F32)])
    ang = positions.astype(F32).reshape(-1)[:, None] * freq
    return jnp.cos(ang), jnp.sin(ang)


def _rot_half_cols(w):
    half = w.shape[-1] // 2
    return jnp.concatenate([-w[..., half:], w[..., :half]], axis=-1)


def _pad_head(w):
    return jnp.pad(w, [(0, 0)] * (w.ndim - 1) + [(0, HEAD_PAD - w.shape[-1])])


def _layer(x, p_l, cos, sin, prm):
    (attn_pre_g, w_in, q_norm_g, w_uq, kv_norm_g, w_ukv, sg_norm_g, w_spatial, b_spatial, mla_out_g,
     sg_out_g, w_out, attn_post_g, ffn_pre_g, w_router, b_router, w_gate_up, b_gate_up, w_down, b_down,
     ffn_post_g, w_ple_gate, b_ple_gate, w_ple_proj, ple_norm_g) = prm
    b, s, _ = x.shape
    n = b * s
    xt = x.reshape(n, D_MODEL)
    row2 = lambda a: a.reshape(1, -1)

    w_kr = w_in[:, Q_LORA + KV_LORA:Q_LORA + KV_LORA + QK_ROPE]
    place = lambda w: jnp.pad(w, ((0, 0), (QK_NOPE, HEAD_PAD - QK_NOPE - QK_ROPE)))
    win_aug = jnp.concatenate(
        [w_in[:, :Q_LORA + KV_LORA], place(w_kr), place(_rot_half_cols(w_kr)),
         w_in[:, Q_LORA + KV_LORA + QK_ROPE:]], axis=-1).astype(BF16)
    w_q_rot = jnp.concatenate([jnp.zeros_like(w_uq[..., :QK_NOPE]), _rot_half_cols(w_uq[..., QK_NOPE:])],
                              axis=-1)
    wq_aug = jnp.concatenate([_pad_head(w_uq).reshape(Q_LORA, -1), _pad_head(w_q_rot).reshape(Q_LORA, -1)],
                             axis=-1).astype(BF16)
    wk_pad = _pad_head(w_ukv[..., :QK_NOPE]).reshape(KV_LORA, -1).astype(BF16)
    wv_t = w_ukv[..., QK_NOPE:].reshape(KV_LORA, -1).T.astype(BF16)
    head_of = jnp.arange(SG_WIDTH) // SG_HEAD_DIM
    gsum = (head_of[:, None] == head_of[None, :]).astype(BF16)
    wcat = w_spatial.transpose(1, 0, 2).reshape(SG_CHUNK, SG_HEADS * SG_CHUNK)
    bsp = jnp.repeat(b_spatial.T, SG_HEAD_DIM, axis=1)

    q, k, vt, mix_sg = _inproj(xt, cos, sin, row2(attn_pre_g), win_aug, row2(q_norm_g), wq_aug,
                               row2(kv_norm_g), wk_pad, wv_t, row2(sg_norm_g), gsum, wcat, bsp,
                               row2(sg_out_g))
    mix_mla = _attention(q, k, vt, row2(mla_out_g), b, s)

    w_out_b = w_out.astype(BF16)
    x1, h2, idx, gate, rank, cnt = _postattn(
        xt, mix_mla, mix_sg, w_out_b[:MLA_WIDTH], w_out_b[MLA_WIDTH:], row2(attn_post_g), row2(ffn_pre_g),
        w_router.T.astype(BF16), b_router.reshape(N_EXPERTS, 1))

    a = n * TOP_K
    counts = cnt[:, 0].astype(I32)
    padded = (counts + MOE_BM - 1) // MOE_BM * MOE_BM
    pad_end = jnp.cumsum(padded)
    pad_start = pad_end - padded
    start_of = jnp.sum(jnp.where(idx[..., None] == jnp.arange(N_EXPERTS, dtype=I32), pad_start, 0), axis=-1)
    dest = (start_of + rank).reshape(a)
    n_rows = a + N_EXPERTS * MOE_BM

    x_rows = _sc_dispatch(h2, dest, n_rows)
    y_rows = _moe(padded // MOE_BM, pad_start, x_rows, w_gate_up, b_gate_up[:, None, :],
                  w_down, b_down[:, None, :])
    yg = _sc_gather(y_rows, dest).reshape(TOP_K, n, ROW_WORDS)

    out = _final(x1, yg, gate.T, p_l.reshape(n, D_PLE), row2(ffn_post_g), w_ple_gate.astype(BF16),
                 row2(b_ple_gate), w_ple_proj.astype(BF16), row2(ple_norm_g))
    return out.reshape(b, s, D_MODEL)


def kernel(x, p, positions, attn_pre_g, w_in, q_norm_g, w_uq, kv_norm_g, w_ukv, sg_norm_g, w_spatial, b_spatial, mla_out_g, sg_out_g, w_out, attn_post_g, ffn_pre_g, w_router, b_router, w_gate_up, b_gate_up, w_down, b_down, ffn_post_g, w_ple_gate, b_ple_gate, w_ple_proj, ple_norm_g):
    cos, sin = _rope_tables(positions)
    params = (attn_pre_g, w_in, q_norm_g, w_uq, kv_norm_g, w_ukv, sg_norm_g, w_spatial, b_spatial, mla_out_g,
              sg_out_g, w_out, attn_post_g, ffn_pre_g, w_router, b_router, w_gate_up, b_gate_up, w_down, b_down,
              ffn_post_g, w_ple_gate, b_ple_gate, w_ple_proj, ple_norm_g)
    for layer in range(p.shape[0]):
        x = _layer(x, p[layer], cos, sin, tuple(a[layer] for a in params))
    return x
```

```python
import functools

import jax
import jax.numpy as jnp
from jax import lax
from jax.experimental import pallas as pl
from jax.experimental.pallas import tpu as pltpu
from jax.experimental.pallas import tpu_sc as plsc

F32 = jnp.float32
BF16 = jnp.bfloat16
I32 = jnp.int32

D_MODEL = 1024
HEADS = 8
QK_NOPE = 64
QK_ROPE = 32
V_HEAD = 64
Q_LORA = 256
KV_LORA = 128
ROPE_THETA = 10000.0
SG_HEADS = 8
SG_HEAD_DIM = 64
SG_CHUNK = 128
SG_WIDTH = SG_HEADS * SG_HEAD_DIM
MLA_WIDTH = HEADS * V_HEAD
N_EXPERTS = 32
TOP_K = 4
SWIGLU_LIMIT = 7.0
SWIGLU_ALPHA = 1.702
D_PLE = 256
EPS = 1e-6

LANES = 128
HEAD_PAD = LANES
QK_SCALE = (QK_NOPE + QK_ROPE) ** -0.5
LOG2E = 1.4426950408889634

COL_CQ = 0
COL_CKV = COL_CQ + Q_LORA
COL_KR = COL_CKV + KV_LORA
COL_SG = COL_KR + LANES
IN_COLS_AUG = COL_SG + 2 * SG_WIDTH

TM = 512
BQ = 512
ATTN_BK = 1024
ATTN_ONES_ROWS = 16
MOE_BM = 256
MOE_W_CHUNKS = 8
VMEM_LIMIT = 56 * 1024 * 1024

NT_DIMS = (((1,), (1,)), ((), ()))
ROW_WORDS = D_MODEL // 2
SC_CHUNK = 64
SC_DEST_CHUNK = 8192


def _rms(x, g):
    return x * lax.rsqrt(jnp.mean(x * x, axis=-1, keepdims=True) + EPS) * g


def _dot(a, b):
    return jnp.dot(a, b, preferred_element_type=F32)


def _dot_nt(a, b):
    return lax.dot_general(a, b, NT_DIMS, preferred_element_type=F32)


def _pack_halves(x):
    half = x.shape[1] // 2
    return pltpu.pack_elementwise([x[:, :half], x[:, half:]], packed_dtype=BF16)


def _unpack_halves(w):
    return (pltpu.unpack_elementwise(w, index=0, packed_dtype=BF16, unpacked_dtype=F32),
            pltpu.unpack_elementwise(w, index=1, packed_dtype=BF16, unpacked_dtype=F32))


def _inproj_kernel(x_ref, ang_ref, gpre_ref, win_ref, qg_ref, wq_ref, kvg_ref, wk_ref, wvt_ref,
                   sgg_ref, gsum_ref, wcat_ref, bsp_ref, sgo_ref,
                   q_out, k_out, vt_out, sg_out):
    x = x_ref[...]
    tm = x.shape[0]
    h = _rms(x, gpre_ref[...])
    z = _dot(h.astype(BF16), win_ref[...])

    ang = ang_ref[...]
    c, s = jnp.cos(ang), jnp.sin(ang)
    tail = HEAD_PAD - QK_NOPE - QK_ROPE
    cos = jnp.concatenate([jnp.ones((QK_NOPE, tm), F32), c, c, jnp.ones((tail, tm), F32)], axis=0).T
    sin = jnp.concatenate([jnp.zeros((QK_NOPE, tm), F32), s, s, jnp.zeros((tail, tm), F32)], axis=0).T
    first_half = lax.broadcasted_iota(I32, (tm, HEAD_PAD), 1) < QK_NOPE + QK_ROPE // 2
    sin_signed = jnp.where(first_half, -sin, sin)

    def rope(t):
        partner = jnp.where(first_half, pltpu.roll(t, HEAD_PAD - QK_ROPE // 2, 1), pltpu.roll(t, QK_ROPE // 2, 1))
        return t * cos + partner * sin_signed

    cqn = _rms(z[:, COL_CQ:COL_CQ + Q_LORA], qg_ref[...])
    qq = _dot(cqn.astype(BF16), wq_ref[...])
    for hd in range(HEADS):
        sl = slice(hd * HEAD_PAD, (hd + 1) * HEAD_PAD)
        q_out[:, sl] = (rope(qq[:, sl]) * (QK_SCALE * LOG2E)).astype(BF16)

    ckvn = _rms(z[:, COL_CKV:COL_CKV + KV_LORA], kvg_ref[...]).astype(BF16)
    kk = _dot(ckvn, wk_ref[...])
    kr = rope(z[:, COL_KR:COL_KR + LANES])
    for hd in range(HEADS):
        sl = slice(hd * HEAD_PAD, (hd + 1) * HEAD_PAD)
        k_out[:, sl] = (kk[:, sl] + kr).astype(BF16)
    vt_out[...] = _dot_nt(wvt_ref[...], ckvn).astype(BF16)

    zg = jax.nn.gelu(z[:, COL_SG:COL_SG + 2 * SG_WIDTH])
    u = zg[:, :SG_WIDTH]
    v = zg[:, SG_WIDTH:]
    ms = _dot((v * v).astype(BF16), gsum_ref[...]) * (1.0 / SG_HEAD_DIM)
    vn = v * lax.rsqrt(ms + EPS) * sgg_ref[...]

    row = lax.broadcasted_iota(I32, (SG_CHUNK, SG_HEADS * SG_CHUNK), 0)
    col = lax.broadcasted_iota(I32, (SG_CHUNK, SG_HEADS * SG_CHUNK), 1)
    wcat = jnp.where((col % SG_CHUNK) <= row, wcat_ref[...], 0.0).astype(BF16)
    lane_head = lax.broadcasted_iota(I32, (SG_CHUNK, SG_WIDTH), 1) // SG_HEAD_DIM
    bsp = bsp_ref[...]
    sgo = sgo_ref[...]
    for c in range(x.shape[0] // SG_CHUNK):
        rows = slice(c * SG_CHUNK, (c + 1) * SG_CHUNK)
        vc = vn[rows]
        vbd = jnp.concatenate(
            [jnp.where(lane_head == hd, vc, 0.0).astype(BF16) for hd in range(SG_HEADS)], axis=0)
        vm = _dot(wcat, vbd) + bsp
        sg_out[rows, :] = _rms(u[rows] * vm, sgo).astype(BF16)


def _inproj(x, ang, gpre, win, qg, wq, kvg, wk, wvt, sgg, gsum, wcat, bsp, sgo):
    n = x.shape[0]
    tok = lambda w: pl.BlockSpec((TM, w), lambda i: (i, 0))
    full = lambda a: pl.BlockSpec(a.shape, lambda i: (0,) * a.ndim)
    consts = (gpre, win, qg, wq, kvg, wk, wvt, sgg, gsum, wcat, bsp, sgo)
    return pl.pallas_call(
        _inproj_kernel,
        grid=(n // TM,),
        in_specs=[tok(D_MODEL), pl.BlockSpec((QK_ROPE // 2, TM), lambda i: (0, i))] + [full(a) for a in consts],
        out_specs=[tok(HEADS * HEAD_PAD), tok(HEADS * HEAD_PAD),
                   pl.BlockSpec((MLA_WIDTH, TM), lambda i: (0, i)), tok(SG_WIDTH)],
        out_shape=[jax.ShapeDtypeStruct((n, HEADS * HEAD_PAD), BF16),
                   jax.ShapeDtypeStruct((n, HEADS * HEAD_PAD), BF16),
                   jax.ShapeDtypeStruct((MLA_WIDTH, n), BF16),
                   jax.ShapeDtypeStruct((n, SG_WIDTH), BF16)],
        compiler_params=pltpu.CompilerParams(dimension_semantics=("arbitrary",),
                                             vmem_limit_bytes=VMEM_LIMIT),
        name="inproj",
    )(x, ang, *consts)


def _attn_kernel(q_ref, k_ref, vt_ref, g_ref, o_ref, m_sc, l_sc, acc_sc):
    i = pl.program_id(1)
    bq = q_ref.shape[0]
    m_sc[...] = jnp.full(m_sc.shape, -jnp.inf, F32)
    l_sc[...] = jnp.zeros(l_sc.shape, F32)
    acc_sc[...] = jnp.zeros(acc_sc.shape, F32)

    def qk(start, nk, hd):
        hs = slice(hd * HEAD_PAD, (hd + 1) * HEAD_PAD)
        return _dot_nt(k_ref[pl.ds(start, nk), hs], q_ref[:, hs])

    def block(start, nk, masked):
        ones = jnp.ones((ATTN_ONES_ROWS, nk), BF16)
        queue = [qk(start, nk, 0), qk(start, nk, 1)]
        for hd in range(HEADS):
            vs = slice(hd * V_HEAD, (hd + 1) * V_HEAD)
            st = queue.pop(0)
            if hd + 2 < HEADS:
                queue.append(qk(start, nk, hd + 2))
            if masked:
                causal = (lax.broadcasted_iota(I32, (nk, bq), 0) <= lax.broadcasted_iota(I32, (nk, bq), 1))
                st = jnp.where(causal, st, -jnp.inf)
            m_prev = m_sc[hd:hd + 1, :]
            m_new = jnp.maximum(m_prev, jnp.max(st, axis=0, keepdims=True))
            alpha = jnp.exp2(m_prev - m_new)
            p = jnp.exp2(st - m_new).astype(BF16)
            vta = jnp.concatenate([vt_ref[vs, pl.ds(start, nk)], ones], axis=0)
            pv = _dot(vta, p)
            l_sc[hd:hd + 1, :] = alpha * l_sc[hd:hd + 1, :] + pv[V_HEAD:V_HEAD + 1, :]
            acc_sc[vs, :] = alpha * acc_sc[vs, :] + pv[:V_HEAD, :]
            m_sc[hd:hd + 1, :] = m_new

    def body(kb, carry):
        block(pl.multiple_of(kb * ATTN_BK, ATTN_BK), ATTN_BK, False)
        return carry

    visible = i * bq
    lax.fori_loop(0, visible // ATTN_BK, body, 0)
    for r in range(ATTN_BK // bq - 1, 0, -1):

        @pl.when(visible % ATTN_BK >= r * bq)
        def _():
            block(pl.multiple_of((i - r) * bq, bq), bq, False)

    block(pl.multiple_of(i * bq, bq), bq, True)
    ot = jnp.concatenate(
        [acc_sc[hd * V_HEAD:(hd + 1) * V_HEAD, :] * (1.0 / l_sc[hd:hd + 1, :]) for hd in range(HEADS)], axis=0)
    o_ref[...] = _rms(ot.T, g_ref[...]).astype(BF16)


def _attention(q, k, vt, g, b, s):
    nq = s // BQ
    return pl.pallas_call(
        _attn_kernel,
        grid=(b, nq),
        in_specs=[pl.BlockSpec((BQ, HEADS * HEAD_PAD), lambda bi, i: (bi * nq + i, 0)),
                  pl.BlockSpec((s, HEADS * HEAD_PAD), lambda bi, i: (bi, 0), pipeline_mode=pl.Buffered(1)),
                  pl.BlockSpec((MLA_WIDTH, s), lambda bi, i: (0, bi), pipeline_mode=pl.Buffered(1)),
                  pl.BlockSpec(g.shape, lambda bi, i: (0, 0))],
        out_specs=pl.BlockSpec((BQ, MLA_WIDTH), lambda bi, i: (bi * nq + i, 0)),
        out_shape=jax.ShapeDtypeStruct((b * s, MLA_WIDTH), BF16),
        scratch_shapes=[pltpu.VMEM((HEADS, BQ), F32), pltpu.VMEM((HEADS, BQ), F32),
                        pltpu.VMEM((MLA_WIDTH, BQ), F32)],
        compiler_params=pltpu.CompilerParams(dimension_semantics=("arbitrary", "arbitrary"),
                                             vmem_limit_bytes=VMEM_LIMIT),
        name="attention",
    )(q, k, vt, g)


def _postattn_kernel(x_ref, mla_ref, sg_ref, wo1_ref, wo2_ref, gpost_ref, gffn_ref, wrt_ref, brt_ref,
                     x1_out, h2_out, idx_out, gate_out, rank_out, cnt_out, cnt_sc):
    i = pl.program_id(0)

    @pl.when(i == 0)
    def _():
        cnt_sc[...] = jnp.zeros(cnt_sc.shape, F32)

    a = _dot(mla_ref[...], wo1_ref[...]) + _dot(sg_ref[...], wo2_ref[...])
    x1 = x_ref[...] + _rms(a, gpost_ref[...])
    x1_out[...] = x1
    h2f = _rms(x1, gffn_ref[...])
    h2_out[...] = _pack_halves(h2f)
    h2 = h2f.astype(BF16)

    tm = h2.shape[0]
    logits = _dot_nt(wrt_ref[...], h2) + brt_ref[...]
    eidx = lax.broadcasted_iota(I32, (N_EXPERTS, tm), 0)
    vals, idxs, sels = [], [], []
    for _ in range(TOP_K):
        mx = jnp.max(logits, axis=0, keepdims=True)
        ik = jnp.min(jnp.where(logits == mx, eidx, N_EXPERTS), axis=0, keepdims=True)
        sel = eidx == ik
        vals.append(mx)
        idxs.append(ik)
        sels.append(sel)
        logits = jnp.where(sel, -jnp.inf, logits)
    ex = [jnp.exp(v - vals[0]) for v in vals]
    den = ex[0] + ex[1] + ex[2] + ex[3]
    gate_out[...] = jnp.concatenate([e / den for e in ex], axis=0)
    idx_out[...] = jnp.concatenate(idxs, axis=0)

    maskf = sum(jnp.where(s, 1.0, 0.0) for s in sels)
    before = (lax.broadcasted_iota(I32, (tm, tm), 0) < lax.broadcasted_iota(I32, (tm, tm), 1))
    prefix = _dot(maskf.astype(BF16), jnp.where(before, 1.0, 0.0).astype(BF16))
    base = cnt_sc[...]
    tot = base + prefix
    ranks = [jnp.sum(jnp.where(s, tot, 0.0), axis=0, keepdims=True) for s in sels]
    rank_out[...] = jnp.concatenate(ranks, axis=0).astype(I32)
    cnt = base + jnp.sum(maskf, axis=1, keepdims=True)
    cnt_sc[...] = cnt
    cnt_out[...] = jnp.broadcast_to(cnt, cnt_out.shape)


def _postattn(x, mla, sg, wo1, wo2, gpost, gffn, wrt, brt):
    n = x.shape[0]
    tok = lambda w: pl.BlockSpec((TM, w), lambda i: (i, 0))
    tokt = pl.BlockSpec((TOP_K, TM), lambda i: (0, i))
    full = lambda a: pl.BlockSpec(a.shape, lambda i: (0,) * a.ndim)
    consts = (wo1, wo2, gpost, gffn, wrt, brt)
    return pl.pallas_call(
        _postattn_kernel,
        grid=(n // TM,),
        in_specs=[tok(D_MODEL), tok(MLA_WIDTH), tok(SG_WIDTH)] + [full(a) for a in consts],
        out_specs=[tok(D_MODEL), tok(ROW_WORDS), tokt, tokt, tokt,
                   pl.BlockSpec((N_EXPERTS, LANES), lambda i: (0, 0))],
        out_shape=[jax.ShapeDtypeStruct((n, D_MODEL), F32),
                   jax.ShapeDtypeStruct((n, ROW_WORDS), I32),
                   jax.ShapeDtypeStruct((TOP_K, n), I32),
                   jax.ShapeDtypeStruct((TOP_K, n), F32),
                   jax.ShapeDtypeStruct((TOP_K, n), I32),
                   jax.ShapeDtypeStruct((N_EXPERTS, LANES), F32)],
        scratch_shapes=[pltpu.VMEM((N_EXPERTS, 1), F32)],
        compiler_params=pltpu.CompilerParams(dimension_semantics=("arbitrary",),
                                             vmem_limit_bytes=VMEM_LIMIT),
        name="postattn",
    )(x, mla, sg, *consts)


def _sc_workers():
    info = plsc.get_sparse_core_info()
    return info.num_cores, info.num_cores * info.num_subcores, info.num_lanes


def _sc_stream_rows(table_hbm, idx_v, out_hbm, base, n_chunks, buf, sem_g, sem_w):
    def gather(j, b):
        rows = idx_v.at[pl.ds(pl.multiple_of(j * SC_CHUNK, SC_CHUNK), SC_CHUNK)]
        return pltpu.make_async_copy(table_hbm.at[rows], buf.at[b], sem_g.at[b])

    def write(j, b):
        rows = pl.ds(pl.multiple_of(base + j * SC_CHUNK, SC_CHUNK), SC_CHUNK)
        return pltpu.make_async_copy(buf.at[b], out_hbm.at[rows], sem_w.at[b])

    gather(0, 0).start()

    @pl.loop(0, n_chunks, step=2)
    def _(j0):
        for b in range(2):
            j = j0 + b
            gather(j, b).wait()

            @pl.when(j >= 1)
            def _():
                write(j - 1, 1 - b).wait()

            @pl.when(j + 1 < n_chunks)
            def _():
                gather(j + 1, 1 - b).start()

            write(j, b).start()

    write(n_chunks - 1, 1).wait()


def _sc_gather(table, idx):
    n_out = idx.shape[0]
    width = table.shape[1]
    num_cores, workers, _ = _sc_workers()
    per_w = n_out // workers
    n_chunks = per_w // SC_CHUNK
    assert per_w * workers == n_out and n_chunks * SC_CHUNK == per_w and n_chunks % 2 == 0

    @functools.partial(
        pl.kernel, mesh=plsc.VectorSubcoreMesh(core_axis_name="c", subcore_axis_name="s"),
        out_type=jax.ShapeDtypeStruct((n_out, width), table.dtype),
        scratch_types=[pltpu.VMEM((per_w,), I32), pltpu.VMEM((2, SC_CHUNK, width), table.dtype),
                       pltpu.SemaphoreType.DMA((2,)), pltpu.SemaphoreType.DMA((2,))])
    def gather_kernel(table_hbm, idx_hbm, out_hbm, idx_v, buf, sem_g, sem_w):
        wid = lax.axis_index("s") * num_cores + lax.axis_index("c")
        base = pl.multiple_of(wid * per_w, SC_CHUNK)
        pltpu.sync_copy(idx_hbm.at[pl.ds(base, per_w)], idx_v)
        _sc_stream_rows(table_hbm, idx_v, out_hbm, base, n_chunks, buf, sem_g, sem_w)

    return gather_kernel(table, idx)


def _sc_dispatch(table, dest, n_out):
    n_assign = dest.shape[0]
    n, width = table.shape
    num_cores, workers, lanes = _sc_workers()
    per_w = n_out // workers
    n_chunks = per_w // SC_CHUNK
    assert per_w * workers == n_out and n_chunks * SC_CHUNK == per_w and n_chunks % 2 == 0
    assert n_assign % SC_DEST_CHUNK == 0 and SC_DEST_CHUNK % lanes == 0 and per_w % lanes == 0

    @functools.partial(
        pl.kernel, mesh=plsc.VectorSubcoreMesh(core_axis_name="c", subcore_axis_name="s"),
        out_type=jax.ShapeDtypeStruct((n_out, width), table.dtype),
        scratch_types=[pltpu.VMEM((per_w,), I32), pltpu.VMEM((SC_DEST_CHUNK,), I32),
                       pltpu.VMEM((2, SC_CHUNK, width), table.dtype),
                       pltpu.SemaphoreType.DMA((2,)), pltpu.SemaphoreType.DMA((2,))],
        compiler_params=pltpu.CompilerParams(needs_layout_passes=False))
    def dispatch_kernel(table_hbm, dest_hbm, out_hbm, tok_v, dest_v, buf, sem_g, sem_w):
        wid = lax.axis_index("s") * num_cores + lax.axis_index("c")
        base = pl.multiple_of(wid * per_w, SC_CHUNK)
        lane = lax.iota(I32, lanes)

        @pl.loop(0, per_w, step=lanes)
        def _(r):
            tok_v[pl.ds(r, lanes)] = lax.rem(base + r + lane, n)

        @pl.loop(0, n_assign // SC_DEST_CHUNK)
        def _(c):
            first = pl.multiple_of(c * SC_DEST_CHUNK, SC_DEST_CHUNK)
            pltpu.sync_copy(dest_hbm.at[pl.ds(first, SC_DEST_CHUNK)], dest_v)

            @plsc.parallel_loop(0, SC_DEST_CHUNK, step=lanes, unroll=4)
            def _(i):
                local = dest_v[pl.ds(i, lanes)] - base
                mine = (local >= 0) & (local < per_w)
                tok = lax.rem(first + i + lane, n)
                plsc.store_scatter(tok_v, [jnp.where(mine, local, 0)], tok, mask=mine)

        _sc_stream_rows(table_hbm, tok_v, out_hbm, base, n_chunks, buf, sem_g, sem_w)

    return dispatch_kernel(table, dest)


def _moe_kernel(nb_ref, row0_ref, x_hbm, wgu_hbm, bgu_ref, wd_hbm, bd_ref, y_hbm,
                wgu_f, wd_f, wgu_sc, wd_sc, xbuf, ybuf, sem_x, sem_y, sem_w):
    e = pl.program_id(0)
    n_e = pl.num_programs(0)
    nb = nb_ref[e]
    row0 = row0_ref[e]
    wslot = e % 2
    chunk_rows = D_MODEL // MOE_W_CHUNKS

    def w_copies(expert, c, slot):
        r = pl.ds(c * chunk_rows, chunk_rows)
        return (pltpu.make_async_copy(wgu_hbm.at[expert, r, :], wgu_f.at[slot, r, :], sem_w.at[slot]),
                pltpu.make_async_copy(wd_hbm.at[expert, r, :], wd_f.at[slot, r, :], sem_w.at[slot]))

    def start_chunk(expert, c, slot):
        for cp in w_copies(expert, c, slot):
            cp.start()

    def rows(j):
        return pl.ds(pl.multiple_of(row0 + j * MOE_BM, MOE_BM), MOE_BM)

    def x_copy(j, slot):
        return pltpu.make_async_copy(x_hbm.at[rows(j)], xbuf.at[slot], sem_x.at[slot])

    def y_copy(j, slot):
        return pltpu.make_async_copy(ybuf.at[slot], y_hbm.at[rows(j)], sem_y.at[slot])

    @pl.when(e == 0)
    def _():
        for c in range(MOE_W_CHUNKS):
            start_chunk(0, c, 0)

    @pl.when(nb > 0)
    def _():
        x_copy(0, 0).start()

    for c in range(MOE_W_CHUNKS):
        for cp in w_copies(e, c, wslot):
            cp.wait()
    wgu_sc[...] = wgu_f[wslot].astype(BF16)
    wd_sc[...] = wd_f[wslot].astype(BF16)
    nxt = jnp.minimum(e + 1, n_e - 1)
    has_next = e + 1 < n_e

    def body(j, carry):
        slot = j % 2

        @pl.when(has_next & (j < MOE_W_CHUNKS))
        def _():
            start_chunk(nxt, j, 1 - wslot)

        x_copy(j, slot).wait()

        @pl.when(j + 1 < nb)
        def _():
            x_copy(j + 1, 1 - slot).start()

        @pl.when(j >= 2)
        def _():
            y_copy(j - 2, slot).wait()

        x = jnp.concatenate(_unpack_halves(xbuf[slot]), axis=1).astype(BF16)
        gu = _dot(x, wgu_sc[...]) + bgu_ref[0]
        g = jnp.minimum(gu[:, :D_MODEL], SWIGLU_LIMIT)
        u = jnp.clip(gu[:, D_MODEL:], -SWIGLU_LIMIT, SWIGLU_LIMIT)
        act = (u + 1.0) * (g * jax.nn.sigmoid(SWIGLU_ALPHA * g))
        ybuf[slot] = _pack_halves(_dot(act.astype(BF16), wd_sc[...]) + bd_ref[0])
        y_copy(j, slot).start()
        return carry

    lax.fori_loop(0, nb, body, 0)

    def rest(c, carry):
        @pl.when(has_next)
        def _():
            start_chunk(nxt, c, 1 - wslot)
        return carry

    lax.fori_loop(jnp.minimum(nb, MOE_W_CHUNKS), MOE_W_CHUNKS, rest, 0)

    @pl.when(nb >= 2)
    def _():
        y_copy(nb - 2, nb % 2).wait()

    @pl.when(nb >= 1)
    def _():
        y_copy(nb - 1, (nb - 1) % 2).wait()


def _moe(blocks_of, row0_of, x_rows, wgu, bgu, wd, bd):
    hbm = pl.BlockSpec(memory_space=pl.ANY)
    grid_spec = pltpu.PrefetchScalarGridSpec(
        num_scalar_prefetch=2,
        grid=(N_EXPERTS,),
        in_specs=[hbm, hbm,
                  pl.BlockSpec((1, 1, 2 * D_MODEL), lambda e, nb, r0: (e, 0, 0)),
                  hbm,
                  pl.BlockSpec((1, 1, D_MODEL), lambda e, nb, r0: (e, 0, 0))],
        out_specs=hbm,
        scratch_shapes=[pltpu.VMEM((2, D_MODEL, 2 * D_MODEL), F32), pltpu.VMEM((2, D_MODEL, D_MODEL), F32),
                        pltpu.VMEM((D_MODEL, 2 * D_MODEL), BF16), pltpu.VMEM((D_MODEL, D_MODEL), BF16),
                        pltpu.VMEM((2, MOE_BM, ROW_WORDS), I32), pltpu.VMEM((2, MOE_BM, ROW_WORDS), I32),
                        pltpu.SemaphoreType.DMA((2,)), pltpu.SemaphoreType.DMA((2,)),
                        pltpu.SemaphoreType.DMA((2,))],
    )
    return pl.pallas_call(
        _moe_kernel,
        grid_spec=grid_spec,
        out_shape=jax.ShapeDtypeStruct(x_rows.shape, I32),
        compiler_params=pltpu.CompilerParams(dimension_semantics=("arbitrary",),
                                             vmem_limit_bytes=VMEM_LIMIT),
        name="moe",
    )(blocks_of, row0_of, x_rows, wgu, bgu, wd, bd)


def _final_kernel(x1_ref, yg_ref, gt_ref, p_ref, gpost_ref, wg_ref, bg_ref, wp_ref, gple_ref, o_ref):
    gt = gt_ref[...]
    lo, hi = _unpack_halves(yg_ref[0])
    y = gt[:, 0:1] * jnp.concatenate([lo, hi], axis=1)
    for k in range(1, TOP_K):
        lo, hi = _unpack_halves(yg_ref[k])
        y = y + gt[:, k:k + 1] * jnp.concatenate([lo, hi], axis=1)
    x2 = x1_ref[...] + _rms(y, gpost_ref[...])
    gate = jax.nn.sigmoid(_dot(x2.astype(BF16), wg_ref[...]) + bg_ref[...])
    pp = _dot(p_ref[...].astype(BF16), wp_ref[...])
    o_ref[...] = x2 + _rms(gate * pp, gple_ref[...])


def _final(x1, yg, gate_t, p, gpost, wg, bg, wp, gple):
    n = x1.shape[0]
    tok = lambda w: pl.BlockSpec((TM, w), lambda i: (i, 0))
    full = lambda a: pl.BlockSpec(a.shape, lambda i: (0,) * a.ndim)
    consts = (gpost, wg, bg, wp, gple)
    return pl.pallas_call(
        _final_kernel,
        grid=(n // TM,),
        in_specs=[tok(D_MODEL), pl.BlockSpec((TOP_K, TM, ROW_WORDS), lambda i: (0, i, 0)), tok(TOP_K),
                  tok(D_PLE)] + [full(a) for a in consts],
        out_specs=tok(D_MODEL),
        out_shape=jax.ShapeDtypeStruct((n, D_MODEL), F32),
        compiler_params=pltpu.CompilerParams(dimension_semantics=("arbitrary",),
                                             vmem_limit_bytes=VMEM_LIMIT),
        name="final",
    )(x1, yg, gate_t, p, *consts)


def _rope_angles(positions):
    inv_freq = 1.0 / (ROPE_THETA ** (jnp.arange(0, QK_ROPE, 2, dtype=F32) / QK_ROPE))
    return inv_freq[:, None] * positions.astype(F32).reshape(1, -1)


def _pad_head(w):
    return jnp.pad(w, [(0, 0)] * (w.ndim - 1) + [(0, HEAD_PAD - w.shape[-1])])


def _layer(x, p_l, ang, prm):
    (attn_pre_g, w_in, q_norm_g, w_uq, kv_norm_g, w_ukv, sg_norm_g, w_spatial, b_spatial, mla_out_g,
     sg_out_g, w_out, attn_post_g, ffn_pre_g, w_router, b_router, w_gate_up, b_gate_up, w_down, b_down,
     ffn_post_g, w_ple_gate, b_ple_gate, w_ple_proj, ple_norm_g) = prm
    b, s, _ = x.shape
    n = b * s
    xt = x.reshape(n, D_MODEL)
    row2 = lambda a: a.reshape(1, -1)

    w_kr = w_in[:, Q_LORA + KV_LORA:Q_LORA + KV_LORA + QK_ROPE]
    place = lambda w: jnp.pad(w, ((0, 0), (QK_NOPE, HEAD_PAD - QK_NOPE - QK_ROPE)))
    win_aug = jnp.concatenate(
        [w_in[:, :Q_LORA + KV_LORA], place(w_kr), w_in[:, Q_LORA + KV_LORA + QK_ROPE:]], axis=-1).astype(BF16)
    wq_aug = _pad_head(w_uq).reshape(Q_LORA, -1).astype(BF16)
    wk_pad = _pad_head(w_ukv[..., :QK_NOPE]).reshape(KV_LORA, -1).astype(BF16)
    wv_t = w_ukv[..., QK_NOPE:].reshape(KV_LORA, -1).T.astype(BF16)
    head_of = jnp.arange(SG_WIDTH) // SG_HEAD_DIM
    gsum = (head_of[:, None] == head_of[None, :]).astype(BF16)
    wcat = w_spatial.transpose(1, 0, 2).reshape(SG_CHUNK, SG_HEADS * SG_CHUNK)
    bsp = jnp.repeat(b_spatial.T, SG_HEAD_DIM, axis=1)

    q, k, vt, mix_sg = _inproj(xt, ang, row2(attn_pre_g), win_aug, row2(q_norm_g), wq_aug,
                               row2(kv_norm_g), wk_pad, wv_t, row2(sg_norm_g), gsum, wcat, bsp,
                               row2(sg_out_g))
    mix_mla = _attention(q, k, vt, row2(mla_out_g), b, s)

    w_out_b = w_out.astype(BF16)
    x1, h2, idx, gate, rank, cnt = _postattn(
        xt, mix_mla, mix_sg, w_out_b[:MLA_WIDTH], w_out_b[MLA_WIDTH:], row2(attn_post_g), row2(ffn_pre_g),
        w_router.T.astype(BF16), b_router.reshape(N_EXPERTS, 1))

    a = n * TOP_K
    counts = cnt[:, 0].astype(I32)
    padded = (counts + MOE_BM - 1) // MOE_BM * MOE_BM
    pad_end = jnp.cumsum(padded)
    pad_start = pad_end - padded
    start_of = jnp.sum(jnp.where(idx[..., None] == jnp.arange(N_EXPERTS, dtype=I32), pad_start, 0), axis=-1)
    dest = (start_of + rank).reshape(a)
    n_rows = a + N_EXPERTS * MOE_BM

    x_rows = _sc_dispatch(h2, dest, n_rows)
    y_rows = _moe(padded // MOE_BM, pad_start, x_rows, w_gate_up, b_gate_up[:, None, :],
                  w_down, b_down[:, None, :])
    yg = _sc_gather(y_rows, dest).reshape(TOP_K, n, ROW_WORDS)

    out = _final(x1, yg, gate.T, p_l.reshape(n, D_PLE), row2(ffn_post_g), w_ple_gate.astype(BF16),
                 row2(b_ple_gate), w_ple_proj.astype(BF16), row2(ple_norm_g))
    return out.reshape(b, s, D_MODEL)


def kernel(x, p, positions, attn_pre_g, w_in, q_norm_g, w_uq, kv_norm_g, w_ukv, sg_norm_g, w_spatial, b_spatial, mla_out_g, sg_out_g, w_out, attn_post_g, ffn_pre_g, w_router, b_router, w_gate_up, b_gate_up, w_down, b_down, ffn_post_g, w_ple_gate, b_ple_gate, w_ple_proj, ple_norm_g):
    ang = _rope_angles(positions)
    params = (attn_pre_g, w_in, q_norm_g, w_uq, kv_norm_g, w_ukv, sg_norm_g, w_spatial, b_spatial, mla_out_g,
              sg_out_g, w_out, attn_post_g, ffn_pre_g, w_router, b_router, w_gate_up, b_gate_up, w_down, b_down,
              ffn_post_g, w_ple_gate, b_ple_gate, w_ple_proj, ple_norm_g)
    for layer in range(p.shape[0]):
        x = _layer(x, p[layer], ang, tuple(a[layer] for a in params))
    return x
```

```python
import functools

import jax
import jax.numpy as jnp
from jax import lax
from jax.experimental import pallas as pl
from jax.experimental.pallas import tpu as pltpu
from jax.experimental.pallas import tpu_sc as plsc

F32 = jnp.float32
BF16 = jnp.bfloat16
I32 = jnp.int32

D_MODEL = 1024
HEADS = 8
QK_NOPE = 64
QK_ROPE = 32
V_HEAD = 64
Q_LORA = 256
KV_LORA = 128
ROPE_THETA = 10000.0
SG_HEADS = 8
SG_HEAD_DIM = 64
SG_CHUNK = 128
SG_WIDTH = SG_HEADS * SG_HEAD_DIM
MLA_WIDTH = HEADS * V_HEAD
N_EXPERTS = 32
TOP_K = 4
SWIGLU_LIMIT = 7.0
SWIGLU_ALPHA = 1.702
D_PLE = 256
EPS = 1e-6

LANES = 128
HEAD_PAD = LANES
QK_SCALE = (QK_NOPE + QK_ROPE) ** -0.5
LOG2E = 1.4426950408889634

COL_CQ = 0
COL_CKV = COL_CQ + Q_LORA
COL_KR = COL_CKV + KV_LORA
COL_SG = COL_KR + LANES
IN_COLS_AUG = COL_SG + 2 * SG_WIDTH

TM = 1024
BQ = 512
ATTN_BK = 1024
ATTN_ONES_ROWS = 16
MOE_BM = 256
MOE_W_CHUNKS = 8
VMEM_LIMIT = 56 * 1024 * 1024

NT_DIMS = (((1,), (1,)), ((), ()))
ROW_WORDS = D_MODEL // 2
SC_CHUNK = 64
SC_DEST_CHUNK = 8192


def _rms(x, g):
    return x * lax.rsqrt(jnp.mean(x * x, axis=-1, keepdims=True) + EPS) * g


def _dot(a, b):
    return jnp.dot(a, b, preferred_element_type=F32)


def _dot_nt(a, b):
    return lax.dot_general(a, b, NT_DIMS, preferred_element_type=F32)


def _pack_halves(x):
    half = x.shape[1] // 2
    return pltpu.pack_elementwise([x[:, :half], x[:, half:]], packed_dtype=BF16)


def _unpack_halves(w):
    return (pltpu.unpack_elementwise(w, index=0, packed_dtype=BF16, unpacked_dtype=F32),
            pltpu.unpack_elementwise(w, index=1, packed_dtype=BF16, unpacked_dtype=F32))


def _inproj_kernel(x_ref, ang_ref, gpre_ref, win_ref, qg_ref, wq_ref, kvg_ref, wk_ref, wvt_ref,
                   sgg_ref, gsum_ref, wcat_ref, bsp_ref, sgo_ref,
                   q_out, k_out, vt_out, sg_out):
    x = x_ref[...]
    tm = x.shape[0]
    h = _rms(x, gpre_ref[...])
    z = _dot(h.astype(BF16), win_ref[...])

    ang = ang_ref[...]
    c, s = jnp.cos(ang), jnp.sin(ang)
    tail = HEAD_PAD - QK_NOPE - QK_ROPE
    cos = jnp.concatenate([jnp.ones((QK_NOPE, tm), F32), c, c, jnp.ones((tail, tm), F32)], axis=0).T
    sin = jnp.concatenate([jnp.zeros((QK_NOPE, tm), F32), s, s, jnp.zeros((tail, tm), F32)], axis=0).T
    first_half = lax.broadcasted_iota(I32, (tm, HEAD_PAD), 1) < QK_NOPE + QK_ROPE // 2
    sin_signed = jnp.where(first_half, -sin, sin)

    def rope(t):
        partner = jnp.where(first_half, pltpu.roll(t, HEAD_PAD - QK_ROPE // 2, 1), pltpu.roll(t, QK_ROPE // 2, 1))
        return t * cos + partner * sin_signed

    cqn = _rms(z[:, COL_CQ:COL_CQ + Q_LORA], qg_ref[...])
    qq = _dot(cqn.astype(BF16), wq_ref[...])
    for hd in range(HEADS):
        sl = slice(hd * HEAD_PAD, (hd + 1) * HEAD_PAD)
        q_out[:, sl] = (rope(qq[:, sl]) * (QK_SCALE * LOG2E)).astype(BF16)

    ckvn = _rms(z[:, COL_CKV:COL_CKV + KV_LORA], kvg_ref[...]).astype(BF16)
    kk = _dot(ckvn, wk_ref[...])
    kr = rope(z[:, COL_KR:COL_KR + LANES])
    for hd in range(HEADS):
        sl = slice(hd * HEAD_PAD, (hd + 1) * HEAD_PAD)
        k_out[:, sl] = (kk[:, sl] + kr).astype(BF16)
    vt_out[...] = _dot_nt(wvt_ref[...], ckvn).astype(BF16)

    zg = jax.nn.gelu(z[:, COL_SG:COL_SG + 2 * SG_WIDTH])
    u = zg[:, :SG_WIDTH]
    v = zg[:, SG_WIDTH:]
    ms = _dot((v * v).astype(BF16), gsum_ref[...]) * (1.0 / SG_HEAD_DIM)
    vn = v * lax.rsqrt(ms + EPS) * sgg_ref[...]

    row = lax.broadcasted_iota(I32, (SG_CHUNK, SG_HEADS * SG_CHUNK), 0)
    col = lax.broadcasted_iota(I32, (SG_CHUNK, SG_HEADS * SG_CHUNK), 1)
    wcat = jnp.where((col % SG_CHUNK) <= row, wcat_ref[...], 0.0).astype(BF16)
    lane_head = lax.broadcasted_iota(I32, (SG_CHUNK, SG_WIDTH), 1) // SG_HEAD_DIM
    bsp = bsp_ref[...]
    sgo = sgo_ref[...]
    for c in range(x.shape[0] // SG_CHUNK):
        rows = slice(c * SG_CHUNK, (c + 1) * SG_CHUNK)
        vc = vn[rows]
        vbd = jnp.concatenate(
            [jnp.where(lane_head == hd, vc, 0.0).astype(BF16) for hd in range(SG_HEADS)], axis=0)
        vm = _dot(wcat, vbd) + bsp
        sg_out[rows, :] = _rms(u[rows] * vm, sgo).astype(BF16)


def _inproj(x, ang, gpre, win, qg, wq, kvg, wk, wvt, sgg, gsum, wcat, bsp, sgo):
    n = x.shape[0]
    tok = lambda w: pl.BlockSpec((TM, w), lambda i: (i, 0))
    full = lambda a: pl.BlockSpec(a.shape, lambda i: (0,) * a.ndim)
    consts = (gpre, win, qg, wq, kvg, wk, wvt, sgg, gsum, wcat, bsp, sgo)
    return pl.pallas_call(
        _inproj_kernel,
        grid=(n // TM,),
        in_specs=[tok(D_MODEL), pl.BlockSpec((QK_ROPE // 2, TM), lambda i: (0, i))] + [full(a) for a in consts],
        out_specs=[tok(HEADS * HEAD_PAD), tok(HEADS * HEAD_PAD),
                   pl.BlockSpec((MLA_WIDTH, TM), lambda i: (0, i)), tok(SG_WIDTH)],
        out_shape=[jax.ShapeDtypeStruct((n, HEADS * HEAD_PAD), BF16),
                   jax.ShapeDtypeStruct((n, HEADS * HEAD_PAD), BF16),
                   jax.ShapeDtypeStruct((MLA_WIDTH, n), BF16),
                   jax.ShapeDtypeStruct((n, SG_WIDTH), BF16)],
        compiler_params=pltpu.CompilerParams(dimension_semantics=("arbitrary",),
                                             vmem_limit_bytes=VMEM_LIMIT),
        name="inproj",
    )(x, ang, *consts)


def _attn_kernel(q_ref, k_ref, vt_ref, g_ref, o_ref, m_sc, l_sc, acc_sc):
    i = pl.program_id(1)
    bq = q_ref.shape[0]
    m_sc[...] = jnp.full(m_sc.shape, -jnp.inf, F32)
    l_sc[...] = jnp.zeros(l_sc.shape, F32)
    acc_sc[...] = jnp.zeros(acc_sc.shape, F32)

    def qk(start, nk, hd):
        hs = slice(hd * HEAD_PAD, (hd + 1) * HEAD_PAD)
        return _dot_nt(k_ref[pl.ds(start, nk), hs], q_ref[:, hs])

    def block(start, nk, masked):
        ones = jnp.ones((ATTN_ONES_ROWS, nk), BF16)
        queue = [qk(start, nk, 0), qk(start, nk, 1)]
        for hd in range(HEADS):
            vs = slice(hd * V_HEAD, (hd + 1) * V_HEAD)
            st = queue.pop(0)
            if hd + 2 < HEADS:
                queue.append(qk(start, nk, hd + 2))
            if masked:
                causal = (lax.broadcasted_iota(I32, (nk, bq), 0) <= lax.broadcasted_iota(I32, (nk, bq), 1))
                st = jnp.where(causal, st, -jnp.inf)
            m_prev = m_sc[hd:hd + 1, :]
            m_new = jnp.maximum(m_prev, jnp.max(st, axis=0, keepdims=True))
            alpha = jnp.exp2(m_prev - m_new)
            p = jnp.exp2(st - m_new).astype(BF16)
            vta = jnp.concatenate([vt_ref[vs, pl.ds(start, nk)], ones], axis=0)
            pv = _dot(vta, p)
            l_sc[hd:hd + 1, :] = alpha * l_sc[hd:hd + 1, :] + pv[V_HEAD:V_HEAD + 1, :]
            acc_sc[vs, :] = alpha * acc_sc[vs, :] + pv[:V_HEAD, :]
            m_sc[hd:hd + 1, :] = m_new

    def body(kb, carry):
        block(pl.multiple_of(kb * ATTN_BK, ATTN_BK), ATTN_BK, False)
        return carry

    visible = i * bq
    lax.fori_loop(0, visible // ATTN_BK, body, 0)
    for r in range(ATTN_BK // bq - 1, 0, -1):

        @pl.when(visible % ATTN_BK >= r * bq)
        def _():
            block(pl.multiple_of((i - r) * bq, bq), bq, False)

    block(pl.multiple_of(i * bq, bq), bq, True)
    ot = jnp.concatenate(
        [acc_sc[hd * V_HEAD:(hd + 1) * V_HEAD, :] * (1.0 / l_sc[hd:hd + 1, :]) for hd in range(HEADS)], axis=0)
    o_ref[...] = _rms(ot.T, g_ref[...]).astype(BF16)


def _attention(q, k, vt, g, b, s):
    nq = s // BQ
    return pl.pallas_call(
        _attn_kernel,
        grid=(b, nq),
        in_specs=[pl.BlockSpec((BQ, HEADS * HEAD_PAD), lambda bi, i: (bi * nq + i, 0)),
                  pl.BlockSpec((s, HEADS * HEAD_PAD), lambda bi, i: (bi, 0), pipeline_mode=pl.Buffered(1)),
                  pl.BlockSpec((MLA_WIDTH, s), lambda bi, i: (0, bi), pipeline_mode=pl.Buffered(1)),
                  pl.BlockSpec(g.shape, lambda bi, i: (0, 0))],
        out_specs=pl.BlockSpec((BQ, MLA_WIDTH), lambda bi, i: (bi * nq + i, 0)),
        out_shape=jax.ShapeDtypeStruct((b * s, MLA_WIDTH), BF16),
        scratch_shapes=[pltpu.VMEM((HEADS, BQ), F32), pltpu.VMEM((HEADS, BQ), F32),
                        pltpu.VMEM((MLA_WIDTH, BQ), F32)],
        compiler_params=pltpu.CompilerParams(dimension_semantics=("arbitrary", "arbitrary"),
                                             vmem_limit_bytes=VMEM_LIMIT),
        name="attention",
    )(q, k, vt, g)


def _postattn_kernel(x_ref, mla_ref, sg_ref, wo1_ref, wo2_ref, gpost_ref, gffn_ref, wrt_ref, brt_ref,
                     x1_out, h2_out, idx_out, gate_out, rank_out, cnt_out, cnt_sc):
    i = pl.program_id(0)

    @pl.when(i == 0)
    def _():
        cnt_sc[...] = jnp.zeros(cnt_sc.shape, F32)

    a = _dot(mla_ref[...], wo1_ref[...]) + _dot(sg_ref[...], wo2_ref[...])
    x1 = x_ref[...] + _rms(a, gpost_ref[...])
    x1_out[...] = x1
    h2f = _rms(x1, gffn_ref[...])
    h2_out[...] = _pack_halves(h2f)
    h2 = h2f.astype(BF16)

    tm = h2.shape[0]
    logits = _dot_nt(wrt_ref[...], h2) + brt_ref[...]
    eidx = lax.broadcasted_iota(I32, (N_EXPERTS, tm), 0)
    vals, idxs, sels = [], [], []
    for _ in range(TOP_K):
        mx = jnp.max(logits, axis=0, keepdims=True)
        ik = jnp.min(jnp.where(logits == mx, eidx, N_EXPERTS), axis=0, keepdims=True)
        sel = eidx == ik
        vals.append(mx)
        idxs.append(ik)
        sels.append(sel)
        logits = jnp.where(sel, -jnp.inf, logits)
    ex = [jnp.exp(v - vals[0]) for v in vals]
    den = ex[0] + ex[1] + ex[2] + ex[3]
    gate_out[...] = jnp.concatenate([e / den for e in ex], axis=0)
    idx_out[...] = jnp.concatenate(idxs, axis=0)

    maskf = sum(jnp.where(s, 1.0, 0.0) for s in sels)
    before = (lax.broadcasted_iota(I32, (tm, tm), 0) < lax.broadcasted_iota(I32, (tm, tm), 1))
    prefix = _dot(maskf.astype(BF16), jnp.where(before, 1.0, 0.0).astype(BF16))
    base = cnt_sc[...]
    tot = base + prefix
    ranks = [jnp.sum(jnp.where(s, tot, 0.0), axis=0, keepdims=True) for s in sels]
    rank_out[...] = jnp.concatenate(ranks, axis=0).astype(I32)
    cnt = base + jnp.sum(maskf, axis=1, keepdims=True)
    cnt_sc[...] = cnt
    cnt_out[...] = jnp.broadcast_to(cnt, cnt_out.shape)


def _postattn(x, mla, sg, wo1, wo2, gpost, gffn, wrt, brt):
    n = x.shape[0]
    tok = lambda w: pl.BlockSpec((TM, w), lambda i: (i, 0))
    tokt = pl.BlockSpec((TOP_K, TM), lambda i: (0, i))
    full = lambda a: pl.BlockSpec(a.shape, lambda i: (0,) * a.ndim)
    consts = (wo1, wo2, gpost, gffn, wrt, brt)
    return pl.pallas_call(
        _postattn_kernel,
        grid=(n // TM,),
        in_specs=[tok(D_MODEL), tok(MLA_WIDTH), tok(SG_WIDTH)] + [full(a) for a in consts],
        out_specs=[tok(D_MODEL), tok(ROW_WORDS), tokt, tokt, tokt,
                   pl.BlockSpec((N_EXPERTS, LANES), lambda i: (0, 0))],
        out_shape=[jax.ShapeDtypeStruct((n, D_MODEL), F32),
                   jax.ShapeDtypeStruct((n, ROW_WORDS), I32),
                   jax.ShapeDtypeStruct((TOP_K, n), I32),
                   jax.ShapeDtypeStruct((TOP_K, n), F32),
                   jax.ShapeDtypeStruct((TOP_K, n), I32),
                   jax.ShapeDtypeStruct((N_EXPERTS, LANES), F32)],
        scratch_shapes=[pltpu.VMEM((N_EXPERTS, 1), F32)],
        compiler_params=pltpu.CompilerParams(dimension_semantics=("arbitrary",),
                                             vmem_limit_bytes=VMEM_LIMIT),
        name="postattn",
    )(x, mla, sg, *consts)


def _sc_workers():
    info = plsc.get_sparse_core_info()
    return info.num_cores, info.num_cores * info.num_subcores, info.num_lanes


def _sc_stream_rows(table_hbm, idx_v, out_hbm, base, n_chunks, buf, sem_g, sem_w):
    def gather(j, b):
        rows = idx_v.at[pl.ds(pl.multiple_of(j * SC_CHUNK, SC_CHUNK), SC_CHUNK)]
        return pltpu.make_async_copy(table_hbm.at[rows], buf.at[b], sem_g.at[b])

    def write(j, b):
        rows = pl.ds(pl.multiple_of(base + j * SC_CHUNK, SC_CHUNK), SC_CHUNK)
        return pltpu.make_async_copy(buf.at[b], out_hbm.at[rows], sem_w.at[b])

    gather(0, 0).start()

    @pl.loop(0, n_chunks, step=2)
    def _(j0):
        for b in range(2):
            j = j0 + b
            gather(j, b).wait()

            @pl.when(j >= 1)
            def _():
                write(j - 1, 1 - b).wait()

            @pl.when(j + 1 < n_chunks)
            def _():
                gather(j + 1, 1 - b).start()

            write(j, b).start()

    write(n_chunks - 1, 1).wait()


def _sc_gather(table, idx):
    n_out = idx.shape[0]
    width = table.shape[1]
    num_cores, workers, _ = _sc_workers()
    per_w = n_out // workers
    n_chunks = per_w // SC_CHUNK
    assert per_w * workers == n_out and n_chunks * SC_CHUNK == per_w and n_chunks % 2 == 0

    @functools.partial(
        pl.kernel, mesh=plsc.VectorSubcoreMesh(core_axis_name="c", subcore_axis_name="s"),
        out_type=jax.ShapeDtypeStruct((n_out, width), table.dtype),
        scratch_types=[pltpu.VMEM((per_w,), I32), pltpu.VMEM((2, SC_CHUNK, width), table.dtype),
                       pltpu.SemaphoreType.DMA((2,)), pltpu.SemaphoreType.DMA((2,))])
    def gather_kernel(table_hbm, idx_hbm, out_hbm, idx_v, buf, sem_g, sem_w):
        wid = lax.axis_index("s") * num_cores + lax.axis_index("c")
        base = pl.multiple_of(wid * per_w, SC_CHUNK)
        pltpu.sync_copy(idx_hbm.at[pl.ds(base, per_w)], idx_v)
        _sc_stream_rows(table_hbm, idx_v, out_hbm, base, n_chunks, buf, sem_g, sem_w)

    return gather_kernel(table, idx)


def _sc_dispatch(table, dest, n_out):
    n_assign = dest.shape[0]
    n, width = table.shape
    num_cores, workers, lanes = _sc_workers()
    per_w = n_out // workers
    n_chunks = per_w // SC_CHUNK
    assert per_w * workers == n_out and n_chunks * SC_CHUNK == per_w and n_chunks % 2 == 0
    assert n_assign % SC_DEST_CHUNK == 0 and SC_DEST_CHUNK % lanes == 0 and per_w % lanes == 0

    @functools.partial(
        pl.kernel, mesh=plsc.VectorSubcoreMesh(core_axis_name="c", subcore_axis_name="s"),
        out_type=jax.ShapeDtypeStruct((n_out, width), table.dtype),
        scratch_types=[pltpu.VMEM((per_w,), I32), pltpu.VMEM((SC_DEST_CHUNK,), I32),
                       pltpu.VMEM((2, SC_CHUNK, width), table.dtype),
                       pltpu.SemaphoreType.DMA((2,)), pltpu.SemaphoreType.DMA((2,))],
        compiler_params=pltpu.CompilerParams(needs_layout_passes=False))
    def dispatch_kernel(table_hbm, dest_hbm, out_hbm, tok_v, dest_v, buf, sem_g, sem_w):
        wid = lax.axis_index("s") * num_cores + lax.axis_index("c")
        base = pl.multiple_of(wid * per_w, SC_CHUNK)
        lane = lax.iota(I32, lanes)

        @pl.loop(0, per_w, step=lanes)
        def _(r):
            tok_v[pl.ds(r, lanes)] = lax.rem(base + r + lane, n)

        @pl.loop(0, n_assign // SC_DEST_CHUNK)
        def _(c):
            first = pl.multiple_of(c * SC_DEST_CHUNK, SC_DEST_CHUNK)
            pltpu.sync_copy(dest_hbm.at[pl.ds(first, SC_DEST_CHUNK)], dest_v)

            @plsc.parallel_loop(0, SC_DEST_CHUNK, step=lanes, unroll=4)
            def _(i):
                local = dest_v[pl.ds(i, lanes)] - base
                mine = (local >= 0) & (local < per_w)
                tok = lax.rem(first + i + lane, n)
                plsc.store_scatter(tok_v, [jnp.where(mine, local, 0)], tok, mask=mine)

        _sc_stream_rows(table_hbm, tok_v, out_hbm, base, n_chunks, buf, sem_g, sem_w)

    return dispatch_kernel(table, dest)


def _moe_kernel(nb_ref, row0_ref, x_hbm, wgu_hbm, bgu_ref, wd_hbm, bd_ref, y_hbm,
                wgu_f, wd_f, wgu_sc, wd_sc, xbuf, ybuf, sem_x, sem_y, sem_w):
    e = pl.program_id(0)
    n_e = pl.num_programs(0)
    nb = nb_ref[e]
    row0 = row0_ref[e]
    wslot = e % 2
    chunk_rows = D_MODEL // MOE_W_CHUNKS

    def w_copies(expert, c, slot):
        r = pl.ds(c * chunk_rows, chunk_rows)
        return (pltpu.make_async_copy(wgu_hbm.at[expert, r, :], wgu_f.at[slot, r, :], sem_w.at[slot]),
                pltpu.make_async_copy(wd_hbm.at[expert, r, :], wd_f.at[slot, r, :], sem_w.at[slot]))

    def start_chunk(expert, c, slot):
        for cp in w_copies(expert, c, slot):
            cp.start()

    def rows(j):
        return pl.ds(pl.multiple_of(row0 + j * MOE_BM, MOE_BM), MOE_BM)

    def x_copy(j, slot):
        return pltpu.make_async_copy(x_hbm.at[rows(j)], xbuf.at[slot], sem_x.at[slot])

    def y_copy(j, slot):
        return pltpu.make_async_copy(ybuf.at[slot], y_hbm.at[rows(j)], sem_y.at[slot])

    @pl.when(e == 0)
    def _():
        for c in range(MOE_W_CHUNKS):
            start_chunk(0, c, 0)

    @pl.when(nb > 0)
    def _():
        x_copy(0, 0).start()

    for c in range(MOE_W_CHUNKS):
        for cp in w_copies(e, c, wslot):
            cp.wait()
    wgu_sc[...] = wgu_f[wslot].astype(BF16)
    wd_sc[...] = wd_f[wslot].astype(BF16)
    nxt = jnp.minimum(e + 1, n_e - 1)
    has_next = e + 1 < n_e

    def body(j, carry):
        slot = j % 2

        @pl.when(has_next & (j < MOE_W_CHUNKS))
        def _():
            start_chunk(nxt, j, 1 - wslot)

        x_copy(j, slot).wait()

        @pl.when(j + 1 < nb)
        def _():
            x_copy(j + 1, 1 - slot).start()

        @pl.when(j >= 2)
        def _():
            y_copy(j - 2, slot).wait()

        x = jnp.concatenate(_unpack_halves(xbuf[slot]), axis=1).astype(BF16)
        gu = _dot(x, wgu_sc[...]) + bgu_ref[0]
        g = jnp.minimum(gu[:, :D_MODEL], SWIGLU_LIMIT)
        u = jnp.clip(gu[:, D_MODEL:], -SWIGLU_LIMIT, SWIGLU_LIMIT)
        act = (u + 1.0) * (g * jax.nn.sigmoid(SWIGLU_ALPHA * g))
        ybuf[slot] = _pack_halves(_dot(act.astype(BF16), wd_sc[...]) + bd_ref[0])
        y_copy(j, slot).start()
        return carry

    lax.fori_loop(0, nb, body, 0)

    def rest(c, carry):
        @pl.when(has_next)
        def _():
            start_chunk(nxt, c, 1 - wslot)
        return carry

    lax.fori_loop(jnp.minimum(nb, MOE_W_CHUNKS), MOE_W_CHUNKS, rest, 0)

    @pl.when(nb >= 2)
    def _():
        y_copy(nb - 2, nb % 2).wait()

    @pl.when(nb >= 1)
    def _():
        y_copy(nb - 1, (nb - 1) % 2).wait()


def _moe(blocks_of, row0_of, x_rows, wgu, bgu, wd, bd):
    hbm = pl.BlockSpec(memory_space=pl.ANY)
    grid_spec = pltpu.PrefetchScalarGridSpec(
        num_scalar_prefetch=2,
        grid=(N_EXPERTS,),
        in_specs=[hbm, hbm,
                  pl.BlockSpec((1, 1, 2 * D_MODEL), lambda e, nb, r0: (e, 0, 0)),
                  hbm,
                  pl.BlockSpec((1, 1, D_MODEL), lambda e, nb, r0: (e, 0, 0))],
        out_specs=hbm,
        scratch_shapes=[pltpu.VMEM((2, D_MODEL, 2 * D_MODEL), F32), pltpu.VMEM((2, D_MODEL, D_MODEL), F32),
                        pltpu.VMEM((D_MODEL, 2 * D_MODEL), BF16), pltpu.VMEM((D_MODEL, D_MODEL), BF16),
                        pltpu.VMEM((2, MOE_BM, ROW_WORDS), I32), pltpu.VMEM((2, MOE_BM, ROW_WORDS), I32),
                        pltpu.SemaphoreType.DMA((2,)), pltpu.SemaphoreType.DMA((2,)),
                        pltpu.SemaphoreType.DMA((2,))],
    )
    return pl.pallas_call(
        _moe_kernel,
        grid_spec=grid_spec,
        out_shape=jax.ShapeDtypeStruct(x_rows.shape, I32),
        compiler_params=pltpu.CompilerParams(dimension_semantics=("arbitrary",),
                                             vmem_limit_bytes=VMEM_LIMIT),
        name="moe",
    )(blocks_of, row0_of, x_rows, wgu, bgu, wd, bd)


def _final_kernel(x1_ref, yg_ref, gt_ref, p_ref, gpost_ref, wg_ref, bg_ref, wp_ref, gple_ref, o_ref):
    gt = gt_ref[...]
    lo, hi = _unpack_halves(yg_ref[0])
    y = gt[:, 0:1] * jnp.concatenate([lo, hi], axis=1)
    for k in range(1, TOP_K):
        lo, hi = _unpack_halves(yg_ref[k])
        y = y + gt[:, k:k + 1] * jnp.concatenate([lo, hi], axis=1)
    x2 = x1_ref[...] + _rms(y, gpost_ref[...])
    gate = jax.nn.sigmoid(_dot(x2.astype(BF16), wg_ref[...]) + bg_ref[...])
    pp = _dot(p_ref[...].astype(BF16), wp_ref[...])
    o_ref[...] = x2 + _rms(gate * pp, gple_ref[...])


def _final(x1, yg, gate_t, p, gpost, wg, bg, wp, gple):
    n = x1.shape[0]
    tok = lambda w: pl.BlockSpec((TM, w), lambda i: (i, 0))
    full = lambda a: pl.BlockSpec(a.shape, lambda i: (0,) * a.ndim)
    consts = (gpost, wg, bg, wp, gple)
    return pl.pallas_call(
        _final_kernel,
        grid=(n // TM,),
        in_specs=[tok(D_MODEL), pl.BlockSpec((TOP_K, TM, ROW_WORDS), lambda i: (0, i, 0)), tok(TOP_K),
                  tok(D_PLE)] + [full(a) for a in consts],
        out_specs=tok(D_MODEL),
        out_shape=jax.ShapeDtypeStruct((n, D_MODEL), F32),
        compiler_params=pltpu.CompilerParams(dimension_semantics=("arbitrary",),
                                             vmem_limit_bytes=VMEM_LIMIT),
        name="final",
    )(x1, yg, gate_t, p, *consts)


def _rope_angles(positions):
    inv_freq = 1.0 / (ROPE_THETA ** (jnp.arange(0, QK_ROPE, 2, dtype=F32) / QK_ROPE))
    return inv_freq[:, None] * positions.astype(F32).reshape(1, -1)


def _pad_head(w):
    return jnp.pad(w, [(0, 0)] * (w.ndim - 1) + [(0, HEAD_PAD - w.shape[-1])])


def _layer(x, p_l, ang, prm):
    (attn_pre_g, w_in, q_norm_g, w_uq, kv_norm_g, w_ukv, sg_norm_g, w_spatial, b_spatial, mla_out_g,
     sg_out_g, w_out, attn_post_g, ffn_pre_g, w_router, b_router, w_gate_up, b_gate_up, w_down, b_down,
     ffn_post_g, w_ple_gate, b_ple_gate, w_ple_proj, ple_norm_g) = prm
    b, s, _ = x.shape
    n = b * s
    xt = x.reshape(n, D_MODEL)
    row2 = lambda a: a.reshape(1, -1)

    w_kr = w_in[:, Q_LORA + KV_LORA:Q_LORA + KV_LORA + QK_ROPE]
    place = lambda w: jnp.pad(w, ((0, 0), (QK_NOPE, HEAD_PAD - QK_NOPE - QK_ROPE)))
    win_aug = jnp.concatenate(
        [w_in[:, :Q_LORA + KV_LORA], place(w_kr), w_in[:, Q_LORA + KV_LORA + QK_ROPE:]], axis=-1).astype(BF16)
    wq_aug = _pad_head(w_uq).reshape(Q_LORA, -1).astype(BF16)
    wk_pad = _pad_head(w_ukv[..., :QK_NOPE]).reshape(KV_LORA, -1).astype(BF16)
    wv_t = w_ukv[..., QK_NOPE:].reshape(KV_LORA, -1).T.astype(BF16)
    head_of = jnp.arange(SG_WIDTH) // SG_HEAD_DIM
    gsum = (head_of[:, None] == head_of[None, :]).astype(BF16)
    wcat = w_spatial.transpose(1, 0, 2).reshape(SG_CHUNK, SG_HEADS * SG_CHUNK)
    bsp = jnp.repeat(b_spatial.T, SG_HEAD_DIM, axis=1)

    q, k, vt, mix_sg = _inproj(xt, ang, row2(attn_pre_g), win_aug, row2(q_norm_g), wq_aug,
                               row2(kv_norm_g), wk_pad, wv_t, row2(sg_norm_g), gsum, wcat, bsp,
                               row2(sg_out_g))
    mix_mla = _attention(q, k, vt, row2(mla_out_g), b, s)

    w_out_b = w_out.astype(BF16)
    x1, h2, idx, gate, rank, cnt = _postattn(
        xt, mix_mla, mix_sg, w_out_b[:MLA_WIDTH], w_out_b[MLA_WIDTH:], row2(attn_post_g), row2(ffn_pre_g),
        w_router.T.astype(BF16), b_router.reshape(N_EXPERTS, 1))

    a = n * TOP_K
    counts = cnt[:, 0].astype(I32)
    padded = (counts + MOE_BM - 1) // MOE_BM * MOE_BM
    pad_end = jnp.cumsum(padded)
    pad_start = pad_end - padded
    start_of = jnp.sum(jnp.where(idx[..., None] == jnp.arange(N_EXPERTS, dtype=I32), pad_start, 0), axis=-1)
    dest = (start_of + rank).reshape(a)
    n_rows = a + N_EXPERTS * MOE_BM

    x_rows = _sc_dispatch(h2, dest, n_rows)
    y_rows = _moe(padded // MOE_BM, pad_start, x_rows, w_gate_up, b_gate_up[:, None, :],
                  w_down, b_down[:, None, :])
    yg = _sc_gather(y_rows, dest).reshape(TOP_K, n, ROW_WORDS)

    out = _final(x1, yg, gate.T, p_l.reshape(n, D_PLE), row2(ffn_post_g), w_ple_gate.astype(BF16),
                 row2(b_ple_gate), w_ple_proj.astype(BF16), row2(ple_norm_g))
    return out.reshape(b, s, D_MODEL)


def kernel(x, p, positions, attn_pre_g, w_in, q_norm_g, w_uq, kv_norm_g, w_ukv, sg_norm_g, w_spatial, b_spatial, mla_out_g, sg_out_g, w_out, attn_post_g, ffn_pre_g, w_router, b_router, w_gate_up, b_gate_up, w_down, b_down, ffn_post_g, w_ple_gate, b_ple_gate, w_ple_proj, ple_norm_g):
    ang = _rope_angles(positions)
    params = (attn_pre_g, w_in, q_norm_g, w_uq, kv_norm_g, w_ukv, sg_norm_g, w_spatial, b_spatial, mla_out_g,
              sg_out_g, w_out, attn_post_g, ffn_pre_g, w_router, b_router, w_gate_up, b_gate_up, w_down, b_down,
              ffn_post_g, w_ple_gate, b_ple_gate, w_ple_proj, ple_norm_g)
    for layer in range(p.shape[0]):
        x = _layer(x, p[layer], ang, tuple(a[layer] for a in params))
    return x
```

```python
import functools

import jax
import jax.numpy as jnp
from jax import lax
from jax.experimental import pallas as pl
from jax.experimental.pallas import tpu as pltpu
from jax.experimental.pallas import tpu_sc as plsc

F32 = jnp.float32
BF16 = jnp.bfloat16
I32 = jnp.int32

D_MODEL = 1024
HEADS = 8
QK_NOPE = 64
QK_ROPE = 32
V_HEAD = 64
Q_LORA = 256
KV_LORA = 128
ROPE_THETA = 10000.0
SG_HEADS = 8
SG_HEAD_DIM = 64
SG_CHUNK = 128
SG_WIDTH = SG_HEADS * SG_HEAD_DIM
MLA_WIDTH = HEADS * V_HEAD
N_EXPERTS = 32
TOP_K = 4
SWIGLU_LIMIT = 7.0
SWIGLU_ALPHA = 1.702
D_PLE = 256
EPS = 1e-6

LANES = 128
HEAD_PAD = LANES
QK_SCALE = (QK_NOPE + QK_ROPE) ** -0.5
LOG2E = 1.4426950408889634

COL_CQ = 0
COL_CKV = COL_CQ + Q_LORA
COL_KR = COL_CKV + KV_LORA
COL_SG = COL_KR + LANES
IN_COLS_AUG = COL_SG + 2 * SG_WIDTH

TM = 1024
BQ = 512
ATTN_BK = 1024
ATTN_ONES_ROWS = 16
MOE_BM = 512
MOE_W_CHUNKS = 4
VMEM_LIMIT = 56 * 1024 * 1024

NT_DIMS = (((1,), (1,)), ((), ()))
ROW_WORDS = D_MODEL // 2
SC_CHUNK = 64
SC_DEST_CHUNK = 8192


def _rms(x, g):
    return x * lax.rsqrt(jnp.mean(x * x, axis=-1, keepdims=True) + EPS) * g


def _dot(a, b):
    return jnp.dot(a, b, preferred_element_type=F32)


def _dot_nt(a, b):
    return lax.dot_general(a, b, NT_DIMS, preferred_element_type=F32)


def _pack_halves(x):
    half = x.shape[1] // 2
    return pltpu.pack_elementwise([x[:, :half], x[:, half:]], packed_dtype=BF16)


def _unpack_halves(w):
    return (pltpu.unpack_elementwise(w, index=0, packed_dtype=BF16, unpacked_dtype=F32),
            pltpu.unpack_elementwise(w, index=1, packed_dtype=BF16, unpacked_dtype=F32))


def _inproj_kernel(x_ref, ang_ref, gpre_ref, win_ref, qg_ref, wq_ref, kvg_ref, wk_ref, wvt_ref,
                   sgg_ref, gsum_ref, wcat_ref, bsp_ref, sgo_ref,
                   q_out, k_out, vt_out, sg_out):
    x = x_ref[...]
    tm = x.shape[0]
    h = _rms(x, gpre_ref[...])
    z = _dot(h.astype(BF16), win_ref[...])

    ang = ang_ref[...]
    c, s = jnp.cos(ang), jnp.sin(ang)
    tail = HEAD_PAD - QK_NOPE - QK_ROPE
    cos = jnp.concatenate([jnp.ones((QK_NOPE, tm), F32), c, c, jnp.ones((tail, tm), F32)], axis=0).T
    sin = jnp.concatenate([jnp.zeros((QK_NOPE, tm), F32), s, s, jnp.zeros((tail, tm), F32)], axis=0).T
    first_half = lax.broadcasted_iota(I32, (tm, HEAD_PAD), 1) < QK_NOPE + QK_ROPE // 2
    sin_signed = jnp.where(first_half, -sin, sin)

    def rope(t):
        partner = jnp.where(first_half, pltpu.roll(t, HEAD_PAD - QK_ROPE // 2, 1), pltpu.roll(t, QK_ROPE // 2, 1))
        return t * cos + partner * sin_signed

    cqn = _rms(z[:, COL_CQ:COL_CQ + Q_LORA], qg_ref[...])
    qq = _dot(cqn.astype(BF16), wq_ref[...])
    for hd in range(HEADS):
        sl = slice(hd * HEAD_PAD, (hd + 1) * HEAD_PAD)
        q_out[:, sl] = (rope(qq[:, sl]) * (QK_SCALE * LOG2E)).astype(BF16)

    ckvn = _rms(z[:, COL_CKV:COL_CKV + KV_LORA], kvg_ref[...]).astype(BF16)
    kk = _dot(ckvn, wk_ref[...])
    kr = rope(z[:, COL_KR:COL_KR + LANES])
    for hd in range(HEADS):
        sl = slice(hd * HEAD_PAD, (hd + 1) * HEAD_PAD)
        k_out[:, sl] = (kk[:, sl] + kr).astype(BF16)
    vt_out[...] = _dot_nt(wvt_ref[...], ckvn).astype(BF16)

    zg = jax.nn.gelu(z[:, COL_SG:COL_SG + 2 * SG_WIDTH])
    u = zg[:, :SG_WIDTH]
    v = zg[:, SG_WIDTH:]
    ms = _dot((v * v).astype(BF16), gsum_ref[...]) * (1.0 / SG_HEAD_DIM)
    vn = v * lax.rsqrt(ms + EPS) * sgg_ref[...]

    row = lax.broadcasted_iota(I32, (SG_CHUNK, SG_HEADS * SG_CHUNK), 0)
    col = lax.broadcasted_iota(I32, (SG_CHUNK, SG_HEADS * SG_CHUNK), 1)
    wcat = jnp.where((col % SG_CHUNK) <= row, wcat_ref[...], 0.0).astype(BF16)
    lane_head = lax.broadcasted_iota(I32, (SG_CHUNK, SG_WIDTH), 1) // SG_HEAD_DIM
    bsp = bsp_ref[...]
    sgo = sgo_ref[...]
    for c in range(x.shape[0] // SG_CHUNK):
        rows = slice(c * SG_CHUNK, (c + 1) * SG_CHUNK)
        vc = vn[rows]
        vbd = jnp.concatenate(
            [jnp.where(lane_head == hd, vc, 0.0).astype(BF16) for hd in range(SG_HEADS)], axis=0)
        vm = _dot(wcat, vbd) + bsp
        sg_out[rows, :] = _rms(u[rows] * vm, sgo).astype(BF16)


def _inproj(x, ang, gpre, win, qg, wq, kvg, wk, wvt, sgg, gsum, wcat, bsp, sgo):
    n = x.shape[0]
    tok = lambda w: pl.BlockSpec((TM, w), lambda i: (i, 0))
    full = lambda a: pl.BlockSpec(a.shape, lambda i: (0,) * a.ndim)
    consts = (gpre, win, qg, wq, kvg, wk, wvt, sgg, gsum, wcat, bsp, sgo)
    return pl.pallas_call(
        _inproj_kernel,
        grid=(n // TM,),
        in_specs=[tok(D_MODEL), pl.BlockSpec((QK_ROPE // 2, TM), lambda i: (0, i))] + [full(a) for a in consts],
        out_specs=[tok(HEADS * HEAD_PAD), tok(HEADS * HEAD_PAD),
                   pl.BlockSpec((MLA_WIDTH, TM), lambda i: (0, i)), tok(SG_WIDTH)],
        out_shape=[jax.ShapeDtypeStruct((n, HEADS * HEAD_PAD), BF16),
                   jax.ShapeDtypeStruct((n, HEADS * HEAD_PAD), BF16),
                   jax.ShapeDtypeStruct((MLA_WIDTH, n), BF16),
                   jax.ShapeDtypeStruct((n, SG_WIDTH), BF16)],
        compiler_params=pltpu.CompilerParams(dimension_semantics=("arbitrary",),
                                             vmem_limit_bytes=VMEM_LIMIT),
        name="inproj",
    )(x, ang, *consts)


def _attn_kernel(q_ref, k_ref, vt_ref, g_ref, o_ref, m_sc, l_sc, acc_sc):
    i = pl.program_id(1)
    bq = q_ref.shape[0]
    m_sc[...] = jnp.full(m_sc.shape, -jnp.inf, F32)
    l_sc[...] = jnp.zeros(l_sc.shape, F32)
    acc_sc[...] = jnp.zeros(acc_sc.shape, F32)

    def qk(start, nk, hd):
        hs = slice(hd * HEAD_PAD, (hd + 1) * HEAD_PAD)
        return _dot_nt(k_ref[pl.ds(start, nk), hs], q_ref[:, hs])

    def block(start, nk, masked):
        ones = jnp.ones((ATTN_ONES_ROWS, nk), BF16)
        queue = [qk(start, nk, 0), qk(start, nk, 1)]
        for hd in range(HEADS):
            vs = slice(hd * V_HEAD, (hd + 1) * V_HEAD)
            st = queue.pop(0)
            if hd + 2 < HEADS:
                queue.append(qk(start, nk, hd + 2))
            if masked:
                causal = (lax.broadcasted_iota(I32, (nk, bq), 0) <= lax.broadcasted_iota(I32, (nk, bq), 1))
                st = jnp.where(causal, st, -jnp.inf)
            m_prev = m_sc[hd:hd + 1, :]
            m_new = jnp.maximum(m_prev, jnp.max(st, axis=0, keepdims=True))
            alpha = jnp.exp2(m_prev - m_new)
            p = jnp.exp2(st - m_new).astype(BF16)
            vta = jnp.concatenate([vt_ref[vs, pl.ds(start, nk)], ones], axis=0)
            pv = _dot(vta, p)
            l_sc[hd:hd + 1, :] = alpha * l_sc[hd:hd + 1, :] + pv[V_HEAD:V_HEAD + 1, :]
            acc_sc[vs, :] = alpha * acc_sc[vs, :] + pv[:V_HEAD, :]
            m_sc[hd:hd + 1, :] = m_new

    def body(kb, carry):
        block(pl.multiple_of(kb * ATTN_BK, ATTN_BK), ATTN_BK, False)
        return carry

    visible = i * bq
    lax.fori_loop(0, visible // ATTN_BK, body, 0)
    for r in range(ATTN_BK // bq - 1, 0, -1):

        @pl.when(visible % ATTN_BK >= r * bq)
        def _():
            block(pl.multiple_of((i - r) * bq, bq), bq, False)

    block(pl.multiple_of(i * bq, bq), bq, True)
    ot = jnp.concatenate(
        [acc_sc[hd * V_HEAD:(hd + 1) * V_HEAD, :] * (1.0 / l_sc[hd:hd + 1, :]) for hd in range(HEADS)], axis=0)
    o_ref[...] = _rms(ot.T, g_ref[...]).astype(BF16)


def _attention(q, k, vt, g, b, s):
    nq = s // BQ
    return pl.pallas_call(
        _attn_kernel,
        grid=(b, nq),
        in_specs=[pl.BlockSpec((BQ, HEADS * HEAD_PAD), lambda bi, i: (bi * nq + i, 0)),
                  pl.BlockSpec((s, HEADS * HEAD_PAD), lambda bi, i: (bi, 0), pipeline_mode=pl.Buffered(1)),
                  pl.BlockSpec((MLA_WIDTH, s), lambda bi, i: (0, bi), pipeline_mode=pl.Buffered(1)),
                  pl.BlockSpec(g.shape, lambda bi, i: (0, 0))],
        out_specs=pl.BlockSpec((BQ, MLA_WIDTH), lambda bi, i: (bi * nq + i, 0)),
        out_shape=jax.ShapeDtypeStruct((b * s, MLA_WIDTH), BF16),
        scratch_shapes=[pltpu.VMEM((HEADS, BQ), F32), pltpu.VMEM((HEADS, BQ), F32),
                        pltpu.VMEM((MLA_WIDTH, BQ), F32)],
        compiler_params=pltpu.CompilerParams(dimension_semantics=("arbitrary", "arbitrary"),
                                             vmem_limit_bytes=VMEM_LIMIT),
        name="attention",
    )(q, k, vt, g)


def _postattn_kernel(x_ref, mla_ref, sg_ref, wo1_ref, wo2_ref, gpost_ref, gffn_ref, wrt_ref, brt_ref,
                     x1_out, h2_out, idx_out, gate_out, rank_out, cnt_out, cnt_sc):
    i = pl.program_id(0)

    @pl.when(i == 0)
    def _():
        cnt_sc[...] = jnp.zeros(cnt_sc.shape, F32)

    a = _dot(mla_ref[...], wo1_ref[...]) + _dot(sg_ref[...], wo2_ref[...])
    x1 = x_ref[...] + _rms(a, gpost_ref[...])
    x1_out[...] = x1
    h2f = _rms(x1, gffn_ref[...])
    h2_out[...] = _pack_halves(h2f)
    h2 = h2f.astype(BF16)

    tm = h2.shape[0]
    logits = _dot_nt(wrt_ref[...], h2) + brt_ref[...]
    eidx = lax.broadcasted_iota(I32, (N_EXPERTS, tm), 0)
    vals, idxs, sels = [], [], []
    for _ in range(TOP_K):
        mx = jnp.max(logits, axis=0, keepdims=True)
        ik = jnp.min(jnp.where(logits == mx, eidx, N_EXPERTS), axis=0, keepdims=True)
        sel = eidx == ik
        vals.append(mx)
        idxs.append(ik)
        sels.append(sel)
        logits = jnp.where(sel, -jnp.inf, logits)
    ex = [jnp.exp(v - vals[0]) for v in vals]
    den = ex[0] + ex[1] + ex[2] + ex[3]
    gate_out[...] = jnp.concatenate([e / den for e in ex], axis=0)
    idx_out[...] = jnp.concatenate(idxs, axis=0)

    maskf = sum(jnp.where(s, 1.0, 0.0) for s in sels)
    before = (lax.broadcasted_iota(I32, (tm, tm), 0) < lax.broadcasted_iota(I32, (tm, tm), 1))
    prefix = _dot(maskf.astype(BF16), jnp.where(before, 1.0, 0.0).astype(BF16))
    base = cnt_sc[...]
    tot = base + prefix
    ranks = [jnp.sum(jnp.where(s, tot, 0.0), axis=0, keepdims=True) for s in sels]
    rank_out[...] = jnp.concatenate(ranks, axis=0).astype(I32)
    cnt = base + jnp.sum(maskf, axis=1, keepdims=True)
    cnt_sc[...] = cnt
    cnt_out[...] = jnp.broadcast_to(cnt, cnt_out.shape)


def _postattn(x, mla, sg, wo1, wo2, gpost, gffn, wrt, brt):
    n = x.shape[0]
    tok = lambda w: pl.BlockSpec((TM, w), lambda i: (i, 0))
    tokt = pl.BlockSpec((TOP_K, TM), lambda i: (0, i))
    full = lambda a: pl.BlockSpec(a.shape, lambda i: (0,) * a.ndim)
    consts = (wo1, wo2, gpost, gffn, wrt, brt)
    return pl.pallas_call(
        _postattn_kernel,
        grid=(n // TM,),
        in_specs=[tok(D_MODEL), tok(MLA_WIDTH), tok(SG_WIDTH)] + [full(a) for a in consts],
        out_specs=[tok(D_MODEL), tok(ROW_WORDS), tokt, tokt, tokt,
                   pl.BlockSpec((N_EXPERTS, LANES), lambda i: (0, 0))],
        out_shape=[jax.ShapeDtypeStruct((n, D_MODEL), F32),
                   jax.ShapeDtypeStruct((n, ROW_WORDS), I32),
                   jax.ShapeDtypeStruct((TOP_K, n), I32),
                   jax.ShapeDtypeStruct((TOP_K, n), F32),
                   jax.ShapeDtypeStruct((TOP_K, n), I32),
                   jax.ShapeDtypeStruct((N_EXPERTS, LANES), F32)],
        scratch_shapes=[pltpu.VMEM((N_EXPERTS, 1), F32)],
        compiler_params=pltpu.CompilerParams(dimension_semantics=("arbitrary",),
                                             vmem_limit_bytes=VMEM_LIMIT),
        name="postattn",
    )(x, mla, sg, *consts)


def _sc_workers():
    info = plsc.get_sparse_core_info()
    return info.num_cores, info.num_cores * info.num_subcores, info.num_lanes


def _sc_stream_rows(table_hbm, idx_v, out_hbm, base, n_chunks, buf, sem_g, sem_w):
    def gather(j, b):
        rows = idx_v.at[pl.ds(pl.multiple_of(j * SC_CHUNK, SC_CHUNK), SC_CHUNK)]
        return pltpu.make_async_copy(table_hbm.at[rows], buf.at[b], sem_g.at[b])

    def write(j, b):
        rows = pl.ds(pl.multiple_of(base + j * SC_CHUNK, SC_CHUNK), SC_CHUNK)
        return pltpu.make_async_copy(buf.at[b], out_hbm.at[rows], sem_w.at[b])

    gather(0, 0).start()

    @pl.loop(0, n_chunks, step=2)
    def _(j0):
        for b in range(2):
            j = j0 + b
            gather(j, b).wait()

            @pl.when(j >= 1)
            def _():
                write(j - 1, 1 - b).wait()

            @pl.when(j + 1 < n_chunks)
            def _():
                gather(j + 1, 1 - b).start()

            write(j, b).start()

    write(n_chunks - 1, 1).wait()


def _sc_gather(table, idx):
    n_out = idx.shape[0]
    width = table.shape[1]
    num_cores, workers, _ = _sc_workers()
    per_w = n_out // workers
    n_chunks = per_w // SC_CHUNK
    assert per_w * workers == n_out and n_chunks * SC_CHUNK == per_w and n_chunks % 2 == 0

    @functools.partial(
        pl.kernel, mesh=plsc.VectorSubcoreMesh(core_axis_name="c", subcore_axis_name="s"),
        out_type=jax.ShapeDtypeStruct((n_out, width), table.dtype),
        scratch_types=[pltpu.VMEM((per_w,), I32), pltpu.VMEM((2, SC_CHUNK, width), table.dtype),
                       pltpu.SemaphoreType.DMA((2,)), pltpu.SemaphoreType.DMA((2,))])
    def gather_kernel(table_hbm, idx_hbm, out_hbm, idx_v, buf, sem_g, sem_w):
        wid = lax.axis_index("s") * num_cores + lax.axis_index("c")
        base = pl.multiple_of(wid * per_w, SC_CHUNK)
        pltpu.sync_copy(idx_hbm.at[pl.ds(base, per_w)], idx_v)
        _sc_stream_rows(table_hbm, idx_v, out_hbm, base, n_chunks, buf, sem_g, sem_w)

    return gather_kernel(table, idx)


def _sc_dispatch(table, dest, n_out):
    n_assign = dest.shape[0]
    n, width = table.shape
    num_cores, workers, lanes = _sc_workers()
    per_w = n_out // workers
    n_chunks = per_w // SC_CHUNK
    assert per_w * workers == n_out and n_chunks * SC_CHUNK == per_w and n_chunks % 2 == 0
    assert n_assign % SC_DEST_CHUNK == 0 and SC_DEST_CHUNK % lanes == 0 and per_w % lanes == 0

    @functools.partial(
        pl.kernel, mesh=plsc.VectorSubcoreMesh(core_axis_name="c", subcore_axis_name="s"),
        out_type=jax.ShapeDtypeStruct((n_out, width), table.dtype),
        scratch_types=[pltpu.VMEM((per_w,), I32), pltpu.VMEM((SC_DEST_CHUNK,), I32),
                       pltpu.VMEM((2, SC_CHUNK, width), table.dtype),
                       pltpu.SemaphoreType.DMA((2,)), pltpu.SemaphoreType.DMA((2,))],
        compiler_params=pltpu.CompilerParams(needs_layout_passes=False))
    def dispatch_kernel(table_hbm, dest_hbm, out_hbm, tok_v, dest_v, buf, sem_g, sem_w):
        wid = lax.axis_index("s") * num_cores + lax.axis_index("c")
        base = pl.multiple_of(wid * per_w, SC_CHUNK)
        lane = lax.iota(I32, lanes)

        @pl.loop(0, per_w, step=lanes)
        def _(r):
            tok_v[pl.ds(r, lanes)] = lax.rem(base + r + lane, n)

        @pl.loop(0, n_assign // SC_DEST_CHUNK)
        def _(c):
            first = pl.multiple_of(c * SC_DEST_CHUNK, SC_DEST_CHUNK)
            pltpu.sync_copy(dest_hbm.at[pl.ds(first, SC_DEST_CHUNK)], dest_v)

            @plsc.parallel_loop(0, SC_DEST_CHUNK, step=lanes, unroll=4)
            def _(i):
                local = dest_v[pl.ds(i, lanes)] - base
                mine = (local >= 0) & (local < per_w)
                tok = lax.rem(first + i + lane, n)
                plsc.store_scatter(tok_v, [jnp.where(mine, local, 0)], tok, mask=mine)

        _sc_stream_rows(table_hbm, tok_v, out_hbm, base, n_chunks, buf, sem_g, sem_w)

    return dispatch_kernel(table, dest)


def _moe_kernel(nb_ref, row0_ref, x_hbm, wgu_hbm, bgu_ref, wd_hbm, bd_ref, y_hbm,
                wgu_f, wd_f, wgu_sc, wd_sc, xbuf, ybuf, sem_x, sem_y, sem_w):
    e = pl.program_id(0)
    n_e = pl.num_programs(0)
    nb = nb_ref[e]
    row0 = row0_ref[e]
    wslot = e % 2
    chunk_rows = D_MODEL // MOE_W_CHUNKS

    def w_copies(expert, c, slot):
        r = pl.ds(c * chunk_rows, chunk_rows)
        return (pltpu.make_async_copy(wgu_hbm.at[expert, r, :], wgu_f.at[slot, r, :], sem_w.at[slot]),
                pltpu.make_async_copy(wd_hbm.at[expert, r, :], wd_f.at[slot, r, :], sem_w.at[slot]))

    def start_chunk(expert, c, slot):
        for cp in w_copies(expert, c, slot):
            cp.start()

    def rows(j):
        return pl.ds(pl.multiple_of(row0 + j * MOE_BM, MOE_BM), MOE_BM)

    def x_copy(j, slot):
        return pltpu.make_async_copy(x_hbm.at[rows(j)], xbuf.at[slot], sem_x.at[slot])

    def y_copy(j, slot):
        return pltpu.make_async_copy(ybuf.at[slot], y_hbm.at[rows(j)], sem_y.at[slot])

    @pl.when(e == 0)
    def _():
        for c in range(MOE_W_CHUNKS):
            start_chunk(0, c, 0)

    @pl.when(nb > 0)
    def _():
        x_copy(0, 0).start()

    for c in range(MOE_W_CHUNKS):
        for cp in w_copies(e, c, wslot):
            cp.wait()
    wgu_sc[...] = wgu_f[wslot].astype(BF16)
    wd_sc[...] = wd_f[wslot].astype(BF16)
    nxt = jnp.minimum(e + 1, n_e - 1)
    has_next = e + 1 < n_e

    def body(j, carry):
        slot = j % 2

        @pl.when(has_next & (j < MOE_W_CHUNKS))
        def _():
            start_chunk(nxt, j, 1 - wslot)

        x_copy(j, slot).wait()

        @pl.when(j + 1 < nb)
        def _():
            x_copy(j + 1, 1 - slot).start()

        @pl.when(j >= 2)
        def _():
            y_copy(j - 2, slot).wait()

        x = jnp.concatenate(_unpack_halves(xbuf[slot]), axis=1).astype(BF16)
        gu = _dot(x, wgu_sc[...]) + bgu_ref[0]
        g = jnp.minimum(gu[:, :D_MODEL], SWIGLU_LIMIT)
        u = jnp.clip(gu[:, D_MODEL:], -SWIGLU_LIMIT, SWIGLU_LIMIT)
        act = (u + 1.0) * (g * jax.nn.sigmoid(SWIGLU_ALPHA * g))
        ybuf[slot] = _pack_halves(_dot(act.astype(BF16), wd_sc[...]) + bd_ref[0])
        y_copy(j, slot).start()
        return carry

    lax.fori_loop(0, nb, body, 0)

    def rest(c, carry):
        @pl.when(has_next)
        def _():
            start_chunk(nxt, c, 1 - wslot)
        return carry

    lax.fori_loop(jnp.minimum(nb, MOE_W_CHUNKS), MOE_W_CHUNKS, rest, 0)

    @pl.when(nb >= 2)
    def _():
        y_copy(nb - 2, nb % 2).wait()

    @pl.when(nb >= 1)
    def _():
        y_copy(nb - 1, (nb - 1) % 2).wait()


def _moe(blocks_of, row0_of, x_rows, wgu, bgu, wd, bd):
    hbm = pl.BlockSpec(memory_space=pl.ANY)
    grid_spec = pltpu.PrefetchScalarGridSpec(
        num_scalar_prefetch=2,
        grid=(N_EXPERTS,),
        in_specs=[hbm, hbm,
                  pl.BlockSpec((1, 1, 2 * D_MODEL), lambda e, nb, r0: (e, 0, 0)),
                  hbm,
                  pl.BlockSpec((1, 1, D_MODEL), lambda e, nb, r0: (e, 0, 0))],
        out_specs=hbm,
        scratch_shapes=[pltpu.VMEM((2, D_MODEL, 2 * D_MODEL), F32), pltpu.VMEM((2, D_MODEL, D_MODEL), F32),
                        pltpu.VMEM((D_MODEL, 2 * D_MODEL), BF16), pltpu.VMEM((D_MODEL, D_MODEL), BF16),
                        pltpu.VMEM((2, MOE_BM, ROW_WORDS), I32), pltpu.VMEM((2, MOE_BM, ROW_WORDS), I32),
                        pltpu.SemaphoreType.DMA((2,)), pltpu.SemaphoreType.DMA((2,)),
                        pltpu.SemaphoreType.DMA((2,))],
    )
    return pl.pallas_call(
        _moe_kernel,
        grid_spec=grid_spec,
        out_shape=jax.ShapeDtypeStruct(x_rows.shape, I32),
        compiler_params=pltpu.CompilerParams(dimension_semantics=("arbitrary",),
                                             vmem_limit_bytes=VMEM_LIMIT),
        name="moe",
    )(blocks_of, row0_of, x_rows, wgu, bgu, wd, bd)


def _final_kernel(x1_ref, yg_ref, gt_ref, p_ref, gpost_ref, wg_ref, bg_ref, wp_ref, gple_ref, o_ref):
    gt = gt_ref[...]
    lo, hi = _unpack_halves(yg_ref[0])
    y = gt[:, 0:1] * jnp.concatenate([lo, hi], axis=1)
    for k in range(1, TOP_K):
        lo, hi = _unpack_halves(yg_ref[k])
        y = y + gt[:, k:k + 1] * jnp.concatenate([lo, hi], axis=1)
    x2 = x1_ref[...] + _rms(y, gpost_ref[...])
    gate = jax.nn.sigmoid(_dot(x2.astype(BF16), wg_ref[...]) + bg_ref[...])
    pp = _dot(p_ref[...].astype(BF16), wp_ref[...])
    o_ref[...] = x2 + _rms(gate * pp, gple_ref[...])


def _final(x1, yg, gate_t, p, gpost, wg, bg, wp, gple):
    n = x1.shape[0]
    tok = lambda w: pl.BlockSpec((TM, w), lambda i: (i, 0))
    full = lambda a: pl.BlockSpec(a.shape, lambda i: (0,) * a.ndim)
    consts = (gpost, wg, bg, wp, gple)
    return pl.pallas_call(
        _final_kernel,
        grid=(n // TM,),
        in_specs=[tok(D_MODEL), pl.BlockSpec((TOP_K, TM, ROW_WORDS), lambda i: (0, i, 0)), tok(TOP_K),
                  tok(D_PLE)] + [full(a) for a in consts],
        out_specs=tok(D_MODEL),
        out_shape=jax.ShapeDtypeStruct((n, D_MODEL), F32),
        compiler_params=pltpu.CompilerParams(dimension_semantics=("arbitrary",),
                                             vmem_limit_bytes=VMEM_LIMIT),
        name="final",
    )(x1, yg, gate_t, p, *consts)


def _rope_angles(positions):
    inv_freq = 1.0 / (ROPE_THETA ** (jnp.arange(0, QK_ROPE, 2, dtype=F32) / QK_ROPE))
    return inv_freq[:, None] * positions.astype(F32).reshape(1, -1)


def _pad_head(w):
    return jnp.pad(w, [(0, 0)] * (w.ndim - 1) + [(0, HEAD_PAD - w.shape[-1])])


def _layer(x, p_l, ang, prm):
    (attn_pre_g, w_in, q_norm_g, w_uq, kv_norm_g, w_ukv, sg_norm_g, w_spatial, b_spatial, mla_out_g,
     sg_out_g, w_out, attn_post_g, ffn_pre_g, w_router, b_router, w_gate_up, b_gate_up, w_down, b_down,
     ffn_post_g, w_ple_gate, b_ple_gate, w_ple_proj, ple_norm_g) = prm
    b, s, _ = x.shape
    n = b * s
    xt = x.reshape(n, D_MODEL)
    row2 = lambda a: a.reshape(1, -1)

    w_kr = w_in[:, Q_LORA + KV_LORA:Q_LORA + KV_LORA + QK_ROPE]
    place = lambda w: jnp.pad(w, ((0, 0), (QK_NOPE, HEAD_PAD - QK_NOPE - QK_ROPE)))
    win_aug = jnp.concatenate(
        [w_in[:, :Q_LORA + KV_LORA], place(w_kr), w_in[:, Q_LORA + KV_LORA + QK_ROPE:]], axis=-1).astype(BF16)
    wq_aug = _pad_head(w_uq).reshape(Q_LORA, -1).astype(BF16)
    wk_pad = _pad_head(w_ukv[..., :QK_NOPE]).reshape(KV_LORA, -1).astype(BF16)
    wv_t = w_ukv[..., QK_NOPE:].reshape(KV_LORA, -1).T.astype(BF16)
    head_of = jnp.arange(SG_WIDTH) // SG_HEAD_DIM
    gsum = (head_of[:, None] == head_of[None, :]).astype(BF16)
    wcat = w_spatial.transpose(1, 0, 2).reshape(SG_CHUNK, SG_HEADS * SG_CHUNK)
    bsp = jnp.repeat(b_spatial.T, SG_HEAD_DIM, axis=1)

    q, k, vt, mix_sg = _inproj(xt, ang, row2(attn_pre_g), win_aug, row2(q_norm_g), wq_aug,
                               row2(kv_norm_g), wk_pad, wv_t, row2(sg_norm_g), gsum, wcat, bsp,
                               row2(sg_out_g))
    mix_mla = _attention(q, k, vt, row2(mla_out_g), b, s)

    w_out_b = w_out.astype(BF16)
    x1, h2, idx, gate, rank, cnt = _postattn(
        xt, mix_mla, mix_sg, w_out_b[:MLA_WIDTH], w_out_b[MLA_WIDTH:], row2(attn_post_g), row2(ffn_pre_g),
        w_router.T.astype(BF16), b_router.reshape(N_EXPERTS, 1))

    a = n * TOP_K
    counts = cnt[:, 0].astype(I32)
    padded = (counts + MOE_BM - 1) // MOE_BM * MOE_BM
    pad_end = jnp.cumsum(padded)
    pad_start = pad_end - padded
    start_of = jnp.sum(jnp.where(idx[..., None] == jnp.arange(N_EXPERTS, dtype=I32), pad_start, 0), axis=-1)
    dest = (start_of + rank).reshape(a)
    n_rows = a + N_EXPERTS * MOE_BM

    x_rows = _sc_dispatch(h2, dest, n_rows)
    y_rows = _moe(padded // MOE_BM, pad_start, x_rows, w_gate_up, b_gate_up[:, None, :],
                  w_down, b_down[:, None, :])
    yg = _sc_gather(y_rows, dest).reshape(TOP_K, n, ROW_WORDS)

    out = _final(x1, yg, gate.T, p_l.reshape(n, D_PLE), row2(ffn_post_g), w_ple_gate.astype(BF16),
                 row2(b_ple_gate), w_ple_proj.astype(BF16), row2(ple_norm_g))
    return out.reshape(b, s, D_MODEL)


def kernel(x, p, positions, attn_pre_g, w_in, q_norm_g, w_uq, kv_norm_g, w_ukv, sg_norm_g, w_spatial, b_spatial, mla_out_g, sg_out_g, w_out, attn_post_g, ffn_pre_g, w_router, b_router, w_gate_up, b_gate_up, w_down, b_down, ffn_post_g, w_ple_gate, b_ple_gate, w_ple_proj, ple_norm_g):
    ang = _rope_angles(positions)
    params = (attn_pre_g, w_in, q_norm_g, w_uq, kv_norm_g, w_ukv, sg_norm_g, w_spatial, b_spatial, mla_out_g,
              sg_out_g, w_out, attn_post_g, ffn_pre_g, w_router, b_router, w_gate_up, b_gate_up, w_down, b_down,
              ffn_post_g, w_ple_gate, b_ple_gate, w_ple_proj, ple_norm_g)
    for layer in range(p.shape[0]):
        x = _layer(x, p[layer], ang, tuple(a[layer] for a in params))
    return x
```

```python
import functools

import jax
import jax.numpy as jnp
from jax import lax
from jax.experimental import pallas as pl
from jax.experimental.pallas import tpu as pltpu
from jax.experimental.pallas import tpu_sc as plsc

F32 = jnp.float32
BF16 = jnp.bfloat16
I32 = jnp.int32

D_MODEL = 1024
HEADS = 8
QK_NOPE = 64
QK_ROPE = 32
V_HEAD = 64
Q_LORA = 256
KV_LORA = 128
ROPE_THETA = 10000.0
SG_HEADS = 8
SG_HEAD_DIM = 64
SG_CHUNK = 128
SG_WIDTH = SG_HEADS * SG_HEAD_DIM
MLA_WIDTH = HEADS * V_HEAD
N_EXPERTS = 32
TOP_K = 4
SWIGLU_LIMIT = 7.0
SWIGLU_ALPHA = 1.702
D_PLE = 256
EPS = 1e-6

LANES = 128
HEAD_PAD = LANES
QK_SCALE = (QK_NOPE + QK_ROPE) ** -0.5
LOG2E = 1.4426950408889634

COL_CQ = 0
COL_CKV = COL_CQ + Q_LORA
COL_KR = COL_CKV + KV_LORA
COL_SG = COL_KR + LANES
IN_COLS_AUG = COL_SG + 2 * SG_WIDTH

TM = 1024
BQ = 512
ATTN_BK = 1024
ATTN_ONES_ROWS = 16
MOE_BM = 256
MOE_W_CHUNKS = 8
VMEM_LIMIT = 56 * 1024 * 1024

NT_DIMS = (((1,), (1,)), ((), ()))
ROW_WORDS = D_MODEL // 2
SC_CHUNK = 64
SC_DEST_CHUNK = 8192


def _rms(x, g):
    return x * lax.rsqrt(jnp.mean(x * x, axis=-1, keepdims=True) + EPS) * g


def _dot(a, b):
    return jnp.dot(a, b, preferred_element_type=F32)


def _dot_nt(a, b):
    return lax.dot_general(a, b, NT_DIMS, preferred_element_type=F32)


def _pack_halves(x):
    half = x.shape[1] // 2
    return pltpu.pack_elementwise([x[:, :half], x[:, half:]], packed_dtype=BF16)


def _unpack_halves(w):
    return (pltpu.unpack_elementwise(w, index=0, packed_dtype=BF16, unpacked_dtype=F32),
            pltpu.unpack_elementwise(w, index=1, packed_dtype=BF16, unpacked_dtype=F32))


def _inproj_kernel(x_ref, ang_ref, gpre_ref, win_ref, qg_ref, wq_ref, kvg_ref, wk_ref, wvt_ref,
                   sgg_ref, gsum_ref, wcat_ref, bsp_ref, sgo_ref,
                   q_out, k_out, vt_out, sg_out):
    x = x_ref[...]
    tm = x.shape[0]
    h = _rms(x, gpre_ref[...])
    z = _dot(h.astype(BF16), win_ref[...])

    ang = ang_ref[...]
    c, s = jnp.cos(ang), jnp.sin(ang)
    tail = HEAD_PAD - QK_NOPE - QK_ROPE
    cos = jnp.concatenate([jnp.ones((QK_NOPE, tm), F32), c, c, jnp.ones((tail, tm), F32)], axis=0).T
    sin = jnp.concatenate([jnp.zeros((QK_NOPE, tm), F32), s, s, jnp.zeros((tail, tm), F32)], axis=0).T
    first_half = lax.broadcasted_iota(I32, (tm, HEAD_PAD), 1) < QK_NOPE + QK_ROPE // 2
    sin_signed = jnp.where(first_half, -sin, sin)

    def rope(t):
        partner = jnp.where(first_half, pltpu.roll(t, HEAD_PAD - QK_ROPE // 2, 1), pltpu.roll(t, QK_ROPE // 2, 1))
        return t * cos + partner * sin_signed

    cqn = _rms(z[:, COL_CQ:COL_CQ + Q_LORA], qg_ref[...])
    qq = _dot(cqn.astype(BF16), wq_ref[...])
    for hd in range(HEADS):
        sl = slice(hd * HEAD_PAD, (hd + 1) * HEAD_PAD)
        q_out[:, sl] = (rope(qq[:, sl]) * (QK_SCALE * LOG2E)).astype(BF16)

    ckvn = _rms(z[:, COL_CKV:COL_CKV + KV_LORA], kvg_ref[...]).astype(BF16)
    kk = _dot(ckvn, wk_ref[...])
    kr = rope(z[:, COL_KR:COL_KR + LANES])
    for hd in range(HEADS):
        sl = slice(hd * HEAD_PAD, (hd + 1) * HEAD_PAD)
        k_out[:, sl] = (kk[:, sl] + kr).astype(BF16)
    vt_out[...] = _dot_nt(wvt_ref[...], ckvn).astype(BF16)

    zg = jax.nn.gelu(z[:, COL_SG:COL_SG + 2 * SG_WIDTH])
    u = zg[:, :SG_WIDTH]
    v = zg[:, SG_WIDTH:]
    ms = _dot((v * v).astype(BF16), gsum_ref[...]) * (1.0 / SG_HEAD_DIM)
    vn = v * lax.rsqrt(ms + EPS) * sgg_ref[...]

    row = lax.broadcasted_iota(I32, (SG_CHUNK, SG_HEADS * SG_CHUNK), 0)
    col = lax.broadcasted_iota(I32, (SG_CHUNK, SG_HEADS * SG_CHUNK), 1)
    wcat = jnp.where((col % SG_CHUNK) <= row, wcat_ref[...], 0.0).astype(BF16)
    lane_head = lax.broadcasted_iota(I32, (SG_CHUNK, SG_WIDTH), 1) // SG_HEAD_DIM
    bsp = bsp_ref[...]
    sgo = sgo_ref[...]
    for c in range(x.shape[0] // SG_CHUNK):
        rows = slice(c * SG_CHUNK, (c + 1) * SG_CHUNK)
        vc = vn[rows]
        vbd = jnp.concatenate(
            [jnp.where(lane_head == hd, vc, 0.0).astype(BF16) for hd in range(SG_HEADS)], axis=0)
        vm = _dot(wcat, vbd) + bsp
        sg_out[rows, :] = _rms(u[rows] * vm, sgo).astype(BF16)


def _inproj(x, ang, gpre, win, qg, wq, kvg, wk, wvt, sgg, gsum, wcat, bsp, sgo):
    n = x.shape[0]
    tok = lambda w: pl.BlockSpec((TM, w), lambda i: (i, 0))
    full = lambda a: pl.BlockSpec(a.shape, lambda i: (0,) * a.ndim)
    consts = (gpre, win, qg, wq, kvg, wk, wvt, sgg, gsum, wcat, bsp, sgo)
    return pl.pallas_call(
        _inproj_kernel,
        grid=(n // TM,),
        in_specs=[tok(D_MODEL), pl.BlockSpec((QK_ROPE // 2, TM), lambda i: (0, i))] + [full(a) for a in consts],
        out_specs=[tok(HEADS * HEAD_PAD), tok(HEADS * HEAD_PAD),
                   pl.BlockSpec((MLA_WIDTH, TM), lambda i: (0, i)), tok(SG_WIDTH)],
        out_shape=[jax.ShapeDtypeStruct((n, HEADS * HEAD_PAD), BF16),
                   jax.ShapeDtypeStruct((n, HEADS * HEAD_PAD), BF16),
                   jax.ShapeDtypeStruct((MLA_WIDTH, n), BF16),
                   jax.ShapeDtypeStruct((n, SG_WIDTH), BF16)],
        compiler_params=pltpu.CompilerParams(dimension_semantics=("arbitrary",),
                                             vmem_limit_bytes=VMEM_LIMIT),
        name="inproj",
    )(x, ang, *consts)


def _attn_kernel(q_ref, k_ref, vt_ref, g_ref, o_ref, m_sc, l_sc, acc_sc):
    i = pl.program_id(1)
    bq = q_ref.shape[0]
    m_sc[...] = jnp.full(m_sc.shape, -jnp.inf, F32)
    l_sc[...] = jnp.zeros(l_sc.shape, F32)
    acc_sc[...] = jnp.zeros(acc_sc.shape, F32)

    def qk(start, nk, hd):
        hs = slice(hd * HEAD_PAD, (hd + 1) * HEAD_PAD)
        return _dot_nt(k_ref[pl.ds(start, nk), hs], q_ref[:, hs])

    def block(start, nk, masked):
        ones = jnp.ones((ATTN_ONES_ROWS, nk), BF16)
        queue = [qk(start, nk, 0), qk(start, nk, 1)]
        for hd in range(HEADS):
            vs = slice(hd * V_HEAD, (hd + 1) * V_HEAD)
            st = queue.pop(0)
            if hd + 2 < HEADS:
                queue.append(qk(start, nk, hd + 2))
            if masked:
                causal = (lax.broadcasted_iota(I32, (nk, bq), 0) <= lax.broadcasted_iota(I32, (nk, bq), 1))
                st = jnp.where(causal, st, -jnp.inf)
            m_prev = m_sc[hd:hd + 1, :]
            m_new = jnp.maximum(m_prev, jnp.max(st, axis=0, keepdims=True))
            alpha = jnp.exp2(m_prev - m_new)
            p = jnp.exp2(st - m_new).astype(BF16)
            vta = jnp.concatenate([vt_ref[vs, pl.ds(start, nk)], ones], axis=0)
            pv = _dot(vta, p)
            l_sc[hd:hd + 1, :] = alpha * l_sc[hd:hd + 1, :] + pv[V_HEAD:V_HEAD + 1, :]
            acc_sc[vs, :] = alpha * acc_sc[vs, :] + pv[:V_HEAD, :]
            m_sc[hd:hd + 1, :] = m_new

    def body(kb, carry):
        block(pl.multiple_of(kb * ATTN_BK, ATTN_BK), ATTN_BK, False)
        return carry

    visible = i * bq
    lax.fori_loop(0, visible // ATTN_BK, body, 0)
    for r in range(ATTN_BK // bq - 1, 0, -1):

        @pl.when(visible % ATTN_BK >= r * bq)
        def _():
            block(pl.multiple_of((i - r) * bq, bq), bq, False)

    block(pl.multiple_of(i * bq, bq), bq, True)
    ot = jnp.concatenate(
        [acc_sc[hd * V_HEAD:(hd + 1) * V_HEAD, :] * (1.0 / l_sc[hd:hd + 1, :]) for hd in range(HEADS)], axis=0)
    o_ref[...] = _rms(ot.T, g_ref[...]).astype(BF16)


def _attention(q, k, vt, g, b, s):
    nq = s // BQ
    return pl.pallas_call(
        _attn_kernel,
        grid=(b, nq),
        in_specs=[pl.BlockSpec((BQ, HEADS * HEAD_PAD), lambda bi, i: (bi * nq + i, 0)),
                  pl.BlockSpec((s, HEADS * HEAD_PAD), lambda bi, i: (bi, 0), pipeline_mode=pl.Buffered(1)),
                  pl.BlockSpec((MLA_WIDTH, s), lambda bi, i: (0, bi), pipeline_mode=pl.Buffered(1)),
                  pl.BlockSpec(g.shape, lambda bi, i: (0, 0))],
        out_specs=pl.BlockSpec((BQ, MLA_WIDTH), lambda bi, i: (bi * nq + i, 0)),
        out_shape=jax.ShapeDtypeStruct((b * s, MLA_WIDTH), BF16),
        scratch_shapes=[pltpu.VMEM((HEADS, BQ), F32), pltpu.VMEM((HEADS, BQ), F32),
                        pltpu.VMEM((MLA_WIDTH, BQ), F32)],
        compiler_params=pltpu.CompilerParams(dimension_semantics=("arbitrary", "arbitrary"),
                                             vmem_limit_bytes=VMEM_LIMIT),
        name="attention",
    )(q, k, vt, g)


def _postattn_kernel(x_ref, mla_ref, sg_ref, wof_ref, gpost_ref, gffn_ref, wrt_ref, brt_ref,
                     x1_out, h2_out, idx_out, gate_out, rank_out, cnt_out, cnt_sc, wo_sc):
    i = pl.program_id(0)

    @pl.when(i == 0)
    def _():
        cnt_sc[...] = jnp.zeros(cnt_sc.shape, F32)
        wo_sc[...] = wof_ref[...].astype(BF16)

    a = _dot(mla_ref[...], wo_sc[:MLA_WIDTH, :]) + _dot(sg_ref[...], wo_sc[MLA_WIDTH:, :])
    x1 = x_ref[...] + _rms(a, gpost_ref[...])
    x1_out[...] = x1
    h2f = _rms(x1, gffn_ref[...])
    h2_out[...] = _pack_halves(h2f)
    h2 = h2f.astype(BF16)

    tm = h2.shape[0]
    logits = _dot_nt(wrt_ref[...], h2) + brt_ref[...]
    eidx = lax.broadcasted_iota(I32, (N_EXPERTS, tm), 0)
    vals, idxs, sels = [], [], []
    for _ in range(TOP_K):
        mx = jnp.max(logits, axis=0, keepdims=True)
        ik = jnp.min(jnp.where(logits == mx, eidx, N_EXPERTS), axis=0, keepdims=True)
        sel = eidx == ik
        vals.append(mx)
        idxs.append(ik)
        sels.append(sel)
        logits = jnp.where(sel, -jnp.inf, logits)
    ex = [jnp.exp(v - vals[0]) for v in vals]
    den = ex[0] + ex[1] + ex[2] + ex[3]
    gate_out[...] = jnp.concatenate([e / den for e in ex], axis=0)
    idx_out[...] = jnp.concatenate(idxs, axis=0)

    maskf = sum(jnp.where(s, 1.0, 0.0) for s in sels)
    before = (lax.broadcasted_iota(I32, (tm, tm), 0) < lax.broadcasted_iota(I32, (tm, tm), 1))
    prefix = _dot(maskf.astype(BF16), jnp.where(before, 1.0, 0.0).astype(BF16))
    base = cnt_sc[...]
    tot = base + prefix
    ranks = [jnp.sum(jnp.where(s, tot, 0.0), axis=0, keepdims=True) for s in sels]
    rank_out[...] = jnp.concatenate(ranks, axis=0).astype(I32)
    cnt = base + jnp.sum(maskf, axis=1, keepdims=True)
    cnt_sc[...] = cnt
    cnt_out[...] = jnp.broadcast_to(cnt, cnt_out.shape)


def _postattn(x, mla, sg, wo, gpost, gffn, wrt, brt):
    n = x.shape[0]
    tok = lambda w: pl.BlockSpec((TM, w), lambda i: (i, 0))
    tokt = pl.BlockSpec((TOP_K, TM), lambda i: (0, i))
    full = lambda a: pl.BlockSpec(a.shape, lambda i: (0,) * a.ndim)
    consts = (wo, gpost, gffn, wrt, brt)
    return pl.pallas_call(
        _postattn_kernel,
        grid=(n // TM,),
        in_specs=[tok(D_MODEL), tok(MLA_WIDTH), tok(SG_WIDTH)] + [full(a) for a in consts],
        out_specs=[tok(D_MODEL), tok(ROW_WORDS), tokt, tokt, tokt,
                   pl.BlockSpec((N_EXPERTS, LANES), lambda i: (0, 0))],
        out_shape=[jax.ShapeDtypeStruct((n, D_MODEL), F32),
                   jax.ShapeDtypeStruct((n, ROW_WORDS), I32),
                   jax.ShapeDtypeStruct((TOP_K, n), I32),
                   jax.ShapeDtypeStruct((TOP_K, n), F32),
                   jax.ShapeDtypeStruct((TOP_K, n), I32),
                   jax.ShapeDtypeStruct((N_EXPERTS, LANES), F32)],
        scratch_shapes=[pltpu.VMEM((N_EXPERTS, 1), F32), pltpu.VMEM(wo.shape, BF16)],
        compiler_params=pltpu.CompilerParams(dimension_semantics=("arbitrary",),
                                             vmem_limit_bytes=VMEM_LIMIT),
        name="postattn",
    )(x, mla, sg, *consts)


def _sc_workers():
    info = plsc.get_sparse_core_info()
    return info.num_cores, info.num_cores * info.num_subcores, info.num_lanes


def _sc_stream_rows(table_hbm, idx_v, out_hbm, base, n_chunks, buf, sem_g, sem_w):
    def gather(j, b):
        rows = idx_v.at[pl.ds(pl.multiple_of(j * SC_CHUNK, SC_CHUNK), SC_CHUNK)]
        return pltpu.make_async_copy(table_hbm.at[rows], buf.at[b], sem_g.at[b])

    def write(j, b):
        rows = pl.ds(pl.multiple_of(base + j * SC_CHUNK, SC_CHUNK), SC_CHUNK)
        return pltpu.make_async_copy(buf.at[b], out_hbm.at[rows], sem_w.at[b])

    gather(0, 0).start()

    @pl.loop(0, n_chunks, step=2)
    def _(j0):
        for b in range(2):
            j = j0 + b
            gather(j, b).wait()

            @pl.when(j >= 1)
            def _():
                write(j - 1, 1 - b).wait()

            @pl.when(j + 1 < n_chunks)
            def _():
                gather(j + 1, 1 - b).start()

            write(j, b).start()

    write(n_chunks - 1, 1).wait()


def _sc_gather(table, idx):
    n_out = idx.shape[0]
    width = table.shape[1]
    num_cores, workers, _ = _sc_workers()
    per_w = n_out // workers
    n_chunks = per_w // SC_CHUNK
    assert per_w * workers == n_out and n_chunks * SC_CHUNK == per_w and n_chunks % 2 == 0

    @functools.partial(
        pl.kernel, mesh=plsc.VectorSubcoreMesh(core_axis_name="c", subcore_axis_name="s"),
        out_type=jax.ShapeDtypeStruct((n_out, width), table.dtype),
        scratch_types=[pltpu.VMEM((per_w,), I32), pltpu.VMEM((2, SC_CHUNK, width), table.dtype),
                       pltpu.SemaphoreType.DMA((2,)), pltpu.SemaphoreType.DMA((2,))])
    def gather_kernel(table_hbm, idx_hbm, out_hbm, idx_v, buf, sem_g, sem_w):
        wid = lax.axis_index("s") * num_cores + lax.axis_index("c")
        base = pl.multiple_of(wid * per_w, SC_CHUNK)
        pltpu.sync_copy(idx_hbm.at[pl.ds(base, per_w)], idx_v)
        _sc_stream_rows(table_hbm, idx_v, out_hbm, base, n_chunks, buf, sem_g, sem_w)

    return gather_kernel(table, idx)


def _sc_dispatch(table, dest, n_out):
    n_assign = dest.shape[0]
    n, width = table.shape
    num_cores, workers, lanes = _sc_workers()
    per_w = n_out // workers
    n_chunks = per_w // SC_CHUNK
    assert per_w * workers == n_out and n_chunks * SC_CHUNK == per_w and n_chunks % 2 == 0
    assert n_assign % SC_DEST_CHUNK == 0 and SC_DEST_CHUNK % lanes == 0 and per_w % lanes == 0

    @functools.partial(
        pl.kernel, mesh=plsc.VectorSubcoreMesh(core_axis_name="c", subcore_axis_name="s"),
        out_type=jax.ShapeDtypeStruct((n_out, width), table.dtype),
        scratch_types=[pltpu.VMEM((per_w,), I32), pltpu.VMEM((SC_DEST_CHUNK,), I32),
                       pltpu.VMEM((2, SC_CHUNK, width), table.dtype),
                       pltpu.SemaphoreType.DMA((2,)), pltpu.SemaphoreType.DMA((2,))],
        compiler_params=pltpu.CompilerParams(needs_layout_passes=False))
    def dispatch_kernel(table_hbm, dest_hbm, out_hbm, tok_v, dest_v, buf, sem_g, sem_w):
        wid = lax.axis_index("s") * num_cores + lax.axis_index("c")
        base = pl.multiple_of(wid * per_w, SC_CHUNK)
        lane = lax.iota(I32, lanes)

        @pl.loop(0, per_w, step=lanes)
        def _(r):
            tok_v[pl.ds(r, lanes)] = lax.rem(base + r + lane, n)

        @pl.loop(0, n_assign // SC_DEST_CHUNK)
        def _(c):
            first = pl.multiple_of(c * SC_DEST_CHUNK, SC_DEST_CHUNK)
            pltpu.sync_copy(dest_hbm.at[pl.ds(first, SC_DEST_CHUNK)], dest_v)

            @plsc.parallel_loop(0, SC_DEST_CHUNK, step=lanes, unroll=4)
            def _(i):
                local = dest_v[pl.ds(i, lanes)] - base
                mine = (local >= 0) & (local < per_w)
                tok = lax.rem(first + i + lane, n)
                plsc.store_scatter(tok_v, [jnp.where(mine, local, 0)], tok, mask=mine)

        _sc_stream_rows(table_hbm, tok_v, out_hbm, base, n_chunks, buf, sem_g, sem_w)

    return dispatch_kernel(table, dest)


def _moe_kernel(nb_ref, row0_ref, x_hbm, wgu_hbm, bgu_ref, wd_hbm, bd_ref, y_hbm,
                wgu_f, wd_f, wgu_sc, wd_sc, xbuf, ybuf, sem_x, sem_y, sem_w):
    e = pl.program_id(0)
    n_e = pl.num_programs(0)
    nb = nb_ref[e]
    row0 = row0_ref[e]
    wslot = e % 2
    chunk_rows = D_MODEL // MOE_W_CHUNKS

    def w_copies(expert, c, slot):
        r = pl.ds(c * chunk_rows, chunk_rows)
        return (pltpu.make_async_copy(wgu_hbm.at[expert, r, :], wgu_f.at[slot, r, :], sem_w.at[slot]),
                pltpu.make_async_copy(wd_hbm.at[expert, r, :], wd_f.at[slot, r, :], sem_w.at[slot]))

    def start_chunk(expert, c, slot):
        for cp in w_copies(expert, c, slot):
            cp.start()

    def rows(j):
        return pl.ds(pl.multiple_of(row0 + j * MOE_BM, MOE_BM), MOE_BM)

    def x_copy(j, slot):
        return pltpu.make_async_copy(x_hbm.at[rows(j)], xbuf.at[slot], sem_x.at[slot])

    def y_copy(j, slot):
        return pltpu.make_async_copy(ybuf.at[slot], y_hbm.at[rows(j)], sem_y.at[slot])

    @pl.when(e == 0)
    def _():
        for c in range(MOE_W_CHUNKS):
            start_chunk(0, c, 0)

    @pl.when(nb > 0)
    def _():
        x_copy(0, 0).start()

    for c in range(MOE_W_CHUNKS):
        for cp in w_copies(e, c, wslot):
            cp.wait()
    wgu_sc[...] = wgu_f[wslot].astype(BF16)
    wd_sc[...] = wd_f[wslot].astype(BF16)
    nxt = jnp.minimum(e + 1, n_e - 1)
    has_next = e + 1 < n_e

    def body(j, carry):
        slot = j % 2

        @pl.when(has_next & (j < MOE_W_CHUNKS))
        def _():
            start_chunk(nxt, j, 1 - wslot)

        x_copy(j, slot).wait()

        @pl.when(j + 1 < nb)
        def _():
            x_copy(j + 1, 1 - slot).start()

        @pl.when(j >= 2)
        def _():
            y_copy(j - 2, slot).wait()

        x = jnp.concatenate(_unpack_halves(xbuf[slot]), axis=1).astype(BF16)
        gu = _dot(x, wgu_sc[...]) + bgu_ref[0]
        g = jnp.minimum(gu[:, :D_MODEL], SWIGLU_LIMIT)
        u = jnp.clip(gu[:, D_MODEL:], -SWIGLU_LIMIT, SWIGLU_LIMIT)
        act = (u + 1.0) * (g * jax.nn.sigmoid(SWIGLU_ALPHA * g))
        ybuf[slot] = _pack_halves(_dot(act.astype(BF16), wd_sc[...]) + bd_ref[0])
        y_copy(j, slot).start()
        return carry

    lax.fori_loop(0, nb, body, 0)

    def rest(c, carry):
        @pl.when(has_next)
        def _():
            start_chunk(nxt, c, 1 - wslot)
        return carry

    lax.fori_loop(jnp.minimum(nb, MOE_W_CHUNKS), MOE_W_CHUNKS, rest, 0)

    @pl.when(nb >= 2)
    def _():
        y_copy(nb - 2, nb % 2).wait()

    @pl.when(nb >= 1)
    def _():
        y_copy(nb - 1, (nb - 1) % 2).wait()


def _moe(blocks_of, row0_of, x_rows, wgu, bgu, wd, bd):
    hbm = pl.BlockSpec(memory_space=pl.ANY)
    grid_spec = pltpu.PrefetchScalarGridSpec(
        num_scalar_prefetch=2,
        grid=(N_EXPERTS,),
        in_specs=[hbm, hbm,
                  pl.BlockSpec((1, 1, 2 * D_MODEL), lambda e, nb, r0: (e, 0, 0)),
                  hbm,
                  pl.BlockSpec((1, 1, D_MODEL), lambda e, nb, r0: (e, 0, 0))],
        out_specs=hbm,
        scratch_shapes=[pltpu.VMEM((2, D_MODEL, 2 * D_MODEL), F32), pltpu.VMEM((2, D_MODEL, D_MODEL), F32),
                        pltpu.VMEM((D_MODEL, 2 * D_MODEL), BF16), pltpu.VMEM((D_MODEL, D_MODEL), BF16),
                        pltpu.VMEM((2, MOE_BM, ROW_WORDS), I32), pltpu.VMEM((2, MOE_BM, ROW_WORDS), I32),
                        pltpu.SemaphoreType.DMA((2,)), pltpu.SemaphoreType.DMA((2,)),
                        pltpu.SemaphoreType.DMA((2,))],
    )
    return pl.pallas_call(
        _moe_kernel,
        grid_spec=grid_spec,
        out_shape=jax.ShapeDtypeStruct(x_rows.shape, I32),
        compiler_params=pltpu.CompilerParams(dimension_semantics=("arbitrary",),
                                             vmem_limit_bytes=VMEM_LIMIT),
        name="moe",
    )(blocks_of, row0_of, x_rows, wgu, bgu, wd, bd)


def _final_kernel(x1_ref, yg_ref, gt_ref, p_ref, gpost_ref, wgf_ref, bg_ref, wpf_ref, gple_ref, o_ref,
                  wg_ref, wp_ref):
    @pl.when(pl.program_id(0) == 0)
    def _():
        wg_ref[...] = wgf_ref[...].astype(BF16)
        wp_ref[...] = wpf_ref[...].astype(BF16)

    gt = gt_ref[...]
    lo, hi = _unpack_halves(yg_ref[0])
    y = gt[:, 0:1] * jnp.concatenate([lo, hi], axis=1)
    for k in range(1, TOP_K):
        lo, hi = _unpack_halves(yg_ref[k])
        y = y + gt[:, k:k + 1] * jnp.concatenate([lo, hi], axis=1)
    x2 = x1_ref[...] + _rms(y, gpost_ref[...])
    gate = jax.nn.sigmoid(_dot(x2.astype(BF16), wg_ref[...]) + bg_ref[...])
    pp = _dot(p_ref[...].astype(BF16), wp_ref[...])
    o_ref[...] = x2 + _rms(gate * pp, gple_ref[...])


def _final(x1, yg, gate_t, p, gpost, wg, bg, wp, gple):
    n = x1.shape[0]
    tok = lambda w: pl.BlockSpec((TM, w), lambda i: (i, 0))
    full = lambda a: pl.BlockSpec(a.shape, lambda i: (0,) * a.ndim)
    consts = (gpost, wg, bg, wp, gple)
    return pl.pallas_call(
        _final_kernel,
        grid=(n // TM,),
        in_specs=[tok(D_MODEL), pl.BlockSpec((TOP_K, TM, ROW_WORDS), lambda i: (0, i, 0)), tok(TOP_K),
                  tok(D_PLE)] + [full(a) for a in consts],
        out_specs=tok(D_MODEL),
        out_shape=jax.ShapeDtypeStruct((n, D_MODEL), F32),
        scratch_shapes=[pltpu.VMEM(wg.shape, BF16), pltpu.VMEM(wp.shape, BF16)],
        compiler_params=pltpu.CompilerParams(dimension_semantics=("arbitrary",),
                                             vmem_limit_bytes=VMEM_LIMIT),
        name="final",
    )(x1, yg, gate_t, p, *consts)


def _rope_angles(positions):
    inv_freq = 1.0 / (ROPE_THETA ** (jnp.arange(0, QK_ROPE, 2, dtype=F32) / QK_ROPE))
    return inv_freq[:, None] * positions.astype(F32).reshape(1, -1)


def _pad_head(w):
    return jnp.pad(w, [(0, 0)] * (w.ndim - 1) + [(0, HEAD_PAD - w.shape[-1])])


def _layer(x, p_l, ang, prm):
    (attn_pre_g, w_in, q_norm_g, w_uq, kv_norm_g, w_ukv, sg_norm_g, w_spatial, b_spatial, mla_out_g,
     sg_out_g, w_out, attn_post_g, ffn_pre_g, w_router, b_router, w_gate_up, b_gate_up, w_down, b_down,
     ffn_post_g, w_ple_gate, b_ple_gate, w_ple_proj, ple_norm_g) = prm
    b, s, _ = x.shape
    n = b * s
    xt = x.reshape(n, D_MODEL)
    row2 = lambda a: a.reshape(1, -1)

    w_kr = w_in[:, Q_LORA + KV_LORA:Q_LORA + KV_LORA + QK_ROPE]
    place = lambda w: jnp.pad(w, ((0, 0), (QK_NOPE, HEAD_PAD - QK_NOPE - QK_ROPE)))
    win_aug = jnp.concatenate(
        [w_in[:, :Q_LORA + KV_LORA], place(w_kr), w_in[:, Q_LORA + KV_LORA + QK_ROPE:]], axis=-1).astype(BF16)
    wq_aug = _pad_head(w_uq).reshape(Q_LORA, -1).astype(BF16)
    wk_pad = _pad_head(w_ukv[..., :QK_NOPE]).reshape(KV_LORA, -1).astype(BF16)
    wv_t = w_ukv[..., QK_NOPE:].reshape(KV_LORA, -1).T.astype(BF16)
    head_of = jnp.arange(SG_WIDTH) // SG_HEAD_DIM
    gsum = (head_of[:, None] == head_of[None, :]).astype(BF16)
    wcat = w_spatial.transpose(1, 0, 2).reshape(SG_CHUNK, SG_HEADS * SG_CHUNK)
    bsp = jnp.repeat(b_spatial.T, SG_HEAD_DIM, axis=1)

    q, k, vt, mix_sg = _inproj(xt, ang, row2(attn_pre_g), win_aug, row2(q_norm_g), wq_aug,
                               row2(kv_norm_g), wk_pad, wv_t, row2(sg_norm_g), gsum, wcat, bsp,
                               row2(sg_out_g))
    mix_mla = _attention(q, k, vt, row2(mla_out_g), b, s)

    x1, h2, idx, gate, rank, cnt = _postattn(
        xt, mix_mla, mix_sg, w_out, row2(attn_post_g), row2(ffn_pre_g),
        w_router.T.astype(BF16), b_router.reshape(N_EXPERTS, 1))

    a = n * TOP_K
    counts = cnt[:, 0].astype(I32)
    padded = (counts + MOE_BM - 1) // MOE_BM * MOE_BM
    pad_end = jnp.cumsum(padded)
    pad_start = pad_end - padded
    start_of = jnp.sum(jnp.where(idx[..., None] == jnp.arange(N_EXPERTS, dtype=I32), pad_start, 0), axis=-1)
    dest = (start_of + rank).reshape(a)
    n_rows = a + N_EXPERTS * MOE_BM

    x_rows = _sc_dispatch(h2, dest, n_rows)
    y_rows = _moe(padded // MOE_BM, pad_start, x_rows, w_gate_up, b_gate_up[:, None, :],
                  w_down, b_down[:, None, :])
    yg = _sc_gather(y_rows, dest).reshape(TOP_K, n, ROW_WORDS)

    out = _final(x1, yg, gate.T, p_l.reshape(n, D_PLE), row2(ffn_post_g), w_ple_gate,
                 row2(b_ple_gate), w_ple_proj, row2(ple_norm_g))
    return out.reshape(b, s, D_MODEL)


def kernel(x, p, positions, attn_pre_g, w_in, q_norm_g, w_uq, kv_norm_g, w_ukv, sg_norm_g, w_spatial, b_spatial, mla_out_g, sg_out_g, w_out, attn_post_g, ffn_pre_g, w_router, b_router, w_gate_up, b_gate_up, w_down, b_down, ffn_post_g, w_ple_gate, b_ple_gate, w_ple_proj, ple_norm_g):
    ang = _rope_angles(positions)
    params = (attn_pre_g, w_in, q_norm_g, w_uq, kv_norm_g, w_ukv, sg_norm_g, w_spatial, b_spatial, mla_out_g,
              sg_out_g, w_out, attn_post_g, ffn_pre_g, w_router, b_router, w_gate_up, b_gate_up, w_down, b_down,
              ffn_post_g, w_ple_gate, b_ple_gate, w_ple_proj, ple_norm_g)
    for layer in range(p.shape[0]):
        x = _layer(x, p[layer], ang, tuple(a[layer] for a in params))
    return x
```
